```python
import jax, jax.numpy as jnp
from jax import lax
import numpy as np

D_MODEL = 1024
BATCH = 8
SEQ = 4096
DEPTH = 1

HEAD_DIM = 64
N_HEADS = D_MODEL // HEAD_DIM
N_HEADS_A = N_HEADS // 2
N_HEADS_B = N_HEADS - N_HEADS_A
DILATED_CONFIGS = ((128, 1), (512, 4), (2048, 16))
BLOCK = 128
KV_RANK = D_MODEL // 8
IDX_HEADS = 8
IDX_DIM = 64
IDX_SCALE = (IDX_HEADS * IDX_DIM) ** -0.5
TOPK_MAX = 256
D_FF = -(-8 * D_MODEL // (3 * 256)) * 256
EPS = 1e-6
SPLITS = (N_HEADS_A * HEAD_DIM, N_HEADS_A * HEAD_DIM, N_HEADS_A * HEAD_DIM,
          N_HEADS_B * HEAD_DIM, KV_RANK,
          IDX_HEADS * IDX_DIM, IDX_DIM, IDX_HEADS)
D_IN = sum(SPLITS)

kernel_name = "hybrid_dilated_swa_dsa_block"


def rmsnorm(x, g):
    xf = x.astype(jnp.float32)
    y = xf * lax.rsqrt(jnp.mean(xf * xf, axis=-1, keepdims=True) + EPS)
    return (y * g.astype(jnp.float32)).astype(x.dtype)


def alibi_slopes():
    s = 2.0 ** (-8.0 * (np.arange(N_HEADS, dtype=np.float32) + 1.0) / N_HEADS)
    return jnp.asarray(s[0::2], jnp.float32), jnp.asarray(s[1::2], jnp.float32)


def dilated_branch(q, k, v, window, dilation, slopes):
    bsz, seq, nh, hd = q.shape
    n = seq // dilation
    nb = -(-n // BLOCK)
    n_pad = nb * BLOCK
    w_res = window // dilation
    assert w_res <= BLOCK
    g = bsz * dilation

    def to_res(a):
        a = a.reshape(bsz, n, dilation, nh, hd).transpose(0, 2, 3, 1, 4).reshape(g, nh, n, hd)
        return jnp.pad(a, ((0, 0), (0, 0), (0, n_pad - n), (0, 0)))

    qr, kr, vr = to_res(q), to_res(k), to_res(v)

    def band(a):
        cur = a.reshape(g, nh, nb, BLOCK, hd)
        prev = jnp.pad(a, ((0, 0), (0, 0), (BLOCK, 0), (0, 0)))[:, :, :n_pad].reshape(g, nh, nb, BLOCK, hd)
        return jnp.concatenate([prev, cur], axis=3)

    kb, vb = band(kr), band(vr)
    qb = qr.reshape(g, nh, nb, BLOCK, hd)
    s = jnp.einsum('ghnqd,ghnkd->ghnqk', qb, kb).astype(jnp.float32) * (hd ** -0.5)
    dist = np.arange(BLOCK)[:, None] - np.arange(2 * BLOCK)[None, :] + BLOCK
    key_pos = np.arange(nb)[:, None] * BLOCK - BLOCK + np.arange(2 * BLOCK)[None, :]
    mask = ((dist >= 0) & (dist <= w_res))[None] & (key_pos >= 0)[:, None, :]
    bias = -slopes[:, None, None] * jnp.asarray(dist * dilation, jnp.float32)[None]
    s = jnp.where(mask[None, None], s + bias[None, :, None], -jnp.inf)
    m = jnp.max(s, axis=-1, keepdims=True)
    p = jnp.exp(s - m)
    l = jnp.sum(p, axis=-1, keepdims=True)
    o = jnp.einsum('ghnqk,ghnkd->ghnqd', (p / l).astype(v.dtype), vb)
    lse = (m + jnp.log(l))[..., 0]
    o = o.reshape(g, nh, n_pad, hd)[:, :, :n].reshape(bsz, dilation, nh, n, hd)
    o = o.transpose(0, 3, 1, 2, 4).reshape(bsz, seq, nh, hd)
    lse = lse.reshape(g, nh, n_pad)[:, :, :n].reshape(bsz, dilation, nh, n)
    lse = lse.transpose(0, 3, 1, 2).reshape(bsz, seq, nh)
    return o, lse


def dilated_mixture(q, k, v, slopes):
    outs, lses = [], []
    for window, dilation in DILATED_CONFIGS:
        o, lse = dilated_branch(q, k, v, window, dilation, slopes)
        outs.append(o.astype(jnp.float32))
        lses.append(lse)
    alpha = jax.nn.softmax(jnp.stack(lses, 0), axis=0)
    out = jnp.sum(alpha[..., None] * jnp.stack(outs, 0), axis=0)
    return out.astype(q.dtype)


def sparse_attention(q_idx, k_idx, w_idx, q_lat, c_kv, slopes):
    bsz, seq = c_kv.shape[0], c_kv.shape[1]
    nb = seq // BLOCK
    topk = min(TOPK_MAX, seq // 4)
    s_pos = jnp.arange(seq)
    k_idx_f = k_idx.astype(jnp.float32)

    def block_fn(i):
        start = i * BLOCK
        qi = lax.dynamic_slice_in_dim(q_idx, start, BLOCK, axis=1).astype(jnp.float32)
        wi = lax.dynamic_slice_in_dim(w_idx, start, BLOCK, axis=1).astype(jnp.float32)
        ql = lax.dynamic_slice_in_dim(q_lat, start, BLOCK, axis=1)
        t_pos = start + jnp.arange(BLOCK)
        logits = jnp.einsum('bqhd,bkd->bqhk', qi, k_idx_f)
        score = jnp.einsum('bqh,bqhk->bqk', wi, jax.nn.relu(logits))
        score = jnp.where(s_pos[None, None, :] <= t_pos[None, :, None], score, -jnp.inf)
        _, idx = lax.top_k(score, topk)
        c_sel = jax.vmap(lambda cb, ib: cb[ib])(c_kv, idx)
        s = jnp.einsum('bqhr,bqkr->bhqk', ql, c_sel).astype(jnp.float32) * (HEAD_DIM ** -0.5)
        dist = (t_pos[None, :, None] - idx).astype(jnp.float32)
        s = s - slopes[None, :, None, None] * dist[:, None]
        s = jnp.where((idx <= t_pos[None, :, None])[:, None], s, -jnp.inf)
        p = jax.nn.softmax(s, axis=-1)
        return jnp.einsum('bhqk,bqkr->bqhr', p.astype(c_sel.dtype), c_sel)

    o = lax.map(block_fn, jnp.arange(nb))
    return o.transpose(1, 0, 2, 3, 4).reshape(bsz, seq, o.shape[3], o.shape[4])


def setup_inputs(seed: int = 0) -> dict:
    key = jax.random.key(seed)
    ks = jax.random.split(key, 16)
    f32 = jnp.float32
    nrm = lambda k, shape, scale: jax.random.normal(k, shape, f32) * scale
    return {
        "x": nrm(ks[0], (BATCH, SEQ, D_MODEL), 1.0),
        "c": nrm(ks[1], (BATCH, D_MODEL), 1.0),
        "w_ada": nrm(ks[2], (DEPTH, D_MODEL, 6 * D_MODEL), 0.5 * D_MODEL ** -0.5),
        "b_ada": nrm(ks[3], (DEPTH, 6 * D_MODEL), 0.01),
        "g_attn": 1.0 + nrm(ks[4], (DEPTH, D_MODEL), 0.02),
        "w_in": nrm(ks[5], (DEPTH, D_MODEL, D_IN), D_MODEL ** -0.5),
        "kv_norm_g": 1.0 + nrm(ks[6], (DEPTH, KV_RANK), 0.02),
        "w_uk": nrm(ks[7], (DEPTH, N_HEADS_B, HEAD_DIM, KV_RANK), HEAD_DIM ** -0.5),
        "w_uv": nrm(ks[8], (DEPTH, N_HEADS_B, KV_RANK, HEAD_DIM), KV_RANK ** -0.5),
        "w_out": nrm(ks[9], (DEPTH, D_MODEL, D_MODEL), D_MODEL ** -0.5),
        "g_ffn": 1.0 + nrm(ks[10], (DEPTH, D_MODEL), 0.02),
        "w_gu": nrm(ks[11], (DEPTH, D_MODEL, 2 * D_FF), D_MODEL ** -0.5),
        "w_down": nrm(ks[12], (DEPTH, D_FF, D_MODEL), D_FF ** -0.5),
        "g_final": 1.0 + nrm(ks[13], (D_MODEL,), 0.02),
    }


def reference(x, c, w_ada, b_ada, g_attn, w_in, kv_norm_g, w_uk, w_uv, w_out, g_ffn, w_gu, w_down, g_final):
    bsz, seq, _ = x.shape
    slopes_a, slopes_b = alibi_slopes()
    offsets = [int(o) for o in np.cumsum(SPLITS)[:-1]]
    c_act = jax.nn.silu(c)
    for l in range(DEPTH):
        mod = c_act @ w_ada[l] + b_ada[l]
        sh1, sc1, ga1, sh2, sc2, ga2 = [m[:, None, :] for m in jnp.split(mod, 6, axis=-1)]

        h = rmsnorm(x, g_attn[l]) * (1.0 + sc1) + sh1
        proj = h @ w_in[l]
        qa, ka, va, qb, ckv, qi, ki, wi = jnp.split(proj, offsets, axis=-1)
        shp_a = (bsz, seq, N_HEADS_A, HEAD_DIM)
        out_a = dilated_mixture(qa.reshape(shp_a), ka.reshape(shp_a), va.reshape(shp_a), slopes_a)
        ckv = rmsnorm(ckv, kv_norm_g[l])
        q_lat = jnp.einsum('bthd,hdr->bthr', qb.reshape(bsz, seq, N_HEADS_B, HEAD_DIM), w_uk[l])
        o_lat = sparse_attention(qi.reshape(bsz, seq, IDX_HEADS, IDX_DIM), ki, wi * IDX_SCALE,
                                 q_lat, ckv, slopes_b)
        out_b = jnp.einsum('bthr,hrd->bthd', o_lat, w_uv[l])
        mixed = jnp.concatenate([out_a.reshape(bsz, seq, -1), out_b.reshape(bsz, seq, -1)], axis=-1)
        x = x + ga1 * (mixed @ w_out[l])

        h2 = rmsnorm(x, g_ffn[l]) * (1.0 + sc2) + sh2
        gate, up = jnp.split(h2 @ w_gu[l], 2, axis=-1)
        x = x + ga2 * ((jax.nn.silu(gate) * up) @ w_down[l])
    return rmsnorm(x, g_final)
```

```python
import functools

import numpy as np
import jax
import jax.numpy as jnp
from jax import lax
from jax.experimental import pallas as pl
from jax.experimental.pallas import tpu as pltpu

F32 = jnp.float32
BF16 = jnp.bfloat16

HEAD_DIM = 64
BLOCK = 128
DILATED_CONFIGS = ((128, 1), (512, 4), (2048, 16))
IDX_HEADS = 8
IDX_DIM = 64
TOPK_MAX = 256
EPS = 1e-6
NEG = -1e30
LANES = 128
VMEM_LIMIT = 56 * 1024 * 1024

_NT = (((1,), (1,)), ((), ()))


def _cparams(sem):
    return pltpu.CompilerParams(dimension_semantics=sem, vmem_limit_bytes=VMEM_LIMIT)


def _const_spec(shape):
    nd = len(shape)
    return pl.BlockSpec(shape, lambda *_: (0,) * nd, pipeline_mode=pl.Buffered(1))


def _mod_kernel(c_ref, w_ref, b_ref, o_ref):
    c = c_ref[...]
    c_act = c * (1.0 / (1.0 + jnp.exp(-c)))
    o_ref[...] = jnp.dot(c_act, w_ref[...], preferred_element_type=F32,
                         precision=lax.Precision.HIGHEST) + b_ref[...]


def _mod_call(c, w_ada, b_ada):
    bsz, d = c.shape
    n = w_ada.shape[1]
    tn = 1024
    return pl.pallas_call(
        _mod_kernel,
        grid=(n // tn,),
        in_specs=[pl.BlockSpec((bsz, d), lambda j: (0, 0)),
                  pl.BlockSpec((d, tn), lambda j: (0, j)),
                  pl.BlockSpec((1, tn), lambda j: (0, j))],
        out_specs=pl.BlockSpec((bsz, tn), lambda j: (0, j)),
        out_shape=jax.ShapeDtypeStruct((bsz, n), F32),
        compiler_params=_cparams(("arbitrary",)),
        name="adaln_mod",
    )(c, w_ada, b_ada.reshape(1, n))


def _rms(x):
    return x * lax.rsqrt(jnp.mean(x * x, axis=-1, keepdims=True) + EPS)


def _in_kernel(x_ref, mod_ref, g_ref, w_ref, wuk_ref, kvg_ref,
               qa_ref, ka_ref, va_ref, ql_ref, ckv_ref, qi_ref, ki_ref, wi_ref, *, nh, idx_scale):
    hd = HEAD_DIM
    h = _rms(x_ref[0]) * g_ref[...]
    h = h * (1.0 + mod_ref[0, 1:2, :]) + mod_ref[0, 0:1, :]
    proj = jnp.dot(h.astype(BF16), w_ref[...], preferred_element_type=F32)
    o_q, o_k, o_v, o_qb = 0, nh * hd, 2 * nh * hd, 3 * nh * hd
    o_ckv = 4 * nh * hd
    rank = ckv_ref.shape[-1]
    o_qi = o_ckv + rank
    o_ki = o_qi + IDX_HEADS * IDX_DIM
    o_wi = o_ki + IDX_DIM
    for hh in range(nh):
        qa_ref[0, hh] = proj[:, o_q + hh * hd:o_q + (hh + 1) * hd].astype(BF16)
        ka_ref[0, hh] = proj[:, o_k + hh * hd:o_k + (hh + 1) * hd].astype(BF16)
        va_ref[0, hh] = proj[:, o_v + hh * hd:o_v + (hh + 1) * hd].astype(BF16)
        qb_h = proj[:, o_qb + hh * hd:o_qb + (hh + 1) * hd].astype(BF16)
        q_lat = jnp.dot(qb_h, wuk_ref[hh], preferred_element_type=F32)
        ql_ref[0, hh] = (q_lat * (hd ** -0.5)).astype(BF16)
    for hh in range(IDX_HEADS):
        qi_ref[0, hh] = proj[:, o_qi + hh * IDX_DIM:o_qi + (hh + 1) * IDX_DIM].astype(BF16)
    ckv = proj[:, o_ckv:o_ckv + rank]
    ckv_ref[0] = (_rms(ckv) * kvg_ref[...]).astype(BF16)
    ki_ref[0] = proj[:, o_ki:o_ki + IDX_DIM].astype(BF16)
    wi_ref[0] = proj[:, o_wi:o_wi + IDX_HEADS] * idx_scale


def _in_call(x, mod, g, w_in_p, w_uk, kvg, *, nh, tm):
    bsz, seq, d = x.shape
    n_pad = w_in_p.shape[1]
    rank = w_uk.shape[-1]
    hd = HEAD_DIM
    idx_scale = float((IDX_HEADS * IDX_DIM) ** -0.5)
    head_spec = lambda w: pl.BlockSpec((1, nh, tm, w), lambda b, i: (b, 0, i, 0))
    row_spec = lambda w: pl.BlockSpec((1, tm, w), lambda b, i: (b, i, 0))
    hm = lambda w: jax.ShapeDtypeStruct((bsz, nh, seq, w), BF16)
    return pl.pallas_call(
        functools.partial(_in_kernel, nh=nh, idx_scale=idx_scale),
        grid=(bsz, seq // tm),
        in_specs=[pl.BlockSpec((1, tm, d), lambda b, i: (b, i, 0)),
                  pl.BlockSpec((1, 6, d), lambda b, i: (b, 0, 0)),
                  _const_spec((1, d)),
                  _const_spec((d, n_pad)),
                  _const_spec((nh, hd, rank)),
                  _const_spec((1, rank))],
        out_specs=[head_spec(hd), head_spec(hd), head_spec(hd), head_spec(rank),
                   row_spec(rank), head_spec(IDX_DIM), row_spec(IDX_DIM), row_spec(IDX_HEADS)],
        out_shape=[hm(hd), hm(hd), hm(hd), hm(rank),
                   jax.ShapeDtypeStruct((bsz, seq, rank), BF16),
                   jax.ShapeDtypeStruct((bsz, IDX_HEADS, seq, IDX_DIM), BF16),
                   jax.ShapeDtypeStruct((bsz, seq, IDX_DIM), BF16),
                   jax.ShapeDtypeStruct((bsz, seq, IDX_HEADS), F32)],
        compiler_params=_cparams(("parallel", "parallel")),
        name="in_proj",
    )(x, mod, g, w_in_p, w_uk, kvg)


def _dilated_tables():
    span = max(w for w, _ in DILATED_CONFIGS)
    n_off = span // BLOCK + 1
    r = np.arange(BLOCK)[:, None]
    c = np.arange(BLOCK)[None, :]
    cnt = np.zeros((n_off, BLOCK, BLOCK), np.float32)
    dist = np.zeros((n_off, BLOCK, BLOCK), np.float32)
    for o in range(n_off):
        d = o * BLOCK + r - c
        dist[o] = d
        for window, dil in DILATED_CONFIGS:
            cnt[o] += ((d >= 0) & (d <= window) & (d % dil == 0)).astype(np.float32)
    return jnp.asarray(cnt), jnp.asarray(dist)


def _dil_kernel(slope_ref, q_ref, k_ref, v_ref, cnt_ref, dist_ref, o_ref, s_buf, *, heads_per_step):
    i = pl.program_id(2)
    n_off = cnt_ref.shape[0]
    lo = jnp.maximum(i - (n_off - 1), 0)
    outs = []
    for hh in range(heads_per_step):
        slope = slope_ref[pl.program_id(1) * heads_per_step + hh]
        q = q_ref[0, hh]

        def score_body(j, m, hh=hh, slope=slope, q=q):
            off = i - j
            kc = k_ref[0, hh, pl.ds(pl.multiple_of(j * BLOCK, BLOCK), BLOCK), :]
            s = lax.dot_general(q, kc, _NT, preferred_element_type=F32) * (HEAD_DIM ** -0.5)
            s = s - slope * dist_ref[off]
            s = jnp.where(cnt_ref[off] > 0.0, s, NEG)
            s_buf[off] = s
            return jnp.maximum(m, jnp.max(s, axis=-1, keepdims=True))

        m = lax.fori_loop(lo, i + 1, score_body, jnp.full((BLOCK, 1), NEG, F32))

        def pv_body(j, carry, hh=hh, m=m):
            l, acc = carry
            off = i - j
            p = jnp.exp(s_buf[off] - m) * cnt_ref[off]
            vc = v_ref[0, hh, pl.ds(pl.multiple_of(j * BLOCK, BLOCK), BLOCK), :]
            acc = acc + jnp.dot(p.astype(BF16), vc, preferred_element_type=F32)
            return l + jnp.sum(p, axis=-1, keepdims=True), acc

        l, acc = lax.fori_loop(lo, i + 1, pv_body,
                               (jnp.zeros((BLOCK, 1), F32), jnp.zeros((BLOCK, HEAD_DIM), F32)))
        outs.append(acc / l)
    o_ref[0] = jnp.concatenate(outs, axis=-1).astype(BF16)


def _dil_call(slopes, qa, ka, va):
    bsz, nh, seq, hd = qa.shape
    hps = LANES // hd
    cnt, dist = _dilated_tables()
    n_off = cnt.shape[0]
    return pl.pallas_call(
        functools.partial(_dil_kernel, heads_per_step=hps),
        grid=(bsz, nh // hps, seq // BLOCK),
        in_specs=[pl.BlockSpec(memory_space=pltpu.SMEM),
                  pl.BlockSpec((1, hps, BLOCK, hd), lambda b, h, i: (b, h, i, 0)),
                  pl.BlockSpec((1, hps, seq, hd), lambda b, h, i: (b, h, 0, 0)),
                  pl.BlockSpec((1, hps, seq, hd), lambda b, h, i: (b, h, 0, 0)),
                  _const_spec((n_off, BLOCK, BLOCK)),
                  _const_spec((n_off, BLOCK, BLOCK))],
        out_specs=pl.BlockSpec((1, BLOCK, LANES), lambda b, h, i: (b, i, h)),
        out_shape=jax.ShapeDtypeStruct((bsz, seq, nh * hd), BF16),
        scratch_shapes=[pltpu.VMEM((n_off, BLOCK, BLOCK), F32)],
        compiler_params=_cparams(("parallel", "parallel", "arbitrary")),
        name="dilated_attn",
    )(slopes, qa, ka, va, cnt, dist)


def _key_to_f32(key):
    bits = key ^ (lax.shift_right_arithmetic(key, 31) & jnp.int32(0x7FFFFFFF))
    return lax.bitcast_convert_type(bits, F32)


def _dsa_kernel(slope_ref, qi_ref, wi_ref, ki_ref, ql_ref, ckv_ref, wuv_ref, tri_ref, o_ref,
                sc_ref, m_ref, l_ref, acc_ref, *, topk, chunk):
    i = pl.program_id(1)
    t0 = i * BLOCK
    nh = ql_ref.shape[1]
    rank = ql_ref.shape[-1]
    n_ch = (t0 + BLOCK + chunk - 1) // chunk
    row = lax.broadcasted_iota(jnp.int32, (BLOCK, chunk), 0)
    col = lax.broadcasted_iota(jnp.int32, (BLOCK, chunk), 1)
    rel = row - col
    k_start = lambda j: pl.multiple_of(j * chunk, chunk)

    qi = qi_ref[0].reshape(IDX_HEADS * BLOCK, IDX_DIM)
    w = wi_ref[0]
    w_cols = [w[:, hh:hh + 1] for hh in range(IDX_HEADS)]

    def score_body(j, carry):
        k0 = k_start(j)
        kc = ki_ref[0, pl.ds(k0, chunk), :]
        logits = lax.dot_general(qi, kc, _NT, preferred_element_type=F32)
        score = jnp.zeros((BLOCK, chunk), F32)
        for hh in range(IDX_HEADS):
            score = score + w_cols[hh] * jnp.maximum(logits[hh * BLOCK:(hh + 1) * BLOCK], 0.0)
        score = jnp.where(score == 0.0, 0.0, score)
        sc_ref[:, pl.ds(k0, chunk)] = jnp.where(rel + (t0 - k0) >= 0, score, -jnp.inf)
        return carry

    lax.fori_loop(0, n_ch, score_body, 0)

    def count(pred):
        def body(j, cnt):
            return cnt + jnp.where(pred(sc_ref[:, pl.ds(k_start(j), chunk)]), 1.0, 0.0)
        cnt = lax.fori_loop(0, n_ch, body, jnp.zeros((BLOCK, chunk), F32))
        return jnp.sum(cnt, axis=-1, keepdims=True)

    kf = float(topk)
    int_min = jnp.int32(-2 ** 31)
    key = jnp.where(count(lambda s: s >= 0.0) >= kf, jnp.int32(0), int_min)

    def bit_body(b, key):
        trial = key | lax.shift_left(jnp.int32(1), 30 - b)
        cand = _key_to_f32(trial)
        return jnp.where(count(lambda s: s >= cand) >= kf, trial, key)

    key = lax.fori_loop(0, 31, bit_body, key)
    tq = t0 + lax.broadcasted_iota(jnp.int32, (BLOCK, 1), 0)
    few = tq < topk - 1
    tau = jnp.where(few, -jnp.inf, _key_to_f32(key))
    n_gt = count(lambda s: s > tau)
    n_tie = jnp.where(few, 0.0, kf - n_gt)

    def select_body(j, seen):
        k0 = k_start(j)
        s = sc_ref[:, pl.ds(k0, chunk)]
        eq = jnp.where(s == tau, 1.0, 0.0)
        rank_eq = seen + jnp.dot(eq.astype(BF16), tri_ref[...], preferred_element_type=F32)
        take_tie = jnp.where(rank_eq <= n_tie, eq, 0.0)
        sel = jnp.where(s > tau, 1.0, take_tie)
        sc_ref[:, pl.ds(k0, chunk)] = jnp.where(sel > 0.0, 0.0, NEG)
        return seen + jnp.sum(eq, axis=-1, keepdims=True)

    lax.fori_loop(0, n_ch, select_body, jnp.zeros((BLOCK, 1), F32))

    ql = ql_ref[0].reshape(nh * BLOCK, rank)
    m_ref[...] = jnp.full(m_ref.shape, NEG, F32)
    l_ref[...] = jnp.zeros(l_ref.shape, F32)
    acc_ref[...] = jnp.zeros(acc_ref.shape, F32)

    def att_body(j, carry):
        k0 = k_start(j)
        kc = ckv_ref[0, pl.ds(k0, chunk), :]
        s = lax.dot_general(ql, kc, _NT, preferred_element_type=F32)
        dist = (rel + (t0 - k0)).astype(F32)
        mask = sc_ref[:, pl.ds(k0, chunk)]
        s = jnp.concatenate(
            [s[hh * BLOCK:(hh + 1) * BLOCK] - slope_ref[hh] * dist + mask for hh in range(nh)], axis=0)
        m_old = m_ref[...]
        m_new = jnp.maximum(m_old, jnp.max(s, axis=-1, keepdims=True))
        alpha = jnp.exp(m_old - m_new)
        p = jnp.exp(s - m_new)
        l_ref[...] = alpha * l_ref[...] + jnp.sum(p, axis=-1, keepdims=True)
        acc_ref[...] = alpha * acc_ref[...] + jnp.dot(p.astype(BF16), kc, preferred_element_type=F32)
        m_ref[...] = m_new
        return carry

    lax.fori_loop(0, n_ch, att_body, 0)

    o_lat = acc_ref[...] / l_ref[...]
    outs = [jnp.dot(o_lat[hh * BLOCK:(hh + 1) * BLOCK].astype(BF16), wuv_ref[hh], preferred_element_type=F32)
            for hh in range(nh)]
    o_ref[0] = jnp.concatenate(outs, axis=-1).astype(BF16)


def _dsa_call(slopes, qi, wi, ki, ql, ckv, w_uv, *, chunk=256):
    bsz, nh, seq, rank = ql.shape
    hd = w_uv.shape[-1]
    topk = min(TOPK_MAX, seq // 4)
    tri = jnp.asarray(np.triu(np.ones((chunk, chunk), np.float32)), BF16)
    seq_pad = -(-seq // chunk) * chunk
    return pl.pallas_call(
        functools.partial(_dsa_kernel, topk=topk, chunk=chunk),
        grid=(bsz, seq // BLOCK),
        in_specs=[pl.BlockSpec(memory_space=pltpu.SMEM),
                  pl.BlockSpec((1, IDX_HEADS, BLOCK, IDX_DIM), lambda b, i: (b, 0, i, 0)),
                  pl.BlockSpec((1, BLOCK, IDX_HEADS), lambda b, i: (b, i, 0)),
                  pl.BlockSpec((1, seq, IDX_DIM), lambda b, i: (b, 0, 0)),
                  pl.BlockSpec((1, nh, BLOCK, rank), lambda b, i: (b, 0, i, 0)),
                  pl.BlockSpec((1, seq, rank), lambda b, i: (b, 0, 0)),
                  _const_spec((nh, rank, hd)),
                  _const_spec((chunk, chunk))],
        out_specs=pl.BlockSpec((1, BLOCK, nh * hd), lambda b, i: (b, i, 0)),
        out_shape=jax.ShapeDtypeStruct((bsz, seq, nh * hd), BF16),
        scratch_shapes=[pltpu.VMEM((BLOCK, seq_pad), F32),
                        pltpu.VMEM((nh * BLOCK, 1), F32),
                        pltpu.VMEM((nh * BLOCK, 1), F32),
                        pltpu.VMEM((nh * BLOCK, rank), F32)],
        compiler_params=_cparams(("parallel", "arbitrary")),
        name="sparse_attn",
    )(slopes, qi, wi, ki, ql, ckv, w_uv, tri)


def _out_kernel(x_ref, oa_ref, ob_ref, mod_ref, wo_ref, g2_ref, wgu_ref, wd_ref, gf_ref, o_ref,
                *, d_ff, ff_chunks, final_norm):
    mixed = jnp.concatenate([oa_ref[0], ob_ref[0]], axis=-1)
    x1 = x_ref[0] + mod_ref[0, 2:3, :] * jnp.dot(mixed, wo_ref[...], preferred_element_type=F32)
    h2 = _rms(x1) * g2_ref[...]
    h2 = (h2 * (1.0 + mod_ref[0, 4:5, :]) + mod_ref[0, 3:4, :]).astype(BF16)
    fc = d_ff // ff_chunks
    ffn = jnp.zeros(x1.shape, F32)
    for cc in range(ff_chunks):
        gate = jnp.dot(h2, wgu_ref[:, cc * fc:(cc + 1) * fc], preferred_element_type=F32)
        up = jnp.dot(h2, wgu_ref[:, d_ff + cc * fc:d_ff + (cc + 1) * fc], preferred_element_type=F32)
        act = gate * (1.0 / (1.0 + jnp.exp(-gate))) * up
        ffn = ffn + jnp.dot(act.astype(BF16), wd_ref[cc * fc:(cc + 1) * fc, :], preferred_element_type=F32)
    x2 = x1 + mod_ref[0, 5:6, :] * ffn
    if final_norm:
        x2 = _rms(x2) * gf_ref[...]
    o_ref[0] = x2


def _out_call(x, oa, ob, mod, w_out, g_ffn, w_gu, w_down, g_final, *, tm, final_norm):
    bsz, seq, d = x.shape
    d_ff = w_down.shape[0]
    ff_chunks = 2 if d_ff % (2 * LANES) == 0 else 1
    row = lambda w: pl.BlockSpec((1, tm, w), lambda b, i: (b, i, 0))
    return pl.pallas_call(
        functools.partial(_out_kernel, d_ff=d_ff, ff_chunks=ff_chunks, final_norm=final_norm),
        grid=(bsz, seq // tm),
        in_specs=[row(d), row(oa.shape[-1]), row(ob.shape[-1]),
                  pl.BlockSpec((1, 6, d), lambda b, i: (b, 0, 0)),
                  _const_spec((d, d)), _const_spec((1, d)),
                  _const_spec((d, 2 * d_ff)), _const_spec((d_ff, d)), _const_spec((1, d))],
        out_specs=row(d),
        out_shape=jax.ShapeDtypeStruct((bsz, seq, d), F32),
        compiler_params=_cparams(("parallel", "parallel")),
        name="out_proj_ffn",
    )(x, oa, ob, mod, w_out, g_ffn, w_gu, w_down, g_final)


def _alibi_slopes(n_heads):
    s = 2.0 ** (-8.0 * (np.arange(n_heads, dtype=np.float32) + 1.0) / n_heads)
    return jnp.asarray(s[0::2], F32), jnp.asarray(s[1::2], F32)


def kernel(x, c, w_ada, b_ada, g_attn, w_in, kv_norm_g, w_uk, w_uv, w_out, g_ffn, w_gu, w_down, g_final):
    bsz, seq, d = x.shape
    depth = w_ada.shape[0]
    nh = w_uk.shape[1]
    slopes_a, slopes_b = _alibi_slopes(2 * nh)
    d_in = w_in.shape[-1]
    n_pad = -(-d_in // LANES) * LANES
    tm = 512 if seq % 512 == 0 else BLOCK
    for l in range(depth):
        mod = _mod_call(c, w_ada[l], b_ada[l]).reshape(bsz, 6, d)
        w_in_p = jnp.pad(w_in[l], ((0, 0), (0, n_pad - d_in))).astype(BF16)
        qa, ka, va, ql, ckv, qi, ki, wi = _in_call(
            x, mod, g_attn[l].reshape(1, d), w_in_p, w_uk[l].astype(BF16),
            kv_norm_g[l].reshape(1, -1), nh=nh, tm=tm)
        out_a = _dil_call(slopes_a, qa, ka, va)
        out_b = _dsa_call(slopes_b, qi, wi, ki, ql, ckv, w_uv[l].astype(BF16))
        x = _out_call(x, out_a, out_b, mod, w_out[l].astype(BF16), g_ffn[l].reshape(1, d),
                      w_gu[l].astype(BF16), w_down[l].astype(BF16), g_final.reshape(1, d),
                      tm=tm, final_norm=(l == depth - 1))
    return x
```

```python
import functools

import numpy as np
import jax
import jax.numpy as jnp
from jax import lax
from jax.experimental import pallas as pl
from jax.experimental.pallas import tpu as pltpu

F32 = jnp.float32
BF16 = jnp.bfloat16

HEAD_DIM = 64
BLOCK = 128
DILATED_CONFIGS = ((128, 1), (512, 4), (2048, 16))
IDX_HEADS = 8
IDX_DIM = 64
TOPK_MAX = 256
EPS = 1e-6
NEG = -1e30
LANES = 128
VMEM_LIMIT = 56 * 1024 * 1024

_NT = (((1,), (1,)), ((), ()))


def _cparams(sem):
    return pltpu.CompilerParams(dimension_semantics=sem, vmem_limit_bytes=VMEM_LIMIT)


def _const_spec(shape):
    nd = len(shape)
    return pl.BlockSpec(shape, lambda *_: (0,) * nd, pipeline_mode=pl.Buffered(1))


def _mod_kernel(c_ref, w_ref, b_ref, o_ref):
    c = c_ref[...]
    c_act = c * (1.0 / (1.0 + jnp.exp(-c)))
    o_ref[...] = jnp.dot(c_act, w_ref[...], preferred_element_type=F32,
                         precision=lax.Precision.HIGHEST) + b_ref[...]


def _mod_call(c, w_ada, b_ada):
    bsz, d = c.shape
    n = w_ada.shape[1]
    tn = 1024
    return pl.pallas_call(
        _mod_kernel,
        grid=(n // tn,),
        in_specs=[pl.BlockSpec((bsz, d), lambda j: (0, 0)),
                  pl.BlockSpec((d, tn), lambda j: (0, j)),
                  pl.BlockSpec((1, tn), lambda j: (0, j))],
        out_specs=pl.BlockSpec((bsz, tn), lambda j: (0, j)),
        out_shape=jax.ShapeDtypeStruct((bsz, n), F32),
        compiler_params=_cparams(("arbitrary",)),
        name="adaln_mod",
    )(c, w_ada, b_ada.reshape(1, n))


def _rms(x):
    return x * lax.rsqrt(jnp.mean(x * x, axis=-1, keepdims=True) + EPS)


def _in_kernel(x_ref, mod_ref, g_ref, w_ref, wuk_ref, kvg_ref,
               qa_ref, ka_ref, va_ref, qlt_ref, ckv_ref, ckvt_ref, qit_ref, ki_ref, wit_ref,
               *, nh, idx_scale):
    hd = HEAD_DIM
    h = _rms(x_ref[0]) * g_ref[...]
    h = h * (1.0 + mod_ref[0, 1:2, :]) + mod_ref[0, 0:1, :]
    proj = jnp.dot(h.astype(BF16), w_ref[...], preferred_element_type=F32)
    wa = nh * hd
    rank = ckv_ref.shape[-1]
    o_qb, o_ckv = 3 * wa, 4 * wa
    o_qi = o_ckv + rank
    o_ki = o_qi + IDX_HEADS * IDX_DIM
    qa_ref[0] = proj[:, 0:wa] * (hd ** -0.5)
    ka_ref[0] = proj[:, wa:2 * wa]
    va_ref[0] = proj[:, 2 * wa:3 * wa]
    for hh in range(nh):
        qb_h = proj[:, o_qb + hh * hd:o_qb + (hh + 1) * hd].astype(BF16)
        q_lat = jnp.dot(qb_h, wuk_ref[hh], preferred_element_type=F32) * (hd ** -0.5)
        qlt_ref[0, hh * rank:(hh + 1) * rank, :] = q_lat.T.astype(BF16)
    ckv = (_rms(proj[:, o_ckv:o_ckv + rank]) * kvg_ref[...])
    ckv_ref[0] = ckv.astype(BF16)
    ckvt_ref[0] = ckv.T.astype(BF16)
    for pp in range(IDX_HEADS * IDX_DIM // LANES):
        qit_ref[0, pp * LANES:(pp + 1) * LANES, :] = proj[:, o_qi + pp * LANES:o_qi + (pp + 1) * LANES].T.astype(BF16)
    kiw = proj[:, o_ki:o_ki + LANES]
    ki_ref[0] = kiw[:, :IDX_DIM].astype(BF16)
    wit_ref[0] = kiw.T[IDX_DIM:IDX_DIM + IDX_HEADS, :] * idx_scale


def _in_call(x, mod, g, w_in_p, w_uk, kvg, *, nh, tm):
    bsz, seq, d = x.shape
    n_pad = w_in_p.shape[1]
    rank = w_uk.shape[-1]
    hd = HEAD_DIM
    wa = nh * hd
    idx_scale = float((IDX_HEADS * IDX_DIM) ** -0.5)
    row_spec = lambda w: pl.BlockSpec((1, tm, w), lambda b, i: (b, i, 0))
    col_spec = lambda r: pl.BlockSpec((1, r, tm), lambda b, i: (b, 0, i))
    return pl.pallas_call(
        functools.partial(_in_kernel, nh=nh, idx_scale=idx_scale),
        grid=(bsz, seq // tm),
        in_specs=[pl.BlockSpec((1, tm, d), lambda b, i: (b, i, 0)),
                  pl.BlockSpec((1, 6, d), lambda b, i: (b, 0, 0)),
                  _const_spec((1, d)),
                  _const_spec((d, n_pad)),
                  _const_spec((nh, hd, rank)),
                  _const_spec((1, rank))],
        out_specs=[row_spec(wa), row_spec(wa), row_spec(wa),
                   col_spec(nh * rank), row_spec(rank), col_spec(rank),
                   col_spec(IDX_HEADS * IDX_DIM), row_spec(IDX_DIM), col_spec(IDX_HEADS)],
        out_shape=[jax.ShapeDtypeStruct((bsz, seq, wa), F32),
                   jax.ShapeDtypeStruct((bsz, seq, wa), F32),
                   jax.ShapeDtypeStruct((bsz, seq, wa), F32),
                   jax.ShapeDtypeStruct((bsz, nh * rank, seq), BF16),
                   jax.ShapeDtypeStruct((bsz, seq, rank), BF16),
                   jax.ShapeDtypeStruct((bsz, rank, seq), BF16),
                   jax.ShapeDtypeStruct((bsz, IDX_HEADS * IDX_DIM, seq), BF16),
                   jax.ShapeDtypeStruct((bsz, seq, IDX_DIM), BF16),
                   jax.ShapeDtypeStruct((bsz, IDX_HEADS, seq), F32)],
        compiler_params=_cparams(("parallel", "parallel")),
        name="in_proj",
    )(x, mod, g, w_in_p, w_uk, kvg)


def _band_tables(tq, w_res):
    r = np.arange(tq)[:, None]
    d0 = r - np.arange(tq)[None, :]
    d1 = r - np.arange(tq + w_res)[None, :] + w_res
    clean = lambda d: np.where((d >= 0) & (d <= w_res), d, -1).astype(np.float32)
    return jnp.asarray(clean(d0)), jnp.asarray(clean(d1))


def _dil_kernel(slope_ref, q_ref, k_ref, v_ref, d0_ref, d1_ref, o_ref,
                m_ref, l_ref, acc_ref, b0_ref, b1_ref, *, tq, w_res, configs):
    seq = q_ref.shape[1]
    hp = pl.program_id(1)
    lane = lax.broadcasted_iota(jnp.int32, (1, LANES), 1)
    first = lane < HEAD_DIM
    m_ref[...] = jnp.full(m_ref.shape, NEG, F32)
    l_ref[...] = jnp.zeros(l_ref.shape, F32)
    acc_ref[...] = jnp.zeros(acc_ref.shape, F32)

    def rows(start, size, dil):
        return pl.ds(start, size) if dil == 1 else pl.ds(start, size, stride=dil)

    def tile(q_rows, k_rows, bias_ref):
        q2 = q_ref[0, q_rows, :]
        k2 = k_ref[0, k_rows, :].astype(BF16)
        v2 = v_ref[0, k_rows, :].astype(BF16)
        parts = []
        for hh in range(2):
            keep = first if hh == 0 else jnp.logical_not(first)
            qh = jnp.where(keep, q2, 0.0).astype(BF16)
            s = lax.dot_general(qh, k2, _NT, preferred_element_type=F32) + bias_ref[hh]
            m_loc = jnp.max(s, axis=-1, keepdims=True)
            p = jnp.exp(s - m_loc)
            l_loc = jnp.sum(p, axis=-1, keepdims=True)
            o_loc = jnp.dot(p.astype(BF16), v2, preferred_element_type=F32)
            parts.append((m_loc, l_loc, o_loc))
        pick = lambda a, b: jnp.where(first, a, b)
        m_t = pick(parts[0][0], parts[1][0])
        l_t = pick(parts[0][1], parts[1][1])
        o_t = pick(parts[0][2], parts[1][2])
        m_old = m_ref[q_rows, :]
        m_new = jnp.maximum(m_old, m_t)
        a_old = jnp.exp(m_old - m_new)
        a_new = jnp.exp(m_t - m_new)
        acc_ref[q_rows, :] = a_old * acc_ref[q_rows, :] + a_new * o_t
        l_ref[q_rows, :] = a_old * l_ref[q_rows, :] + a_new * l_t
        m_ref[q_rows, :] = m_new

    for _, dil in configs:
        n = seq // dil
        for hh in range(2):
            neg_slope = -slope_ref[hp * 2 + hh] * float(dil)
            b0_ref[hh] = jnp.where(d0_ref[...] >= 0.0, neg_slope * d0_ref[...], NEG)
            b1_ref[hh] = jnp.where(d1_ref[...] >= 0.0, neg_slope * d1_ref[...], NEG)

        def residue_body(r, carry, dil=dil, n=n):
            tile(rows(r, tq, dil), rows(r, tq, dil), b0_ref)

            def tile_body(jt, c):
                q0 = jt * (tq * dil) + r
                tile(rows(q0, tq, dil), rows(q0 - w_res * dil, tq + w_res, dil), b1_ref)
                return c

            if n // tq > 1:
                lax.fori_loop(1, n // tq, tile_body, 0)
            return carry

        lax.fori_loop(0, dil, residue_body, 0)

    o_ref[0] = (acc_ref[...] / l_ref[...]).astype(BF16)


def _dil_call(slopes, qa, ka, va):
    bsz, seq, wa = qa.shape
    w_res = DILATED_CONFIGS[0][0] // DILATED_CONFIGS[0][1]
    assert all(w // d == w_res for w, d in DILATED_CONFIGS)
    max_dil = max(d for _, d in DILATED_CONFIGS)
    tq = min(256, seq // max_dil)
    assert tq >= w_res and all((seq // d) % tq == 0 for _, d in DILATED_CONFIGS)
    d0, d1 = _band_tables(tq, w_res)
    blk = pl.BlockSpec((1, seq, LANES), lambda b, h: (b, 0, h))
    return pl.pallas_call(
        functools.partial(_dil_kernel, tq=tq, w_res=w_res, configs=DILATED_CONFIGS),
        grid=(bsz, wa // LANES),
        in_specs=[pl.BlockSpec(memory_space=pltpu.SMEM), blk, blk, blk,
                  _const_spec(d0.shape), _const_spec(d1.shape)],
        out_specs=blk,
        out_shape=jax.ShapeDtypeStruct((bsz, seq, wa), BF16),
        scratch_shapes=[pltpu.VMEM((seq, LANES), F32), pltpu.VMEM((seq, LANES), F32),
                        pltpu.VMEM((seq, LANES), F32),
                        pltpu.VMEM((2,) + d0.shape, F32), pltpu.VMEM((2,) + d1.shape, F32)],
        compiler_params=_cparams(("parallel", "parallel")),
        name="dilated_attn",
    )(slopes, qa, ka, va, d0, d1)


def _key_to_f32(key):
    bits = key ^ (lax.shift_right_arithmetic(key, 31) & jnp.int32(0x7FFFFFFF))
    return lax.bitcast_convert_type(bits, F32)


def _dsa_kernel(slope_ref, qit_ref, wit_ref, ki_ref, qlt_ref, ckv_ref, ckvt_ref, wuv_ref, tri_ref, o_ref,
                sc_ref, acc_ref, *, topk, chunk):
    i = pl.program_id(1)
    t0 = i * BLOCK
    rank = ckv_ref.shape[-1]
    nh = qlt_ref.shape[1] // rank
    n_ch = (t0 + BLOCK + chunk - 1) // chunk
    kk = lax.broadcasted_iota(jnp.int32, (chunk, BLOCK), 0)
    qq = lax.broadcasted_iota(jnp.int32, (chunk, BLOCK), 1)
    rel = qq - kk
    k_start = lambda j: pl.multiple_of(j * chunk, chunk)
    heads = lambda ref, width: jnp.concatenate(
        [ref[0, hh * width:(hh + 1) * width, :] for hh in range(ref.shape[1] // width)], axis=-1)

    qit = heads(qit_ref, IDX_DIM)
    wt = wit_ref[0]

    def score_body(j, carry):
        k0 = k_start(j)
        logits = jnp.dot(ki_ref[0, pl.ds(k0, chunk), :], qit, preferred_element_type=F32)
        score = jnp.zeros((chunk, BLOCK), F32)
        for hh in range(IDX_HEADS):
            score = score + wt[hh:hh + 1, :] * jnp.maximum(logits[:, hh * BLOCK:(hh + 1) * BLOCK], 0.0)
        score = jnp.where(score == 0.0, 0.0, score)
        sc_ref[pl.ds(k0, chunk), :] = jnp.where(rel + (t0 - k0) >= 0, score, -jnp.inf)
        return carry

    lax.fori_loop(0, n_ch, score_body, 0)

    def count(pred):
        def body(j, cnt):
            return cnt + jnp.where(pred(sc_ref[pl.ds(k_start(j), chunk), :]), 1.0, 0.0)
        cnt = lax.fori_loop(0, n_ch, body, jnp.zeros((chunk, BLOCK), F32))
        return jnp.sum(cnt, axis=0, keepdims=True)

    kf = float(topk)
    int_min = jnp.int32(-2 ** 31)
    key = jnp.where(count(lambda s: s >= 0.0) >= kf, jnp.int32(0), int_min)

    def bit_body(b, key):
        trial = key | lax.shift_left(jnp.int32(1), 30 - b)
        cand = _key_to_f32(trial)
        return jnp.where(count(lambda s: s >= cand) >= kf, trial, key)

    key = lax.fori_loop(0, 31, bit_body, key)
    tq = t0 + lax.broadcasted_iota(jnp.int32, (1, BLOCK), 1)
    few = tq < topk - 1
    tau = jnp.where(few, -jnp.inf, _key_to_f32(key))
    n_gt = count(lambda s: s > tau)
    n_tie = jnp.where(few, 0.0, kf - n_gt)

    def select_body(j, seen):
        k0 = k_start(j)
        s = sc_ref[pl.ds(k0, chunk), :]
        eq = jnp.where(s == tau, 1.0, 0.0)
        rank_eq = seen + jnp.dot(tri_ref[...], eq.astype(BF16), preferred_element_type=F32)
        take_tie = jnp.where(rank_eq <= n_tie, eq, 0.0)
        sel = jnp.where(s > tau, 1.0, take_tie)
        sc_ref[pl.ds(k0, chunk), :] = jnp.where(sel > 0.0, 0.0, NEG)
        return seen + jnp.sum(eq, axis=0, keepdims=True)

    lax.fori_loop(0, n_ch, select_body, jnp.zeros((1, BLOCK), F32))

    qlt = heads(qlt_ref, rank)
    acc_ref[...] = jnp.zeros(acc_ref.shape, F32)

    def att_body(j, carry):
        m_old, l_old = carry
        k0 = k_start(j)
        s = jnp.dot(ckv_ref[0, pl.ds(k0, chunk), :], qlt, preferred_element_type=F32)
        dist = (rel + (t0 - k0)).astype(F32)
        mask = sc_ref[pl.ds(k0, chunk), :]
        s = jnp.concatenate(
            [s[:, hh * BLOCK:(hh + 1) * BLOCK] + (mask - slope_ref[hh] * dist) for hh in range(nh)], axis=-1)
        m_new = jnp.maximum(m_old, jnp.max(s, axis=0, keepdims=True))
        alpha = jnp.exp(m_old - m_new)
        p = jnp.exp(s - m_new)
        l_new = alpha * l_old + jnp.sum(p, axis=0, keepdims=True)
        pv = jnp.dot(ckvt_ref[0, :, pl.ds(k0, chunk)], p.astype(BF16), preferred_element_type=F32)
        acc_ref[...] = alpha * acc_ref[...] + pv
        return m_new, l_new

    _, l_fin = lax.fori_loop(0, n_ch, att_body,
                             (jnp.full((1, nh * BLOCK), NEG, F32), jnp.zeros((1, nh * BLOCK), F32)))

    o_lat_t = acc_ref[...] / l_fin
    outs = [jnp.dot(o_lat_t[:, hh * BLOCK:(hh + 1) * BLOCK].T.astype(BF16), wuv_ref[hh],
                    preferred_element_type=F32) for hh in range(nh)]
    o_ref[0] = jnp.concatenate(outs, axis=-1).astype(BF16)


def _dsa_call(slopes, qit, wit, ki, qlt, ckv, ckvt, w_uv, *, chunk=256):
    bsz, seq, rank = ckv.shape
    nh, _, hd = w_uv.shape
    topk = min(TOPK_MAX, seq // 4)
    assert seq % chunk == 0
    tri = jnp.asarray(np.tril(np.ones((chunk, chunk), np.float32)), BF16)
    return pl.pallas_call(
        functools.partial(_dsa_kernel, topk=topk, chunk=chunk),
        grid=(bsz, seq // BLOCK),
        in_specs=[pl.BlockSpec(memory_space=pltpu.SMEM),
                  pl.BlockSpec((1, IDX_HEADS * IDX_DIM, BLOCK), lambda b, i: (b, 0, i)),
                  pl.BlockSpec((1, IDX_HEADS, BLOCK), lambda b, i: (b, 0, i)),
                  pl.BlockSpec((1, seq, IDX_DIM), lambda b, i: (b, 0, 0)),
                  pl.BlockSpec((1, nh * rank, BLOCK), lambda b, i: (b, 0, i)),
                  pl.BlockSpec((1, seq, rank), lambda b, i: (b, 0, 0)),
                  pl.BlockSpec((1, rank, seq), lambda b, i: (b, 0, 0)),
                  _const_spec((nh, rank, hd)),
                  _const_spec((chunk, chunk))],
        out_specs=pl.BlockSpec((1, BLOCK, nh * hd), lambda b, i: (b, i, 0)),
        out_shape=jax.ShapeDtypeStruct((bsz, seq, nh * hd), BF16),
        scratch_shapes=[pltpu.VMEM((seq, BLOCK), F32),
                        pltpu.VMEM((rank, nh * BLOCK), F32)],
        compiler_params=_cparams(("parallel", "arbitrary")),
        name="sparse_attn",
    )(slopes, qit, wit, ki, qlt, ckv, ckvt, w_uv, tri)


def _out_kernel(x_ref, oa_ref, ob_ref, mod_ref, wo_ref, g2_ref, wgu_ref, wd_ref, gf_ref, o_ref,
                *, d_ff, ff_chunks, final_norm):
    mixed = jnp.concatenate([oa_ref[0], ob_ref[0]], axis=-1)
    x1 = x_ref[0] + mod_ref[0, 2:3, :] * jnp.dot(mixed, wo_ref[...], preferred_element_type=F32)
    h2 = _rms(x1) * g2_ref[...]
    h2 = (h2 * (1.0 + mod_ref[0, 4:5, :]) + mod_ref[0, 3:4, :]).astype(BF16)
    fc = d_ff // ff_chunks
    ffn = jnp.zeros(x1.shape, F32)
    for cc in range(ff_chunks):
        gate = jnp.dot(h2, wgu_ref[:, cc * fc:(cc + 1) * fc], preferred_element_type=F32)
        up = jnp.dot(h2, wgu_ref[:, d_ff + cc * fc:d_ff + (cc + 1) * fc], preferred_element_type=F32)
        act = gate * (1.0 / (1.0 + jnp.exp(-gate))) * up
        ffn = ffn + jnp.dot(act.astype(BF16), wd_ref[cc * fc:(cc + 1) * fc, :], preferred_element_type=F32)
    x2 = x1 + mod_ref[0, 5:6, :] * ffn
    if final_norm:
        x2 = _rms(x2) * gf_ref[...]
    o_ref[0] = x2


def _out_call(x, oa, ob, mod, w_out, g_ffn, w_gu, w_down, g_final, *, tm, final_norm):
    bsz, seq, d = x.shape
    d_ff = w_down.shape[0]
    ff_chunks = 2 if d_ff % (2 * LANES) == 0 else 1
    row = lambda w: pl.BlockSpec((1, tm, w), lambda b, i: (b, i, 0))
    return pl.pallas_call(
        functools.partial(_out_kernel, d_ff=d_ff, ff_chunks=ff_chunks, final_norm=final_norm),
        grid=(bsz, seq // tm),
        in_specs=[row(d), row(oa.shape[-1]), row(ob.shape[-1]),
                  pl.BlockSpec((1, 6, d), lambda b, i: (b, 0, 0)),
                  _const_spec((d, d)), _const_spec((1, d)),
                  _const_spec((d, 2 * d_ff)), _const_spec((d_ff, d)), _const_spec((1, d))],
        out_specs=row(d),
        out_shape=jax.ShapeDtypeStruct((bsz, seq, d), F32),
        compiler_params=_cparams(("parallel", "parallel")),
        name="out_proj_ffn",
    )(x, oa, ob, mod, w_out, g_ffn, w_gu, w_down, g_final)


def _alibi_slopes(n_heads):
    s = 2.0 ** (-8.0 * (np.arange(n_heads, dtype=np.float32) + 1.0) / n_heads)
    return jnp.asarray(s[0::2], F32), jnp.asarray(s[1::2], F32)


def kernel(x, c, w_ada, b_ada, g_attn, w_in, kv_norm_g, w_uk, w_uv, w_out, g_ffn, w_gu, w_down, g_final):
    bsz, seq, d = x.shape
    depth = w_ada.shape[0]
    nh = w_uk.shape[1]
    slopes_a, slopes_b = _alibi_slopes(2 * nh)
    d_in = w_in.shape[-1]
    n_pad = -(-d_in // LANES) * LANES
    tm = 512 if seq % 512 == 0 else BLOCK
    for l in range(depth):
        mod = _mod_call(c, w_ada[l], b_ada[l]).reshape(bsz, 6, d)
        w_in_p = jnp.pad(w_in[l], ((0, 0), (0, n_pad - d_in))).astype(BF16)
        qa, ka, va, qlt, ckv, ckvt, qit, ki, wit = _in_call(
            x, mod, g_attn[l].reshape(1, d), w_in_p, w_uk[l].astype(BF16),
            kv_norm_g[l].reshape(1, -1), nh=nh, tm=tm)
        out_a = _dil_call(slopes_a, qa, ka, va)
        out_b = _dsa_call(slopes_b, qit, wit, ki, qlt, ckv, ckvt, w_uv[l].astype(BF16))
        x = _out_call(x, out_a, out_b, mod, w_out[l].astype(BF16), g_ffn[l].reshape(1, d),
                      w_gu[l].astype(BF16), w_down[l].astype(BF16), g_final.reshape(1, d),
                      tm=tm, final_norm=(l == depth - 1))
    return x
```

```python
import functools

import numpy as np
import jax
import jax.numpy as jnp
from jax import lax
from jax.experimental import pallas as pl
from jax.experimental.pallas import tpu as pltpu

F32 = jnp.float32
BF16 = jnp.bfloat16

HEAD_DIM = 64
BLOCK = 128
DILATED_CONFIGS = ((128, 1), (512, 4), (2048, 16))
IDX_HEADS = 8
IDX_DIM = 64
TOPK_MAX = 256
EPS = 1e-6
NEG = -1e30
F32_MIN = float(np.finfo(np.float32).min)
LOG2E = float(np.log2(np.e))
POS_RADIX = 64
POS_ROWS = 16
LANES = 128
VMEM_LIMIT = 56 * 1024 * 1024

_NT = (((1,), (1,)), ((), ()))


def _cparams(sem, flags=None):
    return pltpu.CompilerParams(dimension_semantics=sem, vmem_limit_bytes=VMEM_LIMIT, flags=flags)


def _const_spec(shape):
    nd = len(shape)
    return pl.BlockSpec(shape, lambda *_: (0,) * nd, pipeline_mode=pl.Buffered(1))


def _mod_kernel(c_ref, w_ref, b_ref, o_ref):
    c = c_ref[...]
    c_act = c * (1.0 / (1.0 + jnp.exp(-c)))
    o_ref[...] = jnp.dot(c_act, w_ref[...], preferred_element_type=F32,
                         precision=lax.Precision.HIGHEST) + b_ref[...]


def _mod_call(c, w_ada, b_ada):
    bsz, d = c.shape
    n = w_ada.shape[1]
    tn = 1024
    return pl.pallas_call(
        _mod_kernel,
        grid=(n // tn,),
        in_specs=[pl.BlockSpec((bsz, d), lambda j: (0, 0)),
                  pl.BlockSpec((d, tn), lambda j: (0, j)),
                  pl.BlockSpec((1, tn), lambda j: (0, j))],
        out_specs=pl.BlockSpec((bsz, tn), lambda j: (0, j)),
        out_shape=jax.ShapeDtypeStruct((bsz, n), F32),
        compiler_params=_cparams(("arbitrary",)),
        name="adaln_mod",
    )(c, w_ada, b_ada.reshape(1, n))


def _rms(x):
    return x * lax.rsqrt(jnp.mean(x * x, axis=-1, keepdims=True) + EPS)


def _in_kernel(x_ref, mod_ref, g_ref, w_ref, wuk_ref, kvg_ref,
               qa_ref, ka_ref, va_ref, qlt_ref, ckv_ref, ckvt_ref, qit_ref, ki_ref, wit_ref,
               *, nh, idx_scale):
    hd = HEAD_DIM
    h = _rms(x_ref[0]) * g_ref[...]
    h = h * (1.0 + mod_ref[0, 1:2, :]) + mod_ref[0, 0:1, :]
    proj = jnp.dot(h.astype(BF16), w_ref[...], preferred_element_type=F32)
    wa = nh * hd
    rank = ckvt_ref.shape[1]
    tm = x_ref.shape[1]
    o_qb, o_ckv = 3 * wa, 4 * wa
    o_qi = o_ckv + rank
    o_ki = o_qi + IDX_HEADS * IDX_DIM
    qa_ref[0] = proj[:, 0:wa] * (hd ** -0.5)
    ka_ref[0] = proj[:, wa:2 * wa]
    va_ref[0] = proj[:, 2 * wa:3 * wa]
    for hh in range(nh):
        qb_h = proj[:, o_qb + hh * hd:o_qb + (hh + 1) * hd].astype(BF16)
        q_lat = jnp.dot(qb_h, wuk_ref[hh], preferred_element_type=F32) * (hd ** -0.5 * LOG2E)
        qlt_ref[0, hh * rank:(hh + 1) * rank, :] = q_lat.T.astype(BF16)
    ckv = (_rms(proj[:, o_ckv:o_ckv + rank]) * kvg_ref[...])
    ckvt_ref[0] = ckv.T.astype(BF16)
    t = pl.program_id(1) * tm + lax.broadcasted_iota(jnp.int32, (tm, LANES), 0)
    lane = lax.broadcasted_iota(jnp.int32, (tm, LANES), 1)
    pos = jnp.where(lane < 2, t // POS_RADIX, jnp.where(lane < 4, t % POS_RADIX, jnp.where(lane < 6, 1, 0)))
    ckv_ref[0] = jnp.concatenate([ckv, pos.astype(F32)], axis=-1).astype(BF16)
    for pp in range(IDX_HEADS * IDX_DIM // LANES):
        qit_ref[0, pp * LANES:(pp + 1) * LANES, :] = proj[:, o_qi + pp * LANES:o_qi + (pp + 1) * LANES].T.astype(BF16)
    kiw = proj[:, o_ki:o_ki + LANES]
    ki_ref[0] = kiw[:, :IDX_DIM].astype(BF16)
    wit_ref[0] = kiw.T[IDX_DIM:IDX_DIM + IDX_HEADS, :] * idx_scale


def _in_call(x, mod, g, w_in_p, w_uk, kvg, *, nh, tm):
    bsz, seq, d = x.shape
    n_pad = w_in_p.shape[1]
    rank = w_uk.shape[-1]
    hd = HEAD_DIM
    wa = nh * hd
    idx_scale = float((IDX_HEADS * IDX_DIM) ** -0.5)
    row_spec = lambda w: pl.BlockSpec((1, tm, w), lambda b, i: (b, i, 0))
    col_spec = lambda r: pl.BlockSpec((1, r, tm), lambda b, i: (b, 0, i))
    return pl.pallas_call(
        functools.partial(_in_kernel, nh=nh, idx_scale=idx_scale),
        grid=(bsz, seq // tm),
        in_specs=[pl.BlockSpec((1, tm, d), lambda b, i: (b, i, 0)),
                  pl.BlockSpec((1, 6, d), lambda b, i: (b, 0, 0)),
                  _const_spec((1, d)),
                  _const_spec((d, n_pad)),
                  _const_spec((nh, hd, rank)),
                  _const_spec((1, rank))],
        out_specs=[row_spec(wa), row_spec(wa), row_spec(wa),
                   col_spec(nh * rank), row_spec(rank + LANES), col_spec(rank),
                   col_spec(IDX_HEADS * IDX_DIM), row_spec(IDX_DIM), col_spec(IDX_HEADS)],
        out_shape=[jax.ShapeDtypeStruct((bsz, seq, wa), F32),
                   jax.ShapeDtypeStruct((bsz, seq, wa), F32),
                   jax.ShapeDtypeStruct((bsz, seq, wa), F32),
                   jax.ShapeDtypeStruct((bsz, nh * rank, seq), BF16),
                   jax.ShapeDtypeStruct((bsz, seq, rank + LANES), BF16),
                   jax.ShapeDtypeStruct((bsz, rank, seq), BF16),
                   jax.ShapeDtypeStruct((bsz, IDX_HEADS * IDX_DIM, seq), BF16),
                   jax.ShapeDtypeStruct((bsz, seq, IDX_DIM), BF16),
                   jax.ShapeDtypeStruct((bsz, IDX_HEADS, seq), F32)],
        compiler_params=_cparams(("parallel", "parallel")),
        name="in_proj",
    )(x, mod, g, w_in_p, w_uk, kvg)


def _band_tables(tq, w_res):
    r = np.arange(tq)[:, None]
    d0 = r - np.arange(tq)[None, :]
    d1 = r - np.arange(tq + w_res)[None, :] + w_res
    clean = lambda d: np.where((d >= 0) & (d <= w_res), d, -1).astype(np.float32)
    return jnp.asarray(clean(d0)), jnp.asarray(clean(d1))


def _dil_kernel(slope_ref, q_ref, k_ref, v_ref, d0_ref, d1_ref, o_ref,
                m_ref, l_ref, acc_ref, b0_ref, b1_ref, *, tq, w_res, configs):
    seq = q_ref.shape[1]
    hp = pl.program_id(1)
    lane = lax.broadcasted_iota(jnp.int32, (1, LANES), 1)
    first = lane < HEAD_DIM
    m_ref[...] = jnp.full(m_ref.shape, NEG, F32)
    l_ref[...] = jnp.zeros(l_ref.shape, F32)
    acc_ref[...] = jnp.zeros(acc_ref.shape, F32)

    def rows(start, size, dil):
        return pl.ds(start, size) if dil == 1 else pl.ds(start, size, stride=dil)

    def tile(q_rows, k_rows, bias_ref):
        q2 = q_ref[0, q_rows, :]
        k2 = k_ref[0, k_rows, :].astype(BF16)
        v2 = v_ref[0, k_rows, :].astype(BF16)
        parts = []
        for hh in range(2):
            keep = first if hh == 0 else jnp.logical_not(first)
            qh = jnp.where(keep, q2, 0.0).astype(BF16)
            s = lax.dot_general(qh, k2, _NT, preferred_element_type=F32) + bias_ref[hh]
            m_loc = jnp.max(s, axis=-1, keepdims=True)
            p = jnp.exp(s - m_loc)
            l_loc = jnp.sum(p, axis=-1, keepdims=True)
            o_loc = jnp.dot(p.astype(BF16), v2, preferred_element_type=F32)
            parts.append((m_loc, l_loc, o_loc))
        pick = lambda a, b: jnp.where(first, a, b)
        m_t = pick(parts[0][0], parts[1][0])
        l_t = pick(parts[0][1], parts[1][1])
        o_t = pick(parts[0][2], parts[1][2])
        m_old = m_ref[q_rows, :]
        m_new = jnp.maximum(m_old, m_t)
        a_old = jnp.exp(m_old - m_new)
        a_new = jnp.exp(m_t - m_new)
        acc_ref[q_rows, :] = a_old * acc_ref[q_rows, :] + a_new * o_t
        l_ref[q_rows, :] = a_old * l_ref[q_rows, :] + a_new * l_t
        m_ref[q_rows, :] = m_new

    for _, dil in configs:
        n = seq // dil
        for hh in range(2):
            neg_slope = -slope_ref[hp * 2 + hh] * float(dil)
            b0_ref[hh] = jnp.where(d0_ref[...] >= 0.0, neg_slope * d0_ref[...], NEG)
            b1_ref[hh] = jnp.where(d1_ref[...] >= 0.0, neg_slope * d1_ref[...], NEG)

        def residue_body(r, carry, dil=dil, n=n):
            tile(rows(r, tq, dil), rows(r, tq, dil), b0_ref)

            def tile_body(jt, c):
                q0 = jt * (tq * dil) + r
                tile(rows(q0, tq, dil), rows(q0 - w_res * dil, tq + w_res, dil), b1_ref)
                return c

            if n // tq > 1:
                lax.fori_loop(1, n // tq, tile_body, 0)
            return carry

        lax.fori_loop(0, dil, residue_body, 0)

    o_ref[0] = (acc_ref[...] / l_ref[...]).astype(BF16)


def _dil_call(slopes, qa, ka, va):
    bsz, seq, wa = qa.shape
    w_res = DILATED_CONFIGS[0][0] // DILATED_CONFIGS[0][1]
    assert all(w // d == w_res for w, d in DILATED_CONFIGS)
    max_dil = max(d for _, d in DILATED_CONFIGS)
    tq = min(256, seq // max_dil)
    assert tq >= w_res and all((seq // d) % tq == 0 for _, d in DILATED_CONFIGS)
    d0, d1 = _band_tables(tq, w_res)
    blk = pl.BlockSpec((1, seq, LANES), lambda b, h: (b, 0, h))
    return pl.pallas_call(
        functools.partial(_dil_kernel, tq=tq, w_res=w_res, configs=DILATED_CONFIGS),
        grid=(bsz, wa // LANES),
        in_specs=[pl.BlockSpec(memory_space=pltpu.SMEM), blk, blk, blk,
                  _const_spec(d0.shape), _const_spec(d1.shape)],
        out_specs=blk,
        out_shape=jax.ShapeDtypeStruct((bsz, seq, wa), BF16),
        scratch_shapes=[pltpu.VMEM((seq, LANES), F32), pltpu.VMEM((seq, LANES), F32),
                        pltpu.VMEM((seq, LANES), F32),
                        pltpu.VMEM((2,) + d0.shape, F32), pltpu.VMEM((2,) + d1.shape, F32)],
        compiler_params=_cparams(("parallel", "parallel")),
        name="dilated_attn",
    )(jnp.asarray(slopes, F32), qa, ka, va, d0, d1)


def _key_to_f32(key):
    bits = key ^ (lax.shift_right_arithmetic(key, 31) & jnp.int32(0x7FFFFFFF))
    return lax.bitcast_convert_type(bits, F32)


def _dsa_kernel(qit_ref, wit_ref, ki_ref, qlt_ref, qpos_ref, ckv_ref, ckvt_ref, wuv_ref, tri_ref, o_ref,
                sc_ref, acc_ref, s_ref, *, topk, chunk):
    i = pl.program_id(1)
    t0 = i * BLOCK
    rank = ckvt_ref.shape[1]
    nh = qlt_ref.shape[1] // rank
    n_ch = (t0 + BLOCK + chunk - 1) // chunk
    kk = lax.broadcasted_iota(jnp.int32, (chunk, BLOCK), 0)
    qq = lax.broadcasted_iota(jnp.int32, (chunk, BLOCK), 1)
    rel = qq - kk
    k_start = lambda j: pl.multiple_of(j * chunk, chunk)
    heads = lambda ref, width: jnp.concatenate(
        [ref[0, hh * width:(hh + 1) * width, :] for hh in range(ref.shape[1] // width)], axis=-1)

    qit = heads(qit_ref, IDX_DIM)
    wt = wit_ref[0]

    idx_pair = 2 * BLOCK
    idx_cols = [slice(pp * idx_pair, (pp + 1) * idx_pair) for pp in range(IDX_HEADS * BLOCK // idx_pair)]

    def idx_logits(j, cols):
        s_ref[:, cols] = jnp.dot(ki_ref[0, pl.ds(k_start(j), chunk), :], qit[:, cols], preferred_element_type=F32)

    def idx_weigh(pp, cols):
        lg = s_ref[:, cols]
        return (wt[2 * pp:2 * pp + 1, :] * jnp.maximum(lg[:, :BLOCK], 0.0)
                + wt[2 * pp + 1:2 * pp + 2, :] * jnp.maximum(lg[:, BLOCK:], 0.0))

    def idx_store(j, score):
        k0 = k_start(j)
        score = jnp.where(score == 0.0, 0.0, score)
        sc_ref[pl.ds(k0, chunk), :] = jnp.where(rel + (t0 - k0) >= 0, score, -jnp.inf)

    for cols in idx_cols:
        idx_logits(0, cols)

    def score_body(j, carry):
        score = jnp.zeros((chunk, BLOCK), F32)
        for pp, cols in enumerate(idx_cols):
            score = score + idx_weigh(pp, cols)
            idx_logits(j, cols)
        idx_store(j - 1, score)
        return carry

    lax.fori_loop(1, n_ch, score_body, 0)
    idx_store(n_ch - 1, sum((idx_weigh(pp, cols) for pp, cols in enumerate(idx_cols)),
                            jnp.zeros((chunk, BLOCK), F32)))

    def count(pred):
        def body(j, cnt):
            return cnt + jnp.where(pred(sc_ref[pl.ds(k_start(j), chunk), :]), 1.0, 0.0)
        cnt = lax.fori_loop(0, n_ch, body, jnp.zeros((chunk, BLOCK), F32))
        return jnp.sum(cnt, axis=0, keepdims=True)

    kf = float(topk)
    n_all = (n_ch * chunk).astype(F32)
    n_pos = count(lambda s: s >= 0.0)
    pos_ok = n_pos >= kf
    key = jnp.where(pos_ok, jnp.int32(0), jnp.int32(-2 ** 31))
    n_key = jnp.where(pos_ok, n_pos, n_all)

    def bit_body(b, carry):
        key, n_key = carry
        trial = key | lax.shift_left(jnp.int32(1), 30 - b)
        cand = _key_to_f32(trial)
        n_trial = count(lambda s: s >= cand)
        ok = n_trial >= kf
        return jnp.where(ok, trial, key), jnp.where(ok, n_trial, n_key)

    key, n_key = lax.fori_loop(0, 31, bit_body, (key, n_key))
    tq = t0 + lax.broadcasted_iota(jnp.int32, (1, BLOCK), 1)
    few = tq < topk - 1
    tau = jnp.where(few, F32_MIN, _key_to_f32(key))

    surplus = jnp.where(few, 0.0, n_key - kf)

    @pl.when(jnp.max(surplus) > 0.0)
    def _():
        n_tie = kf - count(lambda s: s > tau)

        def tie_body(j, seen):
            k0 = k_start(j)
            s = sc_ref[pl.ds(k0, chunk), :]
            eq = jnp.where(s == tau, 1.0, 0.0)
            rank_eq = seen + jnp.dot(tri_ref[...], eq.astype(BF16), preferred_element_type=F32)
            drop = jnp.where(rank_eq > n_tie, eq, 0.0)
            sc_ref[pl.ds(k0, chunk), :] = jnp.where(drop > 0.0, -jnp.inf, s)
            return seen + jnp.sum(eq, axis=0, keepdims=True)

        lax.fori_loop(0, n_ch, tie_body, jnp.zeros((1, BLOCK), F32))

    pad_rows = ckv_ref.shape[-1] - rank - POS_ROWS
    q_aug = jnp.concatenate([heads(qlt_ref, rank), qpos_ref[0],
                             jnp.zeros((pad_rows, nh * BLOCK), BF16)], axis=0)
    acc_ref[...] = jnp.zeros(acc_ref.shape, F32)
    pair = 2 * BLOCK
    n_pairs = nh * BLOCK // pair
    col_slices = [slice(pp * pair, (pp + 1) * pair) for pp in range(n_pairs)]

    def pair_mask(j):
        mask = jnp.where(sc_ref[pl.ds(k_start(j), chunk), :] >= tau, 0.0, NEG)
        return jnp.concatenate([mask, mask], axis=-1)

    def logits_step(j, cols, mask2):
        s = jnp.dot(ckv_ref[0, pl.ds(k_start(j), chunk), :], q_aug[:, cols], preferred_element_type=F32)
        s = s + mask2
        s_ref[:, cols] = s
        return jnp.max(s, axis=0, keepdims=True)

    def value_step(j, cols, m_use, alpha):
        p = jnp.exp2(s_ref[:, cols] - m_use)
        pv = jnp.dot(ckvt_ref[0, :, pl.ds(k_start(j), chunk)], p.astype(BF16), preferred_element_type=F32)
        acc_ref[:, cols] = alpha * acc_ref[:, cols] + pv
        return jnp.sum(p, axis=0, keepdims=True)

    cat = lambda parts: jnp.concatenate(parts, axis=-1)
    mask_first = pair_mask(0)
    m_first = cat([logits_step(0, cols, mask_first) for cols in col_slices])

    def att_body(j, carry):
        m_prev, alpha_prev, l_prev = carry
        mask2 = pair_mask(j)
        sums, cmax = [], []
        for cols in col_slices:
            sums.append(value_step(j - 1, cols, m_prev[:, cols], alpha_prev[:, cols]))
            cmax.append(logits_step(j, cols, mask2))
        m_new = jnp.maximum(m_prev, cat(cmax))
        return m_new, jnp.exp2(m_prev - m_new), alpha_prev * l_prev + cat(sums)

    zeros_row = jnp.zeros((1, nh * BLOCK), F32)
    m_last, alpha_last, l_part = lax.fori_loop(1, n_ch, att_body, (m_first, zeros_row, zeros_row))
    l_fin = alpha_last * l_part + cat(
        [value_step(n_ch - 1, cols, m_last[:, cols], alpha_last[:, cols]) for cols in col_slices])

    o_lat_t = acc_ref[...] / l_fin
    outs = [jnp.dot(o_lat_t[:, hh * BLOCK:(hh + 1) * BLOCK].T.astype(BF16), wuv_ref[hh],
                    preferred_element_type=F32) for hh in range(nh)]
    o_ref[0] = jnp.concatenate(outs, axis=-1).astype(BF16)


def _alibi_query_table(slopes, n_blocks):
    sl = np.asarray(slopes, np.float64) * LOG2E
    t0 = np.arange(n_blocks, dtype=np.float64)[:, None] * BLOCK
    coef = [np.broadcast_to(sl * POS_RADIX, (n_blocks, sl.size)),
            np.broadcast_to(sl, (n_blocks, sl.size)),
            -sl[None, :] * t0]
    rows = []
    for cf in coef:
        hi = jnp.asarray(cf, F32).astype(BF16)
        lo = (jnp.asarray(cf, F32) - hi.astype(F32)).astype(BF16)
        rows += [hi, lo]
    tab = jnp.stack(rows + [jnp.zeros_like(rows[0])] * (POS_ROWS - len(rows)), axis=1)
    return jnp.repeat(tab, BLOCK, axis=-1)


def _dsa_call(slopes, qit, wit, ki, qlt, ckv, ckvt, w_uv, *, chunk=256):
    bsz, seq, ckv_w = ckv.shape
    rank = ckvt.shape[1]
    nh, _, hd = w_uv.shape
    topk = min(TOPK_MAX, seq // 4)
    assert seq % chunk == 0 and seq // POS_RADIX <= 256
    tri = jnp.asarray(np.tril(np.ones((chunk, chunk), np.float32)), BF16)
    qpos = _alibi_query_table(slopes, seq // BLOCK)
    return pl.pallas_call(
        functools.partial(_dsa_kernel, topk=topk, chunk=chunk),
        grid=(bsz, seq // BLOCK),
        in_specs=[pl.BlockSpec((1, IDX_HEADS * IDX_DIM, BLOCK), lambda b, i: (b, 0, i)),
                  pl.BlockSpec((1, IDX_HEADS, BLOCK), lambda b, i: (b, 0, i)),
                  pl.BlockSpec((1, seq, IDX_DIM), lambda b, i: (b, 0, 0)),
                  pl.BlockSpec((1, nh * rank, BLOCK), lambda b, i: (b, 0, i)),
                  pl.BlockSpec((1, POS_ROWS, nh * BLOCK), lambda b, i: (i, 0, 0)),
                  pl.BlockSpec((1, seq, ckv_w), lambda b, i: (b, 0, 0)),
                  pl.BlockSpec((1, rank, seq), lambda b, i: (b, 0, 0)),
                  _const_spec((nh, rank, hd)),
                  _const_spec((chunk, chunk))],
        out_specs=pl.BlockSpec((1, BLOCK, nh * hd), lambda b, i: (b, i, 0)),
        out_shape=jax.ShapeDtypeStruct((bsz, seq, nh * hd), BF16),
        scratch_shapes=[pltpu.VMEM((seq, BLOCK), F32),
                        pltpu.VMEM((rank, nh * BLOCK), F32),
                        pltpu.VMEM((chunk, nh * BLOCK), F32)],
        compiler_params=_cparams(("parallel", "arbitrary")),
        name="sparse_attn",
    )(qit, wit, ki, qlt, qpos, ckv, ckvt, w_uv, tri)


def _out_kernel(x_ref, oa_ref, ob_ref, mod_ref, wo_ref, g2_ref, wgu_ref, wd_ref, gf_ref, o_ref,
                *, d_ff, ff_chunks, final_norm):
    mixed = jnp.concatenate([oa_ref[0], ob_ref[0]], axis=-1)
    x1 = x_ref[0] + mod_ref[0, 2:3, :] * jnp.dot(mixed, wo_ref[...], preferred_element_type=F32)
    h2 = _rms(x1) * g2_ref[...]
    h2 = (h2 * (1.0 + mod_ref[0, 4:5, :]) + mod_ref[0, 3:4, :]).astype(BF16)
    fc = d_ff // ff_chunks
    ffn = jnp.zeros(x1.shape, F32)
    for cc in range(ff_chunks):
        gate = jnp.dot(h2, wgu_ref[:, cc * fc:(cc + 1) * fc], preferred_element_type=F32)
        up = jnp.dot(h2, wgu_ref[:, d_ff + cc * fc:d_ff + (cc + 1) * fc], preferred_element_type=F32)
        act = gate * (1.0 / (1.0 + jnp.exp(-gate))) * up
        ffn = ffn + jnp.dot(act.astype(BF16), wd_ref[cc * fc:(cc + 1) * fc, :], preferred_element_type=F32)
    x2 = x1 + mod_ref[0, 5:6, :] * ffn
    if final_norm:
        x2 = _rms(x2) * gf_ref[...]
    o_ref[0] = x2


def _out_call(x, oa, ob, mod, w_out, g_ffn, w_gu, w_down, g_final, *, tm, final_norm):
    bsz, seq, d = x.shape
    d_ff = w_down.shape[0]
    ff_chunks = 2 if d_ff % (2 * LANES) == 0 else 1
    row = lambda w: pl.BlockSpec((1, tm, w), lambda b, i: (b, i, 0))
    return pl.pallas_call(
        functools.partial(_out_kernel, d_ff=d_ff, ff_chunks=ff_chunks, final_norm=final_norm),
        grid=(bsz, seq // tm),
        in_specs=[row(d), row(oa.shape[-1]), row(ob.shape[-1]),
                  pl.BlockSpec((1, 6, d), lambda b, i: (b, 0, 0)),
                  _const_spec((d, d)), _const_spec((1, d)),
                  _const_spec((d, 2 * d_ff)), _const_spec((d_ff, d)), _const_spec((1, d))],
        out_specs=row(d),
        out_shape=jax.ShapeDtypeStruct((bsz, seq, d), F32),
        compiler_params=_cparams(("parallel", "parallel")),
        name="out_proj_ffn",
    )(x, oa, ob, mod, w_out, g_ffn, w_gu, w_down, g_final)


def _alibi_slopes(n_heads):
    s = 2.0 ** (-8.0 * (np.arange(n_heads, dtype=np.float32) + 1.0) / n_heads)
    return s[0::2], s[1::2]


def kernel(x, c, w_ada, b_ada, g_attn, w_in, kv_norm_g, w_uk, w_uv, w_out, g_ffn, w_gu, w_down, g_final):
    bsz, seq, d = x.shape
    depth = w_ada.shape[0]
    nh = w_uk.shape[1]
    slopes_a, slopes_b = _alibi_slopes(2 * nh)
    d_in = w_in.shape[-1]
    n_pad = -(-d_in // LANES) * LANES
    tm = 512 if seq % 512 == 0 else BLOCK
    for l in range(depth):
        mod = _mod_call(c, w_ada[l], b_ada[l]).reshape(bsz, 6, d)
        w_in_p = jnp.pad(w_in[l], ((0, 0), (0, n_pad - d_in))).astype(BF16)
        qa, ka, va, qlt, ckv, ckvt, qit, ki, wit = _in_call(
            x, mod, g_attn[l].reshape(1, d), w_in_p, w_uk[l].astype(BF16),
            kv_norm_g[l].reshape(1, -1), nh=nh, tm=tm)
        out_a = _dil_call(slopes_a, qa, ka, va)
        out_b = _dsa_call(slopes_b, qit, wit, ki, qlt, ckv, ckvt, w_uv[l].astype(BF16))
        x = _out_call(x, out_a, out_b, mod, w_out[l].astype(BF16), g_ffn[l].reshape(1, d),
                      w_gu[l].astype(BF16), w_down[l].astype(BF16), g_final.reshape(1, d),
                      tm=tm, final_norm=(l == depth - 1))
    return x
```

```python
import functools

import numpy as np
import jax
import jax.numpy as jnp
from jax import lax
from jax.experimental import pallas as pl
from jax.experimental.pallas import tpu as pltpu

F32 = jnp.float32
BF16 = jnp.bfloat16

HEAD_DIM = 64
BLOCK = 128
DILATED_CONFIGS = ((128, 1), (512, 4), (2048, 16))
IDX_HEADS = 8
IDX_DIM = 64
TOPK_MAX = 256
EPS = 1e-6
NEG = -1e30
F32_MIN = float(np.finfo(np.float32).min)
LOG2E = float(np.log2(np.e))
POS_RADIX = 64
POS_ROWS = 16
LANES = 128
SUBLANES = 8
COUNT_CHAINS = 8
VMEM_LIMIT = 56 * 1024 * 1024

_NT = (((1,), (1,)), ((), ()))


def _cparams(sem, flags=None):
    return pltpu.CompilerParams(dimension_semantics=sem, vmem_limit_bytes=VMEM_LIMIT, flags=flags)


def _const_spec(shape):
    nd = len(shape)
    return pl.BlockSpec(shape, lambda *_: (0,) * nd, pipeline_mode=pl.Buffered(1))


def _mod_kernel(c_ref, w_ref, b_ref, o_ref):
    c = c_ref[...]
    c_act = c * (1.0 / (1.0 + jnp.exp(-c)))
    o_ref[...] = jnp.dot(c_act, w_ref[...], preferred_element_type=F32,
                         precision=lax.Precision.HIGHEST) + b_ref[...]


def _mod_call(c, w_ada, b_ada):
    bsz, d = c.shape
    n = w_ada.shape[1]
    tn = 1024
    return pl.pallas_call(
        _mod_kernel,
        grid=(n // tn,),
        in_specs=[pl.BlockSpec((bsz, d), lambda j: (0, 0)),
                  pl.BlockSpec((d, tn), lambda j: (0, j)),
                  pl.BlockSpec((1, tn), lambda j: (0, j))],
        out_specs=pl.BlockSpec((bsz, tn), lambda j: (0, j)),
        out_shape=jax.ShapeDtypeStruct((bsz, n), F32),
        compiler_params=_cparams(("arbitrary",)),
        name="adaln_mod",
    )(c, w_ada, b_ada.reshape(1, n))


def _rms(x):
    return x * lax.rsqrt(jnp.mean(x * x, axis=-1, keepdims=True) + EPS)


def _in_kernel(x_ref, mod_ref, g_ref, w_ref, wuk_ref, kvg_ref,
               qa_ref, ka_ref, va_ref, qlt_ref, ckv_ref, ckvt_ref, qit_ref, ki_ref, wit_ref,
               *, nh, idx_scale):
    hd = HEAD_DIM
    h = _rms(x_ref[0]) * g_ref[...]
    h = h * (1.0 + mod_ref[0, 1:2, :]) + mod_ref[0, 0:1, :]
    proj = jnp.dot(h.astype(BF16), w_ref[...], preferred_element_type=F32)
    wa = nh * hd
    rank = ckvt_ref.shape[1]
    tm = x_ref.shape[1]
    o_qb, o_ckv = 3 * wa, 4 * wa
    o_qi = o_ckv + rank
    o_ki = o_qi + IDX_HEADS * IDX_DIM
    qa_ref[0] = proj[:, 0:wa] * (hd ** -0.5)
    ka_ref[0] = proj[:, wa:2 * wa]
    va_ref[0] = proj[:, 2 * wa:3 * wa]
    for hh in range(nh):
        qb_h = proj[:, o_qb + hh * hd:o_qb + (hh + 1) * hd].astype(BF16)
        q_lat = jnp.dot(qb_h, wuk_ref[hh], preferred_element_type=F32) * (hd ** -0.5 * LOG2E)
        qlt_ref[0, hh * rank:(hh + 1) * rank, :] = q_lat.T.astype(BF16)
    ckv = (_rms(proj[:, o_ckv:o_ckv + rank]) * kvg_ref[...])
    ckvt_ref[0] = ckv.T.astype(BF16)
    t = pl.program_id(1) * tm + lax.broadcasted_iota(jnp.int32, (tm, LANES), 0)
    lane = lax.broadcasted_iota(jnp.int32, (tm, LANES), 1)
    pos = jnp.where(lane < 2, t // POS_RADIX, jnp.where(lane < 4, t % POS_RADIX, jnp.where(lane < 6, 1, 0)))
    ckv_ref[0] = jnp.concatenate([ckv, pos.astype(F32)], axis=-1).astype(BF16)
    for pp in range(IDX_HEADS * IDX_DIM // LANES):
        qit_ref[0, pp * LANES:(pp + 1) * LANES, :] = proj[:, o_qi + pp * LANES:o_qi + (pp + 1) * LANES].T.astype(BF16)
    kiw = proj[:, o_ki:o_ki + LANES]
    ki_ref[0] = kiw[:, :IDX_DIM].astype(BF16)
    wit_ref[0] = kiw.T[IDX_DIM:IDX_DIM + IDX_HEADS, :] * idx_scale


def _in_call(x, mod, g, w_in_p, w_uk, kvg, *, nh, tm):
    bsz, seq, d = x.shape
    n_pad = w_in_p.shape[1]
    rank = w_uk.shape[-1]
    hd = HEAD_DIM
    wa = nh * hd
    idx_scale = float((IDX_HEADS * IDX_DIM) ** -0.5)
    row_spec = lambda w: pl.BlockSpec((1, tm, w), lambda b, i: (b, i, 0))
    col_spec = lambda r: pl.BlockSpec((1, r, tm), lambda b, i: (b, 0, i))
    return pl.pallas_call(
        functools.partial(_in_kernel, nh=nh, idx_scale=idx_scale),
        grid=(bsz, seq // tm),
        in_specs=[pl.BlockSpec((1, tm, d), lambda b, i: (b, i, 0)),
                  pl.BlockSpec((1, 6, d), lambda b, i: (b, 0, 0)),
                  _const_spec((1, d)),
                  _const_spec((d, n_pad)),
                  _const_spec((nh, hd, rank)),
                  _const_spec((1, rank))],
        out_specs=[row_spec(wa), row_spec(wa), row_spec(wa),
                   col_spec(nh * rank), row_spec(rank + LANES), col_spec(rank),
                   col_spec(IDX_HEADS * IDX_DIM), row_spec(IDX_DIM), col_spec(IDX_HEADS)],
        out_shape=[jax.ShapeDtypeStruct((bsz, seq, wa), F32),
                   jax.ShapeDtypeStruct((bsz, seq, wa), F32),
                   jax.ShapeDtypeStruct((bsz, seq, wa), F32),
                   jax.ShapeDtypeStruct((bsz, nh * rank, seq), BF16),
                   jax.ShapeDtypeStruct((bsz, seq, rank + LANES), BF16),
                   jax.ShapeDtypeStruct((bsz, rank, seq), BF16),
                   jax.ShapeDtypeStruct((bsz, IDX_HEADS * IDX_DIM, seq), BF16),
                   jax.ShapeDtypeStruct((bsz, seq, IDX_DIM), BF16),
                   jax.ShapeDtypeStruct((bsz, IDX_HEADS, seq), F32)],
        compiler_params=_cparams(("parallel", "parallel")),
        name="in_proj",
    )(x, mod, g, w_in_p, w_uk, kvg)


def _band_tables(tq, w_res):
    r = np.arange(tq)[:, None]
    d0 = r - np.arange(tq)[None, :]
    d1 = r - np.arange(tq + w_res)[None, :] + w_res
    clean = lambda d: np.where((d >= 0) & (d <= w_res), d, -1).astype(np.float32)
    return jnp.asarray(clean(d0)), jnp.asarray(clean(d1))


def _dil_kernel(slope_ref, q_ref, k_ref, v_ref, d0_ref, d1_ref, o_ref,
                m_ref, l_ref, acc_ref, b0_ref, b1_ref, *, tq, w_res, configs):
    seq = q_ref.shape[1]
    hp = pl.program_id(1)
    lane = lax.broadcasted_iota(jnp.int32, (1, LANES), 1)
    first = lane < HEAD_DIM
    m_ref[...] = jnp.full(m_ref.shape, NEG, F32)
    l_ref[...] = jnp.zeros(l_ref.shape, F32)
    acc_ref[...] = jnp.zeros(acc_ref.shape, F32)

    def rows(start, size, dil):
        return pl.ds(start, size) if dil == 1 else pl.ds(start, size, stride=dil)

    def tile(q_rows, k_rows, bias_ref):
        q2 = q_ref[0, q_rows, :]
        k2 = k_ref[0, k_rows, :].astype(BF16)
        v2 = v_ref[0, k_rows, :].astype(BF16)
        qs = [jnp.where(first if hh == 0 else jnp.logical_not(first), q2, 0.0).astype(BF16)
              for hh in range(2)]
        ss = [lax.dot_general(qs[hh], k2, _NT, preferred_element_type=F32) + bias_ref[hh] for hh in range(2)]
        ms = [jnp.max(s, axis=-1, keepdims=True) for s in ss]
        ps = [jnp.exp(s - m_loc) for s, m_loc in zip(ss, ms)]
        ls = [jnp.sum(p, axis=-1, keepdims=True) for p in ps]
        os_ = [jnp.dot(p.astype(BF16), v2, preferred_element_type=F32) for p in ps]
        parts = list(zip(ms, ls, os_))
        pick = lambda a, b: jnp.where(first, a, b)
        m_t = pick(parts[0][0], parts[1][0])
        l_t = pick(parts[0][1], parts[1][1])
        o_t = pick(parts[0][2], parts[1][2])
        m_old = m_ref[q_rows, :]
        m_new = jnp.maximum(m_old, m_t)
        a_old = jnp.exp(m_old - m_new)
        a_new = jnp.exp(m_t - m_new)
        acc_ref[q_rows, :] = a_old * acc_ref[q_rows, :] + a_new * o_t
        l_ref[q_rows, :] = a_old * l_ref[q_rows, :] + a_new * l_t
        m_ref[q_rows, :] = m_new

    for _, dil in configs:
        n = seq // dil
        for hh in range(2):
            neg_slope = -slope_ref[hp * 2 + hh] * float(dil)
            b0_ref[hh] = jnp.where(d0_ref[...] >= 0.0, neg_slope * d0_ref[...], NEG)
            b1_ref[hh] = jnp.where(d1_ref[...] >= 0.0, neg_slope * d1_ref[...], NEG)

        def residue_body(r, carry, dil=dil, n=n):
            tile(rows(r, tq, dil), rows(r, tq, dil), b0_ref)

            def tile_body(jt, c):
                q0 = jt * (tq * dil) + r
                tile(rows(q0, tq, dil), rows(q0 - w_res * dil, tq + w_res, dil), b1_ref)
                return c

            if n // tq > 1:
                lax.fori_loop(1, n // tq, tile_body, 0)
            return carry

        lax.fori_loop(0, dil, residue_body, 0)

    o_ref[0] = (acc_ref[...] / l_ref[...]).astype(BF16)


def _dil_call(slopes, qa, ka, va):
    bsz, seq, wa = qa.shape
    w_res = DILATED_CONFIGS[0][0] // DILATED_CONFIGS[0][1]
    assert all(w // d == w_res for w, d in DILATED_CONFIGS)
    max_dil = max(d for _, d in DILATED_CONFIGS)
    tq = min(256, seq // max_dil)
    assert tq >= w_res and all((seq // d) % tq == 0 for _, d in DILATED_CONFIGS)
    d0, d1 = _band_tables(tq, w_res)
    blk = pl.BlockSpec((1, seq, LANES), lambda b, h: (b, 0, h))
    return pl.pallas_call(
        functools.partial(_dil_kernel, tq=tq, w_res=w_res, configs=DILATED_CONFIGS),
        grid=(bsz, wa // LANES),
        in_specs=[pl.BlockSpec(memory_space=pltpu.SMEM), blk, blk, blk,
                  _const_spec(d0.shape), _const_spec(d1.shape)],
        out_specs=blk,
        out_shape=jax.ShapeDtypeStruct((bsz, seq, wa), BF16),
        scratch_shapes=[pltpu.VMEM((seq, LANES), F32), pltpu.VMEM((seq, LANES), F32),
                        pltpu.VMEM((seq, LANES), F32),
                        pltpu.VMEM((2,) + d0.shape, F32), pltpu.VMEM((2,) + d1.shape, F32)],
        compiler_params=_cparams(("parallel", "parallel")),
        name="dilated_attn",
    )(jnp.asarray(slopes, F32), qa, ka, va, d0, d1)


def _key_to_f32(key):
    bits = key ^ (lax.shift_right_arithmetic(key, 31) & jnp.int32(0x7FFFFFFF))
    return lax.bitcast_convert_type(bits, F32)


def _dsa_kernel(qit_ref, wit_ref, ki_ref, qlt_ref, qpos_ref, ckv_ref, ckvt_ref, wuv_ref, tri_ref, o_ref,
                sc_ref, acc_ref, s_ref, *, topk, chunk):
    i = pl.program_id(1)
    t0 = i * BLOCK
    rank = ckvt_ref.shape[1]
    nh = qlt_ref.shape[1] // rank
    n_ch = (t0 + BLOCK + chunk - 1) // chunk
    kk = lax.broadcasted_iota(jnp.int32, (chunk, BLOCK), 0)
    qq = lax.broadcasted_iota(jnp.int32, (chunk, BLOCK), 1)
    rel = qq - kk
    k_start = lambda j: pl.multiple_of(j * chunk, chunk)
    heads = lambda ref, width: jnp.concatenate(
        [ref[0, hh * width:(hh + 1) * width, :] for hh in range(ref.shape[1] // width)], axis=-1)

    qit = heads(qit_ref, IDX_DIM)
    wt = wit_ref[0]

    idx_pair = 2 * BLOCK
    idx_cols = [slice(pp * idx_pair, (pp + 1) * idx_pair) for pp in range(IDX_HEADS * BLOCK // idx_pair)]

    def idx_logits(j, cols):
        s_ref[:, cols] = jnp.dot(ki_ref[0, pl.ds(k_start(j), chunk), :], qit[:, cols], preferred_element_type=F32)

    def idx_weigh(pp, cols):
        lg = s_ref[:, cols]
        return (wt[2 * pp:2 * pp + 1, :] * jnp.maximum(lg[:, :BLOCK], 0.0)
                + wt[2 * pp + 1:2 * pp + 2, :] * jnp.maximum(lg[:, BLOCK:], 0.0))

    def idx_store(j, score):
        k0 = k_start(j)
        score = jnp.where(score == 0.0, 0.0, score)
        sc_ref[pl.ds(k0, chunk), :] = jnp.where(rel + (t0 - k0) >= 0, score, -jnp.inf)

    for cols in idx_cols:
        idx_logits(0, cols)

    def score_body(j, carry):
        score = jnp.zeros((chunk, BLOCK), F32)
        for pp, cols in enumerate(idx_cols):
            score = score + idx_weigh(pp, cols)
            idx_logits(j, cols)
        idx_store(j - 1, score)
        return carry

    lax.fori_loop(1, n_ch, score_body, 0)
    idx_store(n_ch - 1, sum((idx_weigh(pp, cols) for pp, cols in enumerate(idx_cols)),
                            jnp.zeros((chunk, BLOCK), F32)))

    def count(pred):
        def body(j, cnt):
            hit = jnp.where(pred(sc_ref[pl.ds(k_start(j), chunk), :]), 1.0, 0.0)
            return cnt + jnp.sum(hit.reshape(-1, COUNT_CHAINS, SUBLANES, BLOCK), axis=0)
        cnt = lax.fori_loop(0, n_ch, body, jnp.zeros((COUNT_CHAINS, SUBLANES, BLOCK), F32))
        return jnp.sum(jnp.sum(cnt, axis=0), axis=0, keepdims=True)

    kf = float(topk)
    n_all = (n_ch * chunk).astype(F32)
    n_pos = count(lambda s: s >= 0.0)
    pos_ok = n_pos >= kf
    key = jnp.where(pos_ok, jnp.int32(0), jnp.int32(-2 ** 31))
    n_key = jnp.where(pos_ok, n_pos, n_all)

    def bit_body(b, carry):
        key, n_key = carry
        trial = key | lax.shift_left(jnp.int32(1), 30 - b)
        cand = _key_to_f32(trial)
        n_trial = count(lambda s: s >= cand)
        ok = n_trial >= kf
        return jnp.where(ok, trial, key), jnp.where(ok, n_trial, n_key)

    key, n_key = lax.fori_loop(0, 31, bit_body, (key, n_key))
    tq = t0 + lax.broadcasted_iota(jnp.int32, (1, BLOCK), 1)
    few = tq < topk - 1
    tau = jnp.where(few, F32_MIN, _key_to_f32(key))

    surplus = jnp.where(few, 0.0, n_key - kf)

    @pl.when(jnp.max(surplus) > 0.0)
    def _():
        n_tie = kf - count(lambda s: s > tau)

        def tie_body(j, seen):
            k0 = k_start(j)
            s = sc_ref[pl.ds(k0, chunk), :]
            eq = jnp.where(s == tau, 1.0, 0.0)
            rank_eq = seen + jnp.dot(tri_ref[...], eq.astype(BF16), preferred_element_type=F32)
            drop = jnp.where(rank_eq > n_tie, eq, 0.0)
            sc_ref[pl.ds(k0, chunk), :] = jnp.where(drop > 0.0, -jnp.inf, s)
            return seen + jnp.sum(eq, axis=0, keepdims=True)

        lax.fori_loop(0, n_ch, tie_body, jnp.zeros((1, BLOCK), F32))

    pad_rows = ckv_ref.shape[-1] - rank - POS_ROWS
    q_aug = jnp.concatenate([heads(qlt_ref, rank), qpos_ref[0],
                             jnp.zeros((pad_rows, nh * BLOCK), BF16)], axis=0)
    acc_ref[...] = jnp.zeros(acc_ref.shape, F32)
    pair = 2 * BLOCK
    n_pairs = nh * BLOCK // pair
    col_slices = [slice(pp * pair, (pp + 1) * pair) for pp in range(n_pairs)]

    def pair_mask(j):
        mask = jnp.where(sc_ref[pl.ds(k_start(j), chunk), :] >= tau, 0.0, NEG)
        return jnp.concatenate([mask, mask], axis=-1)

    def logits_step(j, cols, mask2):
        s = jnp.dot(ckv_ref[0, pl.ds(k_start(j), chunk), :], q_aug[:, cols], preferred_element_type=F32)
        s = s + mask2
        s_ref[:, cols] = s
        return jnp.max(s, axis=0, keepdims=True)

    def value_step(j, cols, m_use, alpha):
        p = jnp.exp2(s_ref[:, cols] - m_use)
        pv = jnp.dot(ckvt_ref[0, :, pl.ds(k_start(j), chunk)], p.astype(BF16), preferred_element_type=F32)
        acc_ref[:, cols] = alpha * acc_ref[:, cols] + pv
        return jnp.sum(p, axis=0, keepdims=True)

    cat = lambda parts: jnp.concatenate(parts, axis=-1)
    mask_first = pair_mask(0)
    m_first = cat([logits_step(0, cols, mask_first) for cols in col_slices])

    def att_body(j, carry):
        m_prev, alpha_prev, l_prev = carry
        mask2 = pair_mask(j)
        sums, cmax = [], []
        for cols in col_slices:
            sums.append(value_step(j - 1, cols, m_prev[:, cols], alpha_prev[:, cols]))
            cmax.append(logits_step(j, cols, mask2))
        m_new = jnp.maximum(m_prev, cat(cmax))
        return m_new, jnp.exp2(m_prev - m_new), alpha_prev * l_prev + cat(sums)

    zeros_row = jnp.zeros((1, nh * BLOCK), F32)
    m_last, alpha_last, l_part = lax.fori_loop(1, n_ch, att_body, (m_first, zeros_row, zeros_row))
    l_fin = alpha_last * l_part + cat(
        [value_step(n_ch - 1, cols, m_last[:, cols], alpha_last[:, cols]) for cols in col_slices])

    o_lat_t = acc_ref[...] / l_fin
    outs = [jnp.dot(o_lat_t[:, hh * BLOCK:(hh + 1) * BLOCK].T.astype(BF16), wuv_ref[hh],
                    preferred_element_type=F32) for hh in range(nh)]
    o_ref[0] = jnp.concatenate(outs, axis=-1).astype(BF16)


def _alibi_query_table(slopes, n_blocks):
    sl = np.asarray(slopes, np.float64) * LOG2E
    t0 = np.arange(n_blocks, dtype=np.float64)[:, None] * BLOCK
    coef = [np.broadcast_to(sl * POS_RADIX, (n_blocks, sl.size)),
            np.broadcast_to(sl, (n_blocks, sl.size)),
            -sl[None, :] * t0]
    rows = []
    for cf in coef:
        hi = jnp.asarray(cf, F32).astype(BF16)
        lo = (jnp.asarray(cf, F32) - hi.astype(F32)).astype(BF16)
        rows += [hi, lo]
    tab = jnp.stack(rows + [jnp.zeros_like(rows[0])] * (POS_ROWS - len(rows)), axis=1)
    return jnp.repeat(tab, BLOCK, axis=-1)


def _dsa_call(slopes, qit, wit, ki, qlt, ckv, ckvt, w_uv, *, chunk=256):
    bsz, seq, ckv_w = ckv.shape
    rank = ckvt.shape[1]
    nh, _, hd = w_uv.shape
    topk = min(TOPK_MAX, seq // 4)
    assert seq % chunk == 0 and seq // POS_RADIX <= 256
    tri = jnp.asarray(np.tril(np.ones((chunk, chunk), np.float32)), BF16)
    qpos = _alibi_query_table(slopes, seq // BLOCK)
    return pl.pallas_call(
        functools.partial(_dsa_kernel, topk=topk, chunk=chunk),
        grid=(bsz, seq // BLOCK),
        in_specs=[pl.BlockSpec((1, IDX_HEADS * IDX_DIM, BLOCK), lambda b, i: (b, 0, i)),
                  pl.BlockSpec((1, IDX_HEADS, BLOCK), lambda b, i: (b, 0, i)),
                  pl.BlockSpec((1, seq, IDX_DIM), lambda b, i: (b, 0, 0)),
                  pl.BlockSpec((1, nh * rank, BLOCK), lambda b, i: (b, 0, i)),
                  pl.BlockSpec((1, POS_ROWS, nh * BLOCK), lambda b, i: (i, 0, 0)),
                  pl.BlockSpec((1, seq, ckv_w), lambda b, i: (b, 0, 0)),
                  pl.BlockSpec((1, rank, seq), lambda b, i: (b, 0, 0)),
                  _const_spec((nh, rank, hd)),
                  _const_spec((chunk, chunk))],
        out_specs=pl.BlockSpec((1, BLOCK, nh * hd), lambda b, i: (b, i, 0)),
        out_shape=jax.ShapeDtypeStruct((bsz, seq, nh * hd), BF16),
        scratch_shapes=[pltpu.VMEM((seq, BLOCK), F32),
                        pltpu.VMEM((rank, nh * BLOCK), F32),
                        pltpu.VMEM((chunk, nh * BLOCK), F32)],
        compiler_params=_cparams(("parallel", "arbitrary")),
        name="sparse_attn",
    )(qit, wit, ki, qlt, qpos, ckv, ckvt, w_uv, tri)


def _out_kernel(x_ref, oa_ref, ob_ref, mod_ref, wo_ref, g2_ref, wgu_ref, wd_ref, gf_ref, o_ref,
                *, d_ff, ff_chunks, final_norm):
    mixed = jnp.concatenate([oa_ref[0], ob_ref[0]], axis=-1)
    x1 = x_ref[0] + mod_ref[0, 2:3, :] * jnp.dot(mixed, wo_ref[...], preferred_element_type=F32)
    h2 = _rms(x1) * g2_ref[...]
    h2 = (h2 * (1.0 + mod_ref[0, 4:5, :]) + mod_ref[0, 3:4, :]).astype(BF16)
    fc = d_ff // ff_chunks
    ffn = jnp.zeros(x1.shape, F32)
    for cc in range(ff_chunks):
        gate = jnp.dot(h2, wgu_ref[:, cc * fc:(cc + 1) * fc], preferred_element_type=F32)
        up = jnp.dot(h2, wgu_ref[:, d_ff + cc * fc:d_ff + (cc + 1) * fc], preferred_element_type=F32)
        act = gate * (1.0 / (1.0 + jnp.exp(-gate))) * up
        ffn = ffn + jnp.dot(act.astype(BF16), wd_ref[cc * fc:(cc + 1) * fc, :], preferred_element_type=F32)
    x2 = x1 + mod_ref[0, 5:6, :] * ffn
    if final_norm:
        x2 = _rms(x2) * gf_ref[...]
    o_ref[0] = x2


def _out_call(x, oa, ob, mod, w_out, g_ffn, w_gu, w_down, g_final, *, tm, final_norm):
    bsz, seq, d = x.shape
    d_ff = w_down.shape[0]
    ff_chunks = 2 if d_ff % (2 * LANES) == 0 else 1
    row = lambda w: pl.BlockSpec((1, tm, w), lambda b, i: (b, i, 0))
    return pl.pallas_call(
        functools.partial(_out_kernel, d_ff=d_ff, ff_chunks=ff_chunks, final_norm=final_norm),
        grid=(bsz, seq // tm),
        in_specs=[row(d), row(oa.shape[-1]), row(ob.shape[-1]),
                  pl.BlockSpec((1, 6, d), lambda b, i: (b, 0, 0)),
                  _const_spec((d, d)), _const_spec((1, d)),
                  _const_spec((d, 2 * d_ff)), _const_spec((d_ff, d)), _const_spec((1, d))],
        out_specs=row(d),
        out_shape=jax.ShapeDtypeStruct((bsz, seq, d), F32),
        compiler_params=_cparams(("parallel", "parallel")),
        name="out_proj_ffn",
    )(x, oa, ob, mod, w_out, g_ffn, w_gu, w_down, g_final)


def _alibi_slopes(n_heads):
    s = 2.0 ** (-8.0 * (np.arange(n_heads, dtype=np.float32) + 1.0) / n_heads)
    return s[0::2], s[1::2]


def kernel(x, c, w_ada, b_ada, g_attn, w_in, kv_norm_g, w_uk, w_uv, w_out, g_ffn, w_gu, w_down, g_final):
    bsz, seq, d = x.shape
    depth = w_ada.shape[0]
    nh = w_uk.shape[1]
    slopes_a, slopes_b = _alibi_slopes(2 * nh)
    d_in = w_in.shape[-1]
    n_pad = -(-d_in // LANES) * LANES
    tm = 512 if seq % 512 == 0 else BLOCK
    for l in range(depth):
        mod = _mod_call(c, w_ada[l], b_ada[l]).reshape(bsz, 6, d)
        w_in_p = jnp.pad(w_in[l], ((0, 0), (0, n_pad - d_in))).astype(BF16)
        qa, ka, va, qlt, ckv, ckvt, qit, ki, wit = _in_call(
            x, mod, g_attn[l].reshape(1, d), w_in_p, w_uk[l].astype(BF16),
            kv_norm_g[l].reshape(1, -1), nh=nh, tm=tm)
        out_a = _dil_call(slopes_a, qa, ka, va)
        out_b = _dsa_call(slopes_b, qit, wit, ki, qlt, ckv, ckvt, w_uv[l].astype(BF16))
        x = _out_call(x, out_a, out_b, mod, w_out[l].astype(BF16), g_ffn[l].reshape(1, d),
                      w_gu[l].astype(BF16), w_down[l].astype(BF16), g_final.reshape(1, d),
                      tm=tm, final_norm=(l == depth - 1))
    return x
```

```python
import functools

import numpy as np
import jax
import jax.numpy as jnp
from jax import lax
from jax.experimental import pallas as pl
from jax.experimental.pallas import tpu as pltpu

F32 = jnp.float32
BF16 = jnp.bfloat16

HEAD_DIM = 64
BLOCK = 128
DILATED_CONFIGS = ((128, 1), (512, 4), (2048, 16))
IDX_HEADS = 8
IDX_DIM = 64
TOPK_MAX = 256
EPS = 1e-6
NEG = -1e30
F32_MIN = float(np.finfo(np.float32).min)
LOG2E = float(np.log2(np.e))
POS_RADIX = 64
POS_ROWS = 16
LANES = 128
SUBLANES = 8
COUNT_CHAINS = 8
VMEM_LIMIT = 56 * 1024 * 1024

_NT = (((1,), (1,)), ((), ()))


def _cparams(sem, flags=None):
    return pltpu.CompilerParams(dimension_semantics=sem, vmem_limit_bytes=VMEM_LIMIT, flags=flags)


def _const_spec(shape):
    nd = len(shape)
    return pl.BlockSpec(shape, lambda *_: (0,) * nd, pipeline_mode=pl.Buffered(1))


def _mod_kernel(c_ref, w_ref, b_ref, o_ref):
    c = c_ref[...]
    c_act = c * (1.0 / (1.0 + jnp.exp(-c)))
    o_ref[...] = jnp.dot(c_act, w_ref[...], preferred_element_type=F32,
                         precision=lax.Precision.HIGHEST) + b_ref[...]


def _mod_call(c, w_ada, b_ada):
    bsz, d = c.shape
    n = w_ada.shape[1]
    tn = 1024
    return pl.pallas_call(
        _mod_kernel,
        grid=(n // tn,),
        in_specs=[pl.BlockSpec((bsz, d), lambda j: (0, 0)),
                  pl.BlockSpec((d, tn), lambda j: (0, j)),
                  pl.BlockSpec((1, tn), lambda j: (0, j))],
        out_specs=pl.BlockSpec((bsz, tn), lambda j: (0, j)),
        out_shape=jax.ShapeDtypeStruct((bsz, n), F32),
        compiler_params=_cparams(("arbitrary",)),
        name="adaln_mod",
    )(c, w_ada, b_ada.reshape(1, n))


def _rms(x):
    return x * lax.rsqrt(jnp.mean(x * x, axis=-1, keepdims=True) + EPS)


def _in_kernel(x_ref, mod_ref, g_ref, w_ref, wuk_ref, kvg_ref,
               qa_ref, ka_ref, va_ref, qlt_ref, ckv_ref, ckvt_ref, qit_ref, ki_ref, wit_ref,
               *, nh, idx_scale):
    hd = HEAD_DIM
    h = _rms(x_ref[0]) * g_ref[...]
    h = h * (1.0 + mod_ref[0, 1:2, :]) + mod_ref[0, 0:1, :]
    proj = jnp.dot(h.astype(BF16), w_ref[...], preferred_element_type=F32)
    wa = nh * hd
    rank = ckvt_ref.shape[1]
    tm = x_ref.shape[1]
    o_qb, o_ckv = 3 * wa, 4 * wa
    o_qi = o_ckv + rank
    o_ki = o_qi + IDX_HEADS * IDX_DIM
    qa_ref[0] = proj[:, 0:wa] * (hd ** -0.5 * LOG2E)
    ka_ref[0] = proj[:, wa:2 * wa]
    va_ref[0] = proj[:, 2 * wa:3 * wa]
    for hh in range(nh):
        qb_h = proj[:, o_qb + hh * hd:o_qb + (hh + 1) * hd].astype(BF16)
        q_lat = jnp.dot(qb_h, wuk_ref[hh], preferred_element_type=F32) * (hd ** -0.5 * LOG2E)
        qlt_ref[0, hh * rank:(hh + 1) * rank, :] = q_lat.T.astype(BF16)
    ckv = (_rms(proj[:, o_ckv:o_ckv + rank]) * kvg_ref[...])
    ckvt_ref[0] = ckv.T.astype(BF16)
    t = pl.program_id(1) * tm + lax.broadcasted_iota(jnp.int32, (tm, LANES), 0)
    lane = lax.broadcasted_iota(jnp.int32, (tm, LANES), 1)
    pos = jnp.where(lane < 2, t // POS_RADIX, jnp.where(lane < 4, t % POS_RADIX, jnp.where(lane < 6, 1, 0)))
    ckv_ref[0] = jnp.concatenate([ckv, pos.astype(F32)], axis=-1).astype(BF16)
    for pp in range(IDX_HEADS * IDX_DIM // LANES):
        qit_ref[0, pp * LANES:(pp + 1) * LANES, :] = proj[:, o_qi + pp * LANES:o_qi + (pp + 1) * LANES].T.astype(BF16)
    kiw = proj[:, o_ki:o_ki + LANES]
    ki_ref[0] = kiw[:, :IDX_DIM].astype(BF16)
    wit_ref[0] = kiw.T[IDX_DIM:IDX_DIM + IDX_HEADS, :] * idx_scale


def _in_call(x, mod, g, w_in_p, w_uk, kvg, *, nh, tm):
    bsz, seq, d = x.shape
    n_pad = w_in_p.shape[1]
    rank = w_uk.shape[-1]
    hd = HEAD_DIM
    wa = nh * hd
    idx_scale = float((IDX_HEADS * IDX_DIM) ** -0.5)
    row_spec = lambda w: pl.BlockSpec((1, tm, w), lambda b, i: (b, i, 0))
    col_spec = lambda r: pl.BlockSpec((1, r, tm), lambda b, i: (b, 0, i))
    return pl.pallas_call(
        functools.partial(_in_kernel, nh=nh, idx_scale=idx_scale),
        grid=(bsz, seq // tm),
        in_specs=[pl.BlockSpec((1, tm, d), lambda b, i: (b, i, 0)),
                  pl.BlockSpec((1, 6, d), lambda b, i: (b, 0, 0)),
                  _const_spec((1, d)),
                  _const_spec((d, n_pad)),
                  _const_spec((nh, hd, rank)),
                  _const_spec((1, rank))],
        out_specs=[row_spec(wa), row_spec(wa), row_spec(wa),
                   col_spec(nh * rank), row_spec(rank + LANES), col_spec(rank),
                   col_spec(IDX_HEADS * IDX_DIM), row_spec(IDX_DIM), col_spec(IDX_HEADS)],
        out_shape=[jax.ShapeDtypeStruct((bsz, seq, wa), F32),
                   jax.ShapeDtypeStruct((bsz, seq, wa), F32),
                   jax.ShapeDtypeStruct((bsz, seq, wa), F32),
                   jax.ShapeDtypeStruct((bsz, nh * rank, seq), BF16),
                   jax.ShapeDtypeStruct((bsz, seq, rank + LANES), BF16),
                   jax.ShapeDtypeStruct((bsz, rank, seq), BF16),
                   jax.ShapeDtypeStruct((bsz, IDX_HEADS * IDX_DIM, seq), BF16),
                   jax.ShapeDtypeStruct((bsz, seq, IDX_DIM), BF16),
                   jax.ShapeDtypeStruct((bsz, IDX_HEADS, seq), F32)],
        compiler_params=_cparams(("parallel", "parallel")),
        name="in_proj",
    )(x, mod, g, w_in_p, w_uk, kvg)


def _band_tables(tq, kw, w_res):
    r = np.arange(tq)[:, None]
    c = np.arange(kw)[None, :]
    clean = lambda d: np.where((d >= 0) & (d <= w_res), d, -1).astype(np.float32)
    return jnp.asarray(np.stack([clean(r - c), clean(r - c + w_res)]))


def _dil_kernel(slope_ref, q_ref, k_ref, v_ref, *refs, w_res, plan):
    n_tab = 1 + max(t for _, _, t in plan)
    dtab_refs = refs[:n_tab]
    o_ref, m_ref, l_ref, acc_ref, bias_ref, s_ref, mloc_ref = refs[n_tab:]
    seq = q_ref.shape[1]
    hp = pl.program_id(1)
    lane = lax.broadcasted_iota(jnp.int32, (1, LANES), 1)
    first = lane < HEAD_DIM
    keep = (first, jnp.logical_not(first))
    m_ref[...] = jnp.full(m_ref.shape, NEG, F32)
    l_ref[...] = jnp.zeros(l_ref.shape, F32)
    acc_ref[...] = jnp.zeros(acc_ref.shape, F32)

    for dil, tq, tab in plan:
        dtab = dtab_refs[tab]
        kw = dtab.shape[-1]
        nt = seq // dil // tq
        for hh in range(2):
            neg_slope = -slope_ref[hp * 2 + hh] * (float(dil) * LOG2E)
            for off in range(2):
                bias_ref[off, hh, :tq, :kw] = jnp.where(dtab[off] >= 0.0, neg_slope * dtab[off], NEG)

        def rows(start, size, dil=dil):
            return pl.ds(start, size) if dil == 1 else pl.ds(start, size, stride=dil)

        def coords(w, dil=dil, tq=tq, nt=nt):
            r = w // nt
            jt = w - r * nt
            q0 = jt * (tq * dil) + r
            k0 = jnp.maximum(jt * tq - w_res, 0) * dil + r
            return q0, k0, jnp.minimum(jt, 1)

        def logits_stage(w, rows=rows, coords=coords, tq=tq, kw=kw):
            q0, k0, off = coords(w)
            q2 = q_ref[0, rows(q0, tq), :]
            k2 = k_ref[0, rows(k0, kw), :].astype(BF16)
            for hh in range(2):
                qh = jnp.where(keep[hh], q2, 0.0).astype(BF16)
                s = lax.dot_general(qh, k2, _NT, preferred_element_type=F32) + bias_ref[off, hh, :tq, :kw]
                s_ref[hh, :tq, :kw] = s
                mloc_ref[hh, :tq, :] = jnp.broadcast_to(jnp.max(s, axis=-1, keepdims=True), (tq, LANES))

        def value_stage(w, rows=rows, coords=coords, tq=tq, kw=kw):
            q0, k0, _ = coords(w)
            v2 = v_ref[0, rows(k0, kw), :].astype(BF16)
            ms = [mloc_ref[hh, :tq, :] for hh in range(2)]
            ps = [jnp.exp2(s_ref[hh, :tq, :kw] - jnp.concatenate([ms[hh]] * (kw // LANES), axis=-1))
                  for hh in range(2)]
            ls = [jnp.sum(p, axis=-1, keepdims=True) for p in ps]
            outs = [jnp.dot(p.astype(BF16), v2, preferred_element_type=F32) for p in ps]
            pick = lambda pair: jnp.where(first, pair[0], pair[1])
            m_t, l_t, o_t = pick(ms), pick(ls), pick(outs)
            q_rows = rows(q0, tq)
            m_old = m_ref[q_rows, :]
            m_new = jnp.maximum(m_old, m_t)
            a_old = jnp.exp2(m_old - m_new)
            a_new = jnp.exp2(m_t - m_new)
            acc_ref[q_rows, :] = a_old * acc_ref[q_rows, :] + a_new * o_t
            l_ref[q_rows, :] = a_old * l_ref[q_rows, :] + a_new * l_t
            m_ref[q_rows, :] = m_new

        def pipe_body(w, carry, logits_stage=logits_stage, value_stage=value_stage):
            value_stage(w - 1)
            logits_stage(w)
            return carry

        n_tiles = dil * nt
        logits_stage(0)
        lax.fori_loop(1, n_tiles, pipe_body, 0)
        value_stage(n_tiles - 1)

    o_ref[0] = (acc_ref[...] / l_ref[...]).astype(BF16)


def _dil_call(slopes, qa, ka, va):
    bsz, seq, wa = qa.shape
    w_res = DILATED_CONFIGS[0][0] // DILATED_CONFIGS[0][1]
    assert all(w // d == w_res for w, d in DILATED_CONFIGS)
    tile_shapes, plan = [], []
    for _, dil in DILATED_CONFIGS:
        n = seq // dil
        tq = min(256, n)
        kw = tq if n == tq else tq + w_res
        assert n % tq == 0 and tq >= w_res and kw <= n
        if (tq, kw) not in tile_shapes:
            tile_shapes.append((tq, kw))
        plan.append((dil, tq, tile_shapes.index((tq, kw))))
    tables = [_band_tables(tq, kw, w_res) for tq, kw in tile_shapes]
    tq_max = max(tq for tq, _ in tile_shapes)
    kw_max = max(kw for _, kw in tile_shapes)
    blk = pl.BlockSpec((1, seq, LANES), lambda b, h: (b, 0, h))
    return pl.pallas_call(
        functools.partial(_dil_kernel, w_res=w_res, plan=tuple(plan)),
        grid=(bsz, wa // LANES),
        in_specs=[pl.BlockSpec(memory_space=pltpu.SMEM), blk, blk, blk] + [_const_spec(t.shape) for t in tables],
        out_specs=blk,
        out_shape=jax.ShapeDtypeStruct((bsz, seq, wa), BF16),
        scratch_shapes=[pltpu.VMEM((seq, LANES), F32), pltpu.VMEM((seq, LANES), F32),
                        pltpu.VMEM((seq, LANES), F32),
                        pltpu.VMEM((2, 2, tq_max, kw_max), F32),
                        pltpu.VMEM((2, tq_max, kw_max), F32),
                        pltpu.VMEM((2, tq_max, LANES), F32)],
        compiler_params=_cparams(("parallel", "parallel")),
        name="dilated_attn",
    )(jnp.asarray(slopes, F32), qa, ka, va, *tables)


def _key_to_f32(key):
    bits = key ^ (lax.shift_right_arithmetic(key, 31) & jnp.int32(0x7FFFFFFF))
    return lax.bitcast_convert_type(bits, F32)


def _dsa_kernel(qit_ref, wit_ref, ki_ref, qlt_ref, qpos_ref, ckv_ref, ckvt_ref, wuv_ref, tri_ref, o_ref,
                sc_ref, acc_ref, s_ref, *, topk, chunk):
    i = pl.program_id(1)
    t0 = i * BLOCK
    rank = ckvt_ref.shape[1]
    nh = qlt_ref.shape[1] // rank
    n_ch = (t0 + BLOCK + chunk - 1) // chunk
    kk = lax.broadcasted_iota(jnp.int32, (chunk, BLOCK), 0)
    qq = lax.broadcasted_iota(jnp.int32, (chunk, BLOCK), 1)
    rel = qq - kk
    k_start = lambda j: pl.multiple_of(j * chunk, chunk)
    heads = lambda ref, width: jnp.concatenate(
        [ref[0, hh * width:(hh + 1) * width, :] for hh in range(ref.shape[1] // width)], axis=-1)

    qit = heads(qit_ref, IDX_DIM)
    wt = wit_ref[0]

    idx_pair = 2 * BLOCK
    idx_cols = [slice(pp * idx_pair, (pp + 1) * idx_pair) for pp in range(IDX_HEADS * BLOCK // idx_pair)]

    def idx_logits(j, cols):
        s_ref[:, cols] = jnp.dot(ki_ref[0, pl.ds(k_start(j), chunk), :], qit[:, cols], preferred_element_type=F32)

    def idx_weigh(pp, cols):
        lg = s_ref[:, cols]
        return (wt[2 * pp:2 * pp + 1, :] * jnp.maximum(lg[:, :BLOCK], 0.0)
                + wt[2 * pp + 1:2 * pp + 2, :] * jnp.maximum(lg[:, BLOCK:], 0.0))

    def idx_store(j, score):
        k0 = k_start(j)
        score = jnp.where(score == 0.0, 0.0, score)
        sc_ref[pl.ds(k0, chunk), :] = jnp.where(rel + (t0 - k0) >= 0, score, -jnp.inf)

    for cols in idx_cols:
        idx_logits(0, cols)

    def score_body(j, carry):
        score = jnp.zeros((chunk, BLOCK), F32)
        for pp, cols in enumerate(idx_cols):
            score = score + idx_weigh(pp, cols)
            idx_logits(j, cols)
        idx_store(j - 1, score)
        return carry

    lax.fori_loop(1, n_ch, score_body, 0)
    idx_store(n_ch - 1, sum((idx_weigh(pp, cols) for pp, cols in enumerate(idx_cols)),
                            jnp.zeros((chunk, BLOCK), F32)))

    def count(pred):
        def body(j, cnt):
            hit = jnp.where(pred(sc_ref[pl.ds(k_start(j), chunk), :]), 1.0, 0.0)
            return cnt + jnp.sum(hit.reshape(-1, COUNT_CHAINS, SUBLANES, BLOCK), axis=0)
        cnt = lax.fori_loop(0, n_ch, body, jnp.zeros((COUNT_CHAINS, SUBLANES, BLOCK), F32))
        return jnp.sum(jnp.sum(cnt, axis=0), axis=0, keepdims=True)

    kf = float(topk)
    n_all = (n_ch * chunk).astype(F32)
    n_pos = count(lambda s: s >= 0.0)
    pos_ok = n_pos >= kf
    key = jnp.where(pos_ok, jnp.int32(0), jnp.int32(-2 ** 31))
    n_key = jnp.where(pos_ok, n_pos, n_all)

    def bit_body(b, carry):
        key, n_key = carry
        trial = key | lax.shift_left(jnp.int32(1), 30 - b)
        cand = _key_to_f32(trial)
        n_trial = count(lambda s: s >= cand)
        ok = n_trial >= kf
        return jnp.where(ok, trial, key), jnp.where(ok, n_trial, n_key)

    tq = t0 + lax.broadcasted_iota(jnp.int32, (1, BLOCK), 1)
    few = tq < topk - 1
    key, n_key = lax.fori_loop(0, 31, bit_body, (key, n_key))
    tau = jnp.where(few, F32_MIN, _key_to_f32(key))

    @pl.when(jnp.max(jnp.where(few, 0.0, n_key - kf)) > 0.0)
    def _():
        n_tie = kf - count(lambda s: s > tau)

        def tie_body(j, seen):
            k0 = k_start(j)
            s = sc_ref[pl.ds(k0, chunk), :]
            eq = jnp.where(s == tau, 1.0, 0.0)
            rank_eq = seen + jnp.dot(tri_ref[...], eq.astype(BF16), preferred_element_type=F32)
            drop = jnp.where(rank_eq > n_tie, eq, 0.0)
            sc_ref[pl.ds(k0, chunk), :] = jnp.where(drop > 0.0, -jnp.inf, s)
            return seen + jnp.sum(eq, axis=0, keepdims=True)

        lax.fori_loop(0, n_ch, tie_body, jnp.zeros((1, BLOCK), F32))

    pad_rows = ckv_ref.shape[-1] - rank - POS_ROWS
    q_aug = jnp.concatenate([heads(qlt_ref, rank), qpos_ref[0],
                             jnp.zeros((pad_rows, nh * BLOCK), BF16)], axis=0)
    acc_ref[...] = jnp.zeros(acc_ref.shape, F32)
    pair = 2 * BLOCK
    n_pairs = nh * BLOCK // pair
    col_slices = [slice(pp * pair, (pp + 1) * pair) for pp in range(n_pairs)]

    def pair_mask(j):
        mask = jnp.where(sc_ref[pl.ds(k_start(j), chunk), :] >= tau, 0.0, NEG)
        return jnp.concatenate([mask, mask], axis=-1)

    def logits_step(j, cols, mask2):
        s = jnp.dot(ckv_ref[0, pl.ds(k_start(j), chunk), :], q_aug[:, cols], preferred_element_type=F32)
        s = s + mask2
        s_ref[:, cols] = s
        return jnp.max(s, axis=0, keepdims=True)

    def value_step(j, cols, m_use, alpha):
        p = jnp.exp2(s_ref[:, cols] - m_use)
        pv = jnp.dot(ckvt_ref[0, :, pl.ds(k_start(j), chunk)], p.astype(BF16), preferred_element_type=F32)
        acc_ref[:, cols] = alpha * acc_ref[:, cols] + pv
        return jnp.sum(p, axis=0, keepdims=True)

    cat = lambda parts: jnp.concatenate(parts, axis=-1)
    mask_first = pair_mask(0)
    m_first = cat([logits_step(0, cols, mask_first) for cols in col_slices])

    def att_body(j, carry):
        m_prev, alpha_prev, l_prev = carry
        mask2 = pair_mask(j)
        sums, cmax = [], []
        for cols in col_slices:
            sums.append(value_step(j - 1, cols, m_prev[:, cols], alpha_prev[:, cols]))
            cmax.append(logits_step(j, cols, mask2))
        m_new = jnp.maximum(m_prev, cat(cmax))
        return m_new, jnp.exp2(m_prev - m_new), alpha_prev * l_prev + cat(sums)

    zeros_row = jnp.zeros((1, nh * BLOCK), F32)
    m_last, alpha_last, l_part = lax.fori_loop(1, n_ch, att_body, (m_first, zeros_row, zeros_row))
    l_fin = alpha_last * l_part + cat(
        [value_step(n_ch - 1, cols, m_last[:, cols], alpha_last[:, cols]) for cols in col_slices])

    o_lat_t = acc_ref[...] / l_fin
    outs = [jnp.dot(o_lat_t[:, hh * BLOCK:(hh + 1) * BLOCK].T.astype(BF16), wuv_ref[hh],
                    preferred_element_type=F32) for hh in range(nh)]
    o_ref[0] = jnp.concatenate(outs, axis=-1).astype(BF16)


def _alibi_query_table(slopes, n_blocks):
    sl = np.asarray(slopes, np.float64) * LOG2E
    t0 = np.arange(n_blocks, dtype=np.float64)[:, None] * BLOCK
    coef = [np.broadcast_to(sl * POS_RADIX, (n_blocks, sl.size)),
            np.broadcast_to(sl, (n_blocks, sl.size)),
            -sl[None, :] * t0]
    rows = []
    for cf in coef:
        hi = jnp.asarray(cf, F32).astype(BF16)
        lo = (jnp.asarray(cf, F32) - hi.astype(F32)).astype(BF16)
        rows += [hi, lo]
    tab = jnp.stack(rows + [jnp.zeros_like(rows[0])] * (POS_ROWS - len(rows)), axis=1)
    return jnp.repeat(tab, BLOCK, axis=-1)


def _dsa_call(slopes, qit, wit, ki, qlt, ckv, ckvt, w_uv, *, chunk=256):
    bsz, seq, ckv_w = ckv.shape
    rank = ckvt.shape[1]
    nh, _, hd = w_uv.shape
    topk = min(TOPK_MAX, seq // 4)
    assert seq % chunk == 0 and seq // POS_RADIX <= 256
    tri = jnp.asarray(np.tril(np.ones((chunk, chunk), np.float32)), BF16)
    qpos = _alibi_query_table(slopes, seq // BLOCK)
    return pl.pallas_call(
        functools.partial(_dsa_kernel, topk=topk, chunk=chunk),
        grid=(bsz, seq // BLOCK),
        in_specs=[pl.BlockSpec((1, IDX_HEADS * IDX_DIM, BLOCK), lambda b, i: (b, 0, i)),
                  pl.BlockSpec((1, IDX_HEADS, BLOCK), lambda b, i: (b, 0, i)),
                  pl.BlockSpec((1, seq, IDX_DIM), lambda b, i: (b, 0, 0)),
                  pl.BlockSpec((1, nh * rank, BLOCK), lambda b, i: (b, 0, i)),
                  pl.BlockSpec((1, POS_ROWS, nh * BLOCK), lambda b, i: (i, 0, 0)),
                  pl.BlockSpec((1, seq, ckv_w), lambda b, i: (b, 0, 0)),
                  pl.BlockSpec((1, rank, seq), lambda b, i: (b, 0, 0)),
                  _const_spec((nh, rank, hd)),
                  _const_spec((chunk, chunk))],
        out_specs=pl.BlockSpec((1, BLOCK, nh * hd), lambda b, i: (b, i, 0)),
        out_shape=jax.ShapeDtypeStruct((bsz, seq, nh * hd), BF16),
        scratch_shapes=[pltpu.VMEM((seq, BLOCK), F32),
                        pltpu.VMEM((rank, nh * BLOCK), F32),
                        pltpu.VMEM((chunk, nh * BLOCK), F32)],
        compiler_params=_cparams(("parallel", "arbitrary")),
        name="sparse_attn",
    )(qit, wit, ki, qlt, qpos, ckv, ckvt, w_uv, tri)


def _out_kernel(x_ref, oa_ref, ob_ref, mod_ref, wo_ref, g2_ref, wgu_ref, wd_ref, gf_ref, o_ref,
                *, d_ff, ff_chunks, final_norm):
    mixed = jnp.concatenate([oa_ref[0], ob_ref[0]], axis=-1)
    x1 = x_ref[0] + mod_ref[0, 2:3, :] * jnp.dot(mixed, wo_ref[...], preferred_element_type=F32)
    h2 = _rms(x1) * g2_ref[...]
    h2 = (h2 * (1.0 + mod_ref[0, 4:5, :]) + mod_ref[0, 3:4, :]).astype(BF16)
    fc = d_ff // ff_chunks
    ffn = jnp.zeros(x1.shape, F32)
    for cc in range(ff_chunks):
        gate = jnp.dot(h2, wgu_ref[:, cc * fc:(cc + 1) * fc], preferred_element_type=F32)
        up = jnp.dot(h2, wgu_ref[:, d_ff + cc * fc:d_ff + (cc + 1) * fc], preferred_element_type=F32)
        act = gate * (1.0 / (1.0 + jnp.exp(-gate))) * up
        ffn = ffn + jnp.dot(act.astype(BF16), wd_ref[cc * fc:(cc + 1) * fc, :], preferred_element_type=F32)
    x2 = x1 + mod_ref[0, 5:6, :] * ffn
    if final_norm:
        x2 = _rms(x2) * gf_ref[...]
    o_ref[0] = x2


def _out_call(x, oa, ob, mod, w_out, g_ffn, w_gu, w_down, g_final, *, tm, final_norm):
    bsz, seq, d = x.shape
    d_ff = w_down.shape[0]
    ff_chunks = 2 if d_ff % (2 * LANES) == 0 else 1
    row = lambda w: pl.BlockSpec((1, tm, w), lambda b, i: (b, i, 0))
    return pl.pallas_call(
        functools.partial(_out_kernel, d_ff=d_ff, ff_chunks=ff_chunks, final_norm=final_norm),
        grid=(bsz, seq // tm),
        in_specs=[row(d), row(oa.shape[-1]), row(ob.shape[-1]),
                  pl.BlockSpec((1, 6, d), lambda b, i: (b, 0, 0)),
                  _const_spec((d, d)), _const_spec((1, d)),
                  _const_spec((d, 2 * d_ff)), _const_spec((d_ff, d)), _const_spec((1, d))],
        out_specs=row(d),
        out_shape=jax.ShapeDtypeStruct((bsz, seq, d), F32),
        compiler_params=_cparams(("parallel", "parallel")),
        name="out_proj_ffn",
    )(x, oa, ob, mod, w_out, g_ffn, w_gu, w_down, g_final)


def _alibi_slopes(n_heads):
    s = 2.0 ** (-8.0 * (np.arange(n_heads, dtype=np.float32) + 1.0) / n_heads)
    return s[0::2], s[1::2]


def kernel(x, c, w_ada, b_ada, g_attn, w_in, kv_norm_g, w_uk, w_uv, w_out, g_ffn, w_gu, w_down, g_final):
    bsz, seq, d = x.shape
    depth = w_ada.shape[0]
    nh = w_uk.shape[1]
    slopes_a, slopes_b = _alibi_slopes(2 * nh)
    d_in = w_in.shape[-1]
    n_pad = -(-d_in // LANES) * LANES
    tm = 512 if seq % 512 == 0 else BLOCK
    for l in range(depth):
        mod = _mod_call(c, w_ada[l], b_ada[l]).reshape(bsz, 6, d)
        w_in_p = jnp.pad(w_in[l], ((0, 0), (0, n_pad - d_in))).astype(BF16)
        qa, ka, va, qlt, ckv, ckvt, qit, ki, wit = _in_call(
            x, mod, g_attn[l].reshape(1, d), w_in_p, w_uk[l].astype(BF16),
            kv_norm_g[l].reshape(1, -1), nh=nh, tm=tm)
        out_a = _dil_call(slopes_a, qa, ka, va)
        out_b = _dsa_call(slopes_b, qit, wit, ki, qlt, ckv, ckvt, w_uv[l].astype(BF16))
        x = _out_call(x, out_a, out_b, mod, w_out[l].astype(BF16), g_ffn[l].reshape(1, d),
                      w_gu[l].astype(BF16), w_down[l].astype(BF16), g_final.reshape(1, d),
                      tm=tm, final_norm=(l == depth - 1))
    return x
```

```python
import functools

import numpy as np
import jax
import jax.numpy as jnp
from jax import lax
from jax.experimental import pallas as pl
from jax.experimental.pallas import tpu as pltpu

F32 = jnp.float32
BF16 = jnp.bfloat16

HEAD_DIM = 64
BLOCK = 128
DILATED_CONFIGS = ((128, 1), (512, 4), (2048, 16))
IDX_HEADS = 8
IDX_DIM = 64
TOPK_MAX = 256
EPS = 1e-6
NEG = -1e30
F32_MIN = float(np.finfo(np.float32).min)
LOG2E = float(np.log2(np.e))
POS_RADIX = 64
POS_ROWS = 16
LANES = 128
SUBLANES = 8
COUNT_CHAINS = 8
VMEM_LIMIT = 56 * 1024 * 1024

_NT = (((1,), (1,)), ((), ()))


def _cparams(sem, flags=None):
    return pltpu.CompilerParams(dimension_semantics=sem, vmem_limit_bytes=VMEM_LIMIT, flags=flags)


def _const_spec(shape):
    nd = len(shape)
    return pl.BlockSpec(shape, lambda *_: (0,) * nd, pipeline_mode=pl.Buffered(1))


def _mod_kernel(c_ref, w_ref, b_ref, o_ref):
    c = c_ref[...]
    c_act = c * (1.0 / (1.0 + jnp.exp(-c)))
    o_ref[...] = jnp.dot(c_act, w_ref[...], preferred_element_type=F32,
                         precision=lax.Precision.HIGHEST) + b_ref[...]


def _mod_call(c, w_ada, b_ada):
    bsz, d = c.shape
    n = w_ada.shape[1]
    tn = 1024
    return pl.pallas_call(
        _mod_kernel,
        grid=(n // tn,),
        in_specs=[pl.BlockSpec((bsz, d), lambda j: (0, 0)),
                  pl.BlockSpec((d, tn), lambda j: (0, j)),
                  pl.BlockSpec((1, tn), lambda j: (0, j))],
        out_specs=pl.BlockSpec((bsz, tn), lambda j: (0, j)),
        out_shape=jax.ShapeDtypeStruct((bsz, n), F32),
        compiler_params=_cparams(("arbitrary",)),
        name="adaln_mod",
    )(c, w_ada, b_ada.reshape(1, n))


def _rms(x):
    return x * lax.rsqrt(jnp.mean(x * x, axis=-1, keepdims=True) + EPS)


def _in_kernel(x_ref, mod_ref, g_ref, w_ref, wuk_ref, kvg_ref,
               qa_ref, ka_ref, va_ref, qlt_ref, ckv_ref, ckvt_ref, qit_ref, ki_ref, wit_ref,
               *, nh, idx_scale):
    hd = HEAD_DIM
    h = _rms(x_ref[0]) * g_ref[...]
    h = h * (1.0 + mod_ref[0, 1:2, :]) + mod_ref[0, 0:1, :]
    proj = jnp.dot(h.astype(BF16), w_ref[...], preferred_element_type=F32)
    wa = nh * hd
    rank = ckvt_ref.shape[1]
    tm = x_ref.shape[1]
    o_qb, o_ckv = 3 * wa, 4 * wa
    o_qi = o_ckv + rank
    o_ki = o_qi + IDX_HEADS * IDX_DIM
    qa_ref[0] = proj[:, 0:wa] * (hd ** -0.5 * LOG2E)
    ka_ref[0] = proj[:, wa:2 * wa]
    va_ref[0] = proj[:, 2 * wa:3 * wa]
    for hh in range(nh):
        qb_h = proj[:, o_qb + hh * hd:o_qb + (hh + 1) * hd].astype(BF16)
        q_lat = jnp.dot(qb_h, wuk_ref[hh], preferred_element_type=F32) * (hd ** -0.5 * LOG2E)
        qlt_ref[0, hh * rank:(hh + 1) * rank, :] = q_lat.T.astype(BF16)
    ckv = (_rms(proj[:, o_ckv:o_ckv + rank]) * kvg_ref[...])
    ckvt_ref[0] = ckv.T.astype(BF16)
    t = pl.program_id(1) * tm + lax.broadcasted_iota(jnp.int32, (tm, LANES), 0)
    lane = lax.broadcasted_iota(jnp.int32, (tm, LANES), 1)
    pos = jnp.where(lane < 2, t // POS_RADIX, jnp.where(lane < 4, t % POS_RADIX, jnp.where(lane < 6, 1, 0)))
    ckv_ref[0] = jnp.concatenate([ckv, pos.astype(F32)], axis=-1).astype(BF16)
    for pp in range(IDX_HEADS * IDX_DIM // LANES):
        qit_ref[0, pp * LANES:(pp + 1) * LANES, :] = proj[:, o_qi + pp * LANES:o_qi + (pp + 1) * LANES].T.astype(BF16)
    kiw = proj[:, o_ki:o_ki + LANES]
    ki_ref[0] = kiw[:, :IDX_DIM].astype(BF16)
    wit_ref[0] = kiw.T[IDX_DIM:IDX_DIM + IDX_HEADS, :] * idx_scale


def _in_call(x, mod, g, w_in_p, w_uk, kvg, *, nh, tm):
    bsz, seq, d = x.shape
    n_pad = w_in_p.shape[1]
    rank = w_uk.shape[-1]
    hd = HEAD_DIM
    wa = nh * hd
    idx_scale = float((IDX_HEADS * IDX_DIM) ** -0.5)
    row_spec = lambda w: pl.BlockSpec((1, tm, w), lambda b, i: (b, i, 0))
    col_spec = lambda r: pl.BlockSpec((1, r, tm), lambda b, i: (b, 0, i))
    return pl.pallas_call(
        functools.partial(_in_kernel, nh=nh, idx_scale=idx_scale),
        grid=(bsz, seq // tm),
        in_specs=[pl.BlockSpec((1, tm, d), lambda b, i: (b, i, 0)),
                  pl.BlockSpec((1, 6, d), lambda b, i: (b, 0, 0)),
                  _const_spec((1, d)),
                  _const_spec((d, n_pad)),
                  _const_spec((nh, hd, rank)),
                  _const_spec((1, rank))],
        out_specs=[row_spec(wa), row_spec(wa), row_spec(wa),
                   col_spec(nh * rank), row_spec(rank + LANES), col_spec(rank),
                   col_spec(IDX_HEADS * IDX_DIM), row_spec(IDX_DIM), col_spec(IDX_HEADS)],
        out_shape=[jax.ShapeDtypeStruct((bsz, seq, wa), F32),
                   jax.ShapeDtypeStruct((bsz, seq, wa), F32),
                   jax.ShapeDtypeStruct((bsz, seq, wa), F32),
                   jax.ShapeDtypeStruct((bsz, nh * rank, seq), BF16),
                   jax.ShapeDtypeStruct((bsz, seq, rank + LANES), BF16),
                   jax.ShapeDtypeStruct((bsz, rank, seq), BF16),
                   jax.ShapeDtypeStruct((bsz, IDX_HEADS * IDX_DIM, seq), BF16),
                   jax.ShapeDtypeStruct((bsz, seq, IDX_DIM), BF16),
                   jax.ShapeDtypeStruct((bsz, IDX_HEADS, seq), F32)],
        compiler_params=_cparams(("parallel", "parallel")),
        name="in_proj",
    )(x, mod, g, w_in_p, w_uk, kvg)


def _band_tables(tq, kw, w_res):
    r = np.arange(tq)[:, None]
    c = np.arange(kw)[None, :]
    clean = lambda d: np.where((d >= 0) & (d <= w_res), d, -1).astype(np.float32)
    return jnp.asarray(np.stack([clean(r - c), clean(r - c + w_res)]))


def _dil_kernel(slope_ref, q_ref, k_ref, v_ref, *refs, w_res, plan):
    n_tab = 1 + max(t for _, _, t in plan)
    dtab_refs = refs[:n_tab]
    o_ref, m_ref, l_ref, acc_ref, bias_ref, s_ref, mloc_ref = refs[n_tab:]
    seq = q_ref.shape[1]
    hp = pl.program_id(1)
    lane = lax.broadcasted_iota(jnp.int32, (1, LANES), 1)
    first = lane < HEAD_DIM
    keep = (first, jnp.logical_not(first))
    m_ref[...] = jnp.full(m_ref.shape, NEG, F32)
    l_ref[...] = jnp.zeros(l_ref.shape, F32)
    acc_ref[...] = jnp.zeros(acc_ref.shape, F32)

    for dil, tq, tab in plan:
        dtab = dtab_refs[tab]
        kw = dtab.shape[-1]
        nt = seq // dil // tq
        for hh in range(2):
            neg_slope = -slope_ref[hp * 2 + hh] * (float(dil) * LOG2E)
            for off in range(2):
                bias_ref[off, hh, :tq, :kw] = jnp.where(dtab[off] >= 0.0, neg_slope * dtab[off], NEG)

        def rows(start, size, dil=dil):
            return pl.ds(start, size) if dil == 1 else pl.ds(start, size, stride=dil)

        def coords(w, dil=dil, tq=tq, nt=nt):
            r = w // nt
            jt = w - r * nt
            q0 = jt * (tq * dil) + r
            k0 = jnp.maximum(jt * tq - w_res, 0) * dil + r
            return q0, k0, jnp.minimum(jt, 1)

        def logits_stage(w, rows=rows, coords=coords, tq=tq, kw=kw):
            q0, k0, off = coords(w)
            q2 = q_ref[0, rows(q0, tq), :]
            k2 = k_ref[0, rows(k0, kw), :].astype(BF16)
            for hh in range(2):
                qh = jnp.where(keep[hh], q2, 0.0).astype(BF16)
                s = lax.dot_general(qh, k2, _NT, preferred_element_type=F32) + bias_ref[off, hh, :tq, :kw]
                s_ref[hh, :tq, :kw] = s
                mloc_ref[hh, :tq, :] = jnp.broadcast_to(jnp.max(s, axis=-1, keepdims=True), (tq, LANES))

        def value_stage(w, rows=rows, coords=coords, tq=tq, kw=kw):
            q0, k0, _ = coords(w)
            v2 = v_ref[0, rows(k0, kw), :].astype(BF16)
            ms = [mloc_ref[hh, :tq, :] for hh in range(2)]
            ps = [jnp.exp2(s_ref[hh, :tq, :kw] - jnp.concatenate([ms[hh]] * (kw // LANES), axis=-1))
                  for hh in range(2)]
            ls = [jnp.sum(p, axis=-1, keepdims=True) for p in ps]
            outs = [jnp.dot(p.astype(BF16), v2, preferred_element_type=F32) for p in ps]
            pick = lambda pair: jnp.where(first, pair[0], pair[1])
            m_t, l_t, o_t = pick(ms), pick(ls), pick(outs)
            q_rows = rows(q0, tq)
            m_old = m_ref[q_rows, :]
            m_new = jnp.maximum(m_old, m_t)
            a_old = jnp.exp2(m_old - m_new)
            a_new = jnp.exp2(m_t - m_new)
            acc_ref[q_rows, :] = a_old * acc_ref[q_rows, :] + a_new * o_t
            l_ref[q_rows, :] = a_old * l_ref[q_rows, :] + a_new * l_t
            m_ref[q_rows, :] = m_new

        def pipe_body(w, carry, logits_stage=logits_stage, value_stage=value_stage):
            value_stage(w - 1)
            logits_stage(w)
            return carry

        n_tiles = dil * nt
        logits_stage(0)
        lax.fori_loop(1, n_tiles, pipe_body, 0)
        value_stage(n_tiles - 1)

    o_ref[0] = (acc_ref[...] / l_ref[...]).astype(BF16)


def _dil_call(slopes, qa, ka, va):
    bsz, seq, wa = qa.shape
    w_res = DILATED_CONFIGS[0][0] // DILATED_CONFIGS[0][1]
    assert all(w // d == w_res for w, d in DILATED_CONFIGS)
    tile_shapes, plan = [], []
    for _, dil in DILATED_CONFIGS:
        n = seq // dil
        tq = min(256, n)
        kw = tq if n == tq else tq + w_res
        assert n % tq == 0 and tq >= w_res and kw <= n
        if (tq, kw) not in tile_shapes:
            tile_shapes.append((tq, kw))
        plan.append((dil, tq, tile_shapes.index((tq, kw))))
    tables = [_band_tables(tq, kw, w_res) for tq, kw in tile_shapes]
    tq_max = max(tq for tq, _ in tile_shapes)
    kw_max = max(kw for _, kw in tile_shapes)
    blk = pl.BlockSpec((1, seq, LANES), lambda b, h: (b, 0, h))
    return pl.pallas_call(
        functools.partial(_dil_kernel, w_res=w_res, plan=tuple(plan)),
        grid=(bsz, wa // LANES),
        in_specs=[pl.BlockSpec(memory_space=pltpu.SMEM), blk, blk, blk] + [_const_spec(t.shape) for t in tables],
        out_specs=blk,
        out_shape=jax.ShapeDtypeStruct((bsz, seq, wa), BF16),
        scratch_shapes=[pltpu.VMEM((seq, LANES), F32), pltpu.VMEM((seq, LANES), F32),
                        pltpu.VMEM((seq, LANES), F32),
                        pltpu.VMEM((2, 2, tq_max, kw_max), F32),
                        pltpu.VMEM((2, tq_max, kw_max), F32),
                        pltpu.VMEM((2, tq_max, LANES), F32)],
        compiler_params=_cparams(("parallel", "parallel")),
        name="dilated_attn",
    )(jnp.asarray(slopes, F32), qa, ka, va, *tables)


def _key_to_f32(key):
    bits = key ^ (lax.shift_right_arithmetic(key, 31) & jnp.int32(0x7FFFFFFF))
    return lax.bitcast_convert_type(bits, F32)


def _dsa_kernel(qit_ref, wit_ref, ki_ref, qlt_ref, qpos_ref, ckv_ref, ckvt_ref, wuv_ref, tri_ref, o_ref,
                sc_ref, sc16_ref, acc_ref, s_ref, *, topk, chunk):
    i = pl.program_id(1)
    t0 = i * BLOCK
    rank = ckvt_ref.shape[1]
    nh = qlt_ref.shape[1] // rank
    n_ch = (t0 + BLOCK + chunk - 1) // chunk
    kk = lax.broadcasted_iota(jnp.int32, (chunk, BLOCK), 0)
    qq = lax.broadcasted_iota(jnp.int32, (chunk, BLOCK), 1)
    rel = qq - kk
    k_start = lambda j: pl.multiple_of(j * chunk, chunk)
    heads = lambda ref, width: jnp.concatenate(
        [ref[0, hh * width:(hh + 1) * width, :] for hh in range(ref.shape[1] // width)], axis=-1)

    qit = heads(qit_ref, IDX_DIM)
    wt = wit_ref[0]

    idx_pair = 2 * BLOCK
    idx_cols = [slice(pp * idx_pair, (pp + 1) * idx_pair) for pp in range(IDX_HEADS * BLOCK // idx_pair)]

    def idx_logits(j, cols):
        s_ref[:, cols] = jnp.dot(ki_ref[0, pl.ds(k_start(j), chunk), :], qit[:, cols], preferred_element_type=F32)

    def idx_weigh(pp, cols):
        lg = s_ref[:, cols]
        return (wt[2 * pp:2 * pp + 1, :] * jnp.maximum(lg[:, :BLOCK], 0.0)
                + wt[2 * pp + 1:2 * pp + 2, :] * jnp.maximum(lg[:, BLOCK:], 0.0))

    def idx_store(j, score):
        k0 = k_start(j)
        score = jnp.where(score == 0.0, 0.0, score)
        score = jnp.where(rel + (t0 - k0) >= 0, score, -jnp.inf)
        sc_ref[pl.ds(k0, chunk), :] = score
        upper = lax.bitcast_convert_type(score, jnp.int32) & jnp.int32(-2 ** 16)
        sc16_ref[pl.ds(k0, chunk), :] = lax.bitcast_convert_type(upper, F32).astype(BF16)

    for cols in idx_cols:
        idx_logits(0, cols)

    def score_body(j, carry):
        score = jnp.zeros((chunk, BLOCK), F32)
        for pp, cols in enumerate(idx_cols):
            score = score + idx_weigh(pp, cols)
            idx_logits(j, cols)
        idx_store(j - 1, score)
        return carry

    lax.fori_loop(1, n_ch, score_body, 0)
    idx_store(n_ch - 1, sum((idx_weigh(pp, cols) for pp, cols in enumerate(idx_cols)),
                            jnp.zeros((chunk, BLOCK), F32)))

    wide = 2 * chunk
    n_wide = (n_ch + 1) // 2
    w_start = lambda j: pl.multiple_of(j * wide, wide)

    @pl.when(n_ch % 2 == 1)
    def _():
        sc_ref[pl.ds(k_start(n_ch), chunk), :] = jnp.full((chunk, BLOCK), -jnp.inf, F32)
        sc16_ref[pl.ds(k_start(n_ch), chunk), :] = jnp.full((chunk, BLOCK), -jnp.inf, BF16)

    def count(pred):
        def body(j, cnt):
            hit = jnp.where(pred(sc_ref[pl.ds(w_start(j), wide), :]), 1.0, 0.0)
            return cnt + jnp.sum(hit.reshape(-1, COUNT_CHAINS, SUBLANES, BLOCK), axis=0)
        cnt = lax.fori_loop(0, n_wide, body, jnp.zeros((COUNT_CHAINS, SUBLANES, BLOCK), F32))
        return jnp.sum(jnp.sum(cnt, axis=0), axis=0, keepdims=True)

    one16, zero16 = jnp.ones((), BF16), jnp.zeros((), BF16)
    packed_rows = 2 * SUBLANES

    def count16(cand):
        cand = jnp.broadcast_to(cand, (wide, BLOCK))

        def body(j, cnt):
            hit = jnp.where(sc16_ref[pl.ds(w_start(j), wide), :] >= cand, one16, zero16)
            hit = hit.reshape(-1, COUNT_CHAINS, packed_rows, BLOCK)
            part = hit[0]
            for g in range(1, hit.shape[0]):
                part = part + hit[g]
            return cnt + part
        cnt = lax.fori_loop(0, n_wide, body, jnp.zeros((COUNT_CHAINS, packed_rows, BLOCK), BF16))
        return jnp.sum(jnp.sum(cnt.astype(F32), axis=0), axis=0, keepdims=True)

    def key16_to_bf16(key16):
        bits = key16 ^ (lax.shift_right_arithmetic(key16, 15) & jnp.int32(0x7FFF))
        return lax.bitcast_convert_type(lax.shift_left(bits, 16), F32).astype(BF16)

    kf = float(topk)
    n_all = (n_wide * wide).astype(F32)
    n_pos = count16(jnp.zeros((1, BLOCK), BF16))
    pos_ok = n_pos >= kf
    key16 = jnp.where(pos_ok, jnp.int32(0), jnp.int32(-2 ** 15))
    n_key = jnp.where(pos_ok, n_pos, n_all)

    def coarse_body(b, carry):
        key16, n_key = carry
        trial = key16 | lax.shift_left(jnp.int32(1), 14 - b)
        n_trial = count16(key16_to_bf16(trial))
        ok = n_trial >= kf
        return jnp.where(ok, trial, key16), jnp.where(ok, n_trial, n_key)

    key16, n_key = lax.fori_loop(0, 15, coarse_body, (key16, n_key))

    key = jnp.where(key16 == 0, jnp.int32(-2 ** 16), lax.shift_left(key16, 16))
    key_top = lax.shift_left(key16, 16) + jnp.int32(2 ** 16 - 1)

    def bit_body(b, carry):
        key, n_key = carry
        trial = key + lax.shift_left(jnp.int32(1), 16 - b)
        cand = _key_to_f32(trial)
        n_trial = count(lambda s: s >= cand)
        ok = jnp.logical_and(n_trial >= kf, trial <= key_top)
        return jnp.where(ok, trial, key), jnp.where(ok, n_trial, n_key)

    tq = t0 + lax.broadcasted_iota(jnp.int32, (1, BLOCK), 1)
    few = tq < topk - 1
    key, n_key = lax.fori_loop(0, 17, bit_body, (key, n_key))
    tau = jnp.where(few, F32_MIN, _key_to_f32(key))

    @pl.when(jnp.max(jnp.where(few, 0.0, n_key - kf)) > 0.0)
    def _():
        n_tie = kf - count(lambda s: s > tau)

        def tie_body(j, seen):
            k0 = k_start(j)
            s = sc_ref[pl.ds(k0, chunk), :]
            eq = jnp.where(s == tau, 1.0, 0.0)
            rank_eq = seen + jnp.dot(tri_ref[...], eq.astype(BF16), preferred_element_type=F32)
            drop = jnp.where(rank_eq > n_tie, eq, 0.0)
            sc_ref[pl.ds(k0, chunk), :] = jnp.where(drop > 0.0, -jnp.inf, s)
            return seen + jnp.sum(eq, axis=0, keepdims=True)

        lax.fori_loop(0, n_ch, tie_body, jnp.zeros((1, BLOCK), F32))

    pad_rows = ckv_ref.shape[-1] - rank - POS_ROWS
    q_aug = jnp.concatenate([heads(qlt_ref, rank), qpos_ref[0],
                             jnp.zeros((pad_rows, nh * BLOCK), BF16)], axis=0)
    acc_ref[...] = jnp.zeros(acc_ref.shape, F32)
    pair = 2 * BLOCK
    n_pairs = nh * BLOCK // pair
    col_slices = [slice(pp * pair, (pp + 1) * pair) for pp in range(n_pairs)]

    def pair_mask(j):
        mask = jnp.where(sc_ref[pl.ds(k_start(j), chunk), :] >= tau, 0.0, NEG)
        return jnp.concatenate([mask, mask], axis=-1)

    def logits_step(j, cols, mask2):
        s = jnp.dot(ckv_ref[0, pl.ds(k_start(j), chunk), :], q_aug[:, cols], preferred_element_type=F32)
        s = s + mask2
        s_ref[:, cols] = s
        return jnp.max(s, axis=0, keepdims=True)

    def value_step(j, cols, m_use, alpha):
        p = jnp.exp2(s_ref[:, cols] - m_use)
        pv = jnp.dot(ckvt_ref[0, :, pl.ds(k_start(j), chunk)], p.astype(BF16), preferred_element_type=F32)
        acc_ref[:, cols] = alpha * acc_ref[:, cols] + pv
        return jnp.sum(p, axis=0, keepdims=True)

    cat = lambda parts: jnp.concatenate(parts, axis=-1)
    mask_first = pair_mask(0)
    m_first = cat([logits_step(0, cols, mask_first) for cols in col_slices])

    def att_body(j, carry):
        m_prev, alpha_prev, l_prev = carry
        mask2 = pair_mask(j)
        sums, cmax = [], []
        for cols in col_slices:
            sums.append(value_step(j - 1, cols, m_prev[:, cols], alpha_prev[:, cols]))
            cmax.append(logits_step(j, cols, mask2))
        m_new = jnp.maximum(m_prev, cat(cmax))
        return m_new, jnp.exp2(m_prev - m_new), alpha_prev * l_prev + cat(sums)

    zeros_row = jnp.zeros((1, nh * BLOCK), F32)
    m_last, alpha_last, l_part = lax.fori_loop(1, n_ch, att_body, (m_first, zeros_row, zeros_row))
    l_fin = alpha_last * l_part + cat(
        [value_step(n_ch - 1, cols, m_last[:, cols], alpha_last[:, cols]) for cols in col_slices])

    o_lat_t = acc_ref[...] / l_fin
    outs = [jnp.dot(o_lat_t[:, hh * BLOCK:(hh + 1) * BLOCK].T.astype(BF16), wuv_ref[hh],
                    preferred_element_type=F32) for hh in range(nh)]
    o_ref[0] = jnp.concatenate(outs, axis=-1).astype(BF16)


def _alibi_query_table(slopes, n_blocks):
    sl = np.asarray(slopes, np.float64) * LOG2E
    t0 = np.arange(n_blocks, dtype=np.float64)[:, None] * BLOCK
    coef = [np.broadcast_to(sl * POS_RADIX, (n_blocks, sl.size)),
            np.broadcast_to(sl, (n_blocks, sl.size)),
            -sl[None, :] * t0]
    rows = []
    for cf in coef:
        hi = jnp.asarray(cf, F32).astype(BF16)
        lo = (jnp.asarray(cf, F32) - hi.astype(F32)).astype(BF16)
        rows += [hi, lo]
    tab = jnp.stack(rows + [jnp.zeros_like(rows[0])] * (POS_ROWS - len(rows)), axis=1)
    return jnp.repeat(tab, BLOCK, axis=-1)


def _dsa_call(slopes, qit, wit, ki, qlt, ckv, ckvt, w_uv, *, chunk=256):
    bsz, seq, ckv_w = ckv.shape
    rank = ckvt.shape[1]
    nh, _, hd = w_uv.shape
    topk = min(TOPK_MAX, seq // 4)
    assert seq % (2 * chunk) == 0 and seq // POS_RADIX <= 256
    assert seq // BLOCK <= 256
    tri = jnp.asarray(np.tril(np.ones((chunk, chunk), np.float32)), BF16)
    qpos = _alibi_query_table(slopes, seq // BLOCK)
    return pl.pallas_call(
        functools.partial(_dsa_kernel, topk=topk, chunk=chunk),
        grid=(bsz, seq // BLOCK),
        in_specs=[pl.BlockSpec((1, IDX_HEADS * IDX_DIM, BLOCK), lambda b, i: (b, 0, i)),
                  pl.BlockSpec((1, IDX_HEADS, BLOCK), lambda b, i: (b, 0, i)),
                  pl.BlockSpec((1, seq, IDX_DIM), lambda b, i: (b, 0, 0)),
                  pl.BlockSpec((1, nh * rank, BLOCK), lambda b, i: (b, 0, i)),
                  pl.BlockSpec((1, POS_ROWS, nh * BLOCK), lambda b, i: (i, 0, 0)),
                  pl.BlockSpec((1, seq, ckv_w), lambda b, i: (b, 0, 0)),
                  pl.BlockSpec((1, rank, seq), lambda b, i: (b, 0, 0)),
                  _const_spec((nh, rank, hd)),
                  _const_spec((chunk, chunk))],
        out_specs=pl.BlockSpec((1, BLOCK, nh * hd), lambda b, i: (b, i, 0)),
        out_shape=jax.ShapeDtypeStruct((bsz, seq, nh * hd), BF16),
        scratch_shapes=[pltpu.VMEM((seq, BLOCK), F32),
                        pltpu.VMEM((seq, BLOCK), BF16),
                        pltpu.VMEM((rank, nh * BLOCK), F32),
                        pltpu.VMEM((chunk, nh * BLOCK), F32)],
        compiler_params=_cparams(("parallel", "arbitrary")),
        name="sparse_attn",
    )(qit, wit, ki, qlt, qpos, ckv, ckvt, w_uv, tri)


def _out_kernel(x_ref, oa_ref, ob_ref, mod_ref, wo_ref, g2_ref, wgu_ref, wd_ref, gf_ref, o_ref,
                *, d_ff, ff_chunks, final_norm):
    mixed = jnp.concatenate([oa_ref[0], ob_ref[0]], axis=-1)
    x1 = x_ref[0] + mod_ref[0, 2:3, :] * jnp.dot(mixed, wo_ref[...], preferred_element_type=F32)
    h2 = _rms(x1) * g2_ref[...]
    h2 = (h2 * (1.0 + mod_ref[0, 4:5, :]) + mod_ref[0, 3:4, :]).astype(BF16)
    fc = d_ff // ff_chunks
    ffn = jnp.zeros(x1.shape, F32)
    for cc in range(ff_chunks):
        gate = jnp.dot(h2, wgu_ref[:, cc * fc:(cc + 1) * fc], preferred_element_type=F32)
        up = jnp.dot(h2, wgu_ref[:, d_ff + cc * fc:d_ff + (cc + 1) * fc], preferred_element_type=F32)
        act = gate * (1.0 / (1.0 + jnp.exp(-gate))) * up
        ffn = ffn + jnp.dot(act.astype(BF16), wd_ref[cc * fc:(cc + 1) * fc, :], preferred_element_type=F32)
    x2 = x1 + mod_ref[0, 5:6, :] * ffn
    if final_norm:
        x2 = _rms(x2) * gf_ref[...]
    o_ref[0] = x2


def _out_call(x, oa, ob, mod, w_out, g_ffn, w_gu, w_down, g_final, *, tm, final_norm):
    bsz, seq, d = x.shape
    d_ff = w_down.shape[0]
    ff_chunks = 2 if d_ff % (2 * LANES) == 0 else 1
    row = lambda w: pl.BlockSpec((1, tm, w), lambda b, i: (b, i, 0))
    return pl.pallas_call(
        functools.partial(_out_kernel, d_ff=d_ff, ff_chunks=ff_chunks, final_norm=final_norm),
        grid=(bsz, seq // tm),
        in_specs=[row(d), row(oa.shape[-1]), row(ob.shape[-1]),
                  pl.BlockSpec((1, 6, d), lambda b, i: (b, 0, 0)),
                  _const_spec((d, d)), _const_spec((1, d)),
                  _const_spec((d, 2 * d_ff)), _const_spec((d_ff, d)), _const_spec((1, d))],
        out_specs=row(d),
        out_shape=jax.ShapeDtypeStruct((bsz, seq, d), F32),
        compiler_params=_cparams(("parallel", "parallel")),
        name="out_proj_ffn",
    )(x, oa, ob, mod, w_out, g_ffn, w_gu, w_down, g_final)


def _alibi_slopes(n_heads):
    s = 2.0 ** (-8.0 * (np.arange(n_heads, dtype=np.float32) + 1.0) / n_heads)
    return s[0::2], s[1::2]


def kernel(x, c, w_ada, b_ada, g_attn, w_in, kv_norm_g, w_uk, w_uv, w_out, g_ffn, w_gu, w_down, g_final):
    bsz, seq, d = x.shape
    depth = w_ada.shape[0]
    nh = w_uk.shape[1]
    slopes_a, slopes_b = _alibi_slopes(2 * nh)
    d_in = w_in.shape[-1]
    n_pad = -(-d_in // LANES) * LANES
    tm = 512 if seq % 512 == 0 else BLOCK
    for l in range(depth):
        mod = _mod_call(c, w_ada[l], b_ada[l]).reshape(bsz, 6, d)
        w_in_p = jnp.pad(w_in[l], ((0, 0), (0, n_pad - d_in))).astype(BF16)
        qa, ka, va, qlt, ckv, ckvt, qit, ki, wit = _in_call(
            x, mod, g_attn[l].reshape(1, d), w_in_p, w_uk[l].astype(BF16),
            kv_norm_g[l].reshape(1, -1), nh=nh, tm=tm)
        out_a = _dil_call(slopes_a, qa, ka, va)
        out_b = _dsa_call(slopes_b, qit, wit, ki, qlt, ckv, ckvt, w_uv[l].astype(BF16))
        x = _out_call(x, out_a, out_b, mod, w_out[l].astype(BF16), g_ffn[l].reshape(1, d),
                      w_gu[l].astype(BF16), w_down[l].astype(BF16), g_final.reshape(1, d),
                      tm=tm, final_norm=(l == depth - 1))
    return x
```

```python
import functools

import numpy as np
import jax
import jax.numpy as jnp
from jax import lax
from jax.experimental import pallas as pl
from jax.experimental.pallas import tpu as pltpu

F32 = jnp.float32
BF16 = jnp.bfloat16

HEAD_DIM = 64
BLOCK = 128
DILATED_CONFIGS = ((128, 1), (512, 4), (2048, 16))
IDX_HEADS = 8
IDX_DIM = 64
TOPK_MAX = 256
EPS = 1e-6
NEG = -1e30
F32_MIN = float(np.finfo(np.float32).min)
LOG2E = float(np.log2(np.e))
POS_RADIX = 64
POS_ROWS = 16
LANES = 128
SUBLANES = 8
COUNT_CHAINS = 8
VMEM_LIMIT = 56 * 1024 * 1024

_NT = (((1,), (1,)), ((), ()))


def _cparams(sem, flags=None):
    return pltpu.CompilerParams(dimension_semantics=sem, vmem_limit_bytes=VMEM_LIMIT, flags=flags)


def _const_spec(shape):
    nd = len(shape)
    return pl.BlockSpec(shape, lambda *_: (0,) * nd, pipeline_mode=pl.Buffered(1))


def _mod_kernel(c_ref, w_ref, b_ref, o_ref):
    c = c_ref[...]
    c_act = c * (1.0 / (1.0 + jnp.exp(-c)))
    o_ref[...] = jnp.dot(c_act, w_ref[...], preferred_element_type=F32,
                         precision=lax.Precision.HIGHEST) + b_ref[...]


def _mod_call(c, w_ada, b_ada):
    bsz, d = c.shape
    n = w_ada.shape[1]
    tn = 1024
    return pl.pallas_call(
        _mod_kernel,
        grid=(n // tn,),
        in_specs=[pl.BlockSpec((bsz, d), lambda j: (0, 0)),
                  pl.BlockSpec((d, tn), lambda j: (0, j)),
                  pl.BlockSpec((1, tn), lambda j: (0, j))],
        out_specs=pl.BlockSpec((bsz, tn), lambda j: (0, j)),
        out_shape=jax.ShapeDtypeStruct((bsz, n), F32),
        compiler_params=_cparams(("arbitrary",)),
        name="adaln_mod",
    )(c, w_ada, b_ada.reshape(1, n))


def _rms(x):
    return x * lax.rsqrt(jnp.mean(x * x, axis=-1, keepdims=True) + EPS)


def _in_kernel(x_ref, mod_ref, g_ref, w_ref, wuk_ref, kvg_ref,
               qa_ref, ka_ref, va_ref, qlt_ref, ckv_ref, ckvt_ref, qit_ref, ki_ref, wit_ref,
               *, nh, idx_scale):
    hd = HEAD_DIM
    h = _rms(x_ref[0]) * g_ref[...]
    h = h * (1.0 + mod_ref[0, 1:2, :]) + mod_ref[0, 0:1, :]
    proj = jnp.dot(h.astype(BF16), w_ref[...], preferred_element_type=F32)
    wa = nh * hd
    rank = ckvt_ref.shape[1]
    tm = x_ref.shape[1]
    o_qb, o_ckv = 3 * wa, 4 * wa
    o_qi = o_ckv + rank
    o_ki = o_qi + IDX_HEADS * IDX_DIM
    qa_ref[0] = proj[:, 0:wa] * (hd ** -0.5 * LOG2E)
    ka_ref[0] = proj[:, wa:2 * wa]
    va_ref[0] = proj[:, 2 * wa:3 * wa]
    for hh in range(nh):
        qb_h = proj[:, o_qb + hh * hd:o_qb + (hh + 1) * hd].astype(BF16)
        q_lat = jnp.dot(qb_h, wuk_ref[hh], preferred_element_type=F32) * (hd ** -0.5 * LOG2E)
        qlt_ref[0, hh * rank:(hh + 1) * rank, :] = q_lat.T.astype(BF16)
    ckv = (_rms(proj[:, o_ckv:o_ckv + rank]) * kvg_ref[...])
    ckvt_ref[0] = ckv.T.astype(BF16)
    t = pl.program_id(1) * tm + lax.broadcasted_iota(jnp.int32, (tm, LANES), 0)
    lane = lax.broadcasted_iota(jnp.int32, (tm, LANES), 1)
    pos = jnp.where(lane < 2, t // POS_RADIX, jnp.where(lane < 4, t % POS_RADIX, jnp.where(lane < 6, 1, 0)))
    ckv_ref[0] = jnp.concatenate([ckv, pos.astype(F32)], axis=-1).astype(BF16)
    for pp in range(IDX_HEADS * IDX_DIM // LANES):
        qit_ref[0, pp * LANES:(pp + 1) * LANES, :] = proj[:, o_qi + pp * LANES:o_qi + (pp + 1) * LANES].T.astype(BF16)
    kiw = proj[:, o_ki:o_ki + LANES]
    ki_ref[0] = kiw[:, :IDX_DIM].astype(BF16)
    wit_ref[0] = kiw.T[IDX_DIM:IDX_DIM + IDX_HEADS, :] * idx_scale


def _in_call(x, mod, g, w_in_p, w_uk, kvg, *, nh, tm):
    bsz, seq, d = x.shape
    n_pad = w_in_p.shape[1]
    rank = w_uk.shape[-1]
    hd = HEAD_DIM
    wa = nh * hd
    idx_scale = float((IDX_HEADS * IDX_DIM) ** -0.5)
    row_spec = lambda w: pl.BlockSpec((1, tm, w), lambda b, i: (b, i, 0))
    col_spec = lambda r: pl.BlockSpec((1, r, tm), lambda b, i: (b, 0, i))
    return pl.pallas_call(
        functools.partial(_in_kernel, nh=nh, idx_scale=idx_scale),
        grid=(bsz, seq // tm),
        in_specs=[pl.BlockSpec((1, tm, d), lambda b, i: (b, i, 0)),
                  pl.BlockSpec((1, 6, d), lambda b, i: (b, 0, 0)),
                  _const_spec((1, d)),
                  _const_spec((d, n_pad)),
                  _const_spec((nh, hd, rank)),
                  _const_spec((1, rank))],
        out_specs=[row_spec(wa), row_spec(wa), row_spec(wa),
                   col_spec(nh * rank), row_spec(rank + LANES), col_spec(rank),
                   col_spec(IDX_HEADS * IDX_DIM), row_spec(IDX_DIM), col_spec(IDX_HEADS)],
        out_shape=[jax.ShapeDtypeStruct((bsz, seq, wa), F32),
                   jax.ShapeDtypeStruct((bsz, seq, wa), F32),
                   jax.ShapeDtypeStruct((bsz, seq, wa), F32),
                   jax.ShapeDtypeStruct((bsz, nh * rank, seq), BF16),
                   jax.ShapeDtypeStruct((bsz, seq, rank + LANES), BF16),
                   jax.ShapeDtypeStruct((bsz, rank, seq), BF16),
                   jax.ShapeDtypeStruct((bsz, IDX_HEADS * IDX_DIM, seq), BF16),
                   jax.ShapeDtypeStruct((bsz, seq, IDX_DIM), BF16),
                   jax.ShapeDtypeStruct((bsz, IDX_HEADS, seq), F32)],
        compiler_params=_cparams(("parallel", "parallel")),
        name="in_proj",
    )(x, mod, g, w_in_p, w_uk, kvg)


def _band_tables(tq, kw, w_res):
    r = np.arange(tq)[:, None]
    c = np.arange(kw)[None, :]
    clean = lambda d: np.where((d >= 0) & (d <= w_res), d, -1).astype(np.float32)
    return jnp.asarray(np.stack([clean(r - c), clean(r - c + w_res)]))


def _dil_kernel(slope_ref, q_ref, k_ref, v_ref, *refs, w_res, plan):
    n_tab = 1 + max(t for _, _, t in plan)
    dtab_refs = refs[:n_tab]
    o_ref, m_ref, l_ref, acc_ref, bias_ref, s_ref, mloc_ref = refs[n_tab:]
    seq = q_ref.shape[1]
    hp = pl.program_id(1)
    lane = lax.broadcasted_iota(jnp.int32, (1, LANES), 1)
    first = lane < HEAD_DIM
    keep = (first, jnp.logical_not(first))
    m_ref[...] = jnp.full(m_ref.shape, NEG, F32)
    l_ref[...] = jnp.zeros(l_ref.shape, F32)
    acc_ref[...] = jnp.zeros(acc_ref.shape, F32)

    for dil, tq, tab in plan:
        dtab = dtab_refs[tab]
        kw = dtab.shape[-1]
        nt = seq // dil // tq
        for hh in range(2):
            neg_slope = -slope_ref[hp * 2 + hh] * (float(dil) * LOG2E)
            for off in range(2):
                bias_ref[off, hh, :tq, :kw] = jnp.where(dtab[off] >= 0.0, neg_slope * dtab[off], NEG)

        def rows(start, size, dil=dil):
            return pl.ds(start, size) if dil == 1 else pl.ds(start, size, stride=dil)

        def coords(w, dil=dil, tq=tq, nt=nt):
            r = w // nt
            jt = w - r * nt
            q0 = jt * (tq * dil) + r
            k0 = jnp.maximum(jt * tq - w_res, 0) * dil + r
            return q0, k0, jnp.minimum(jt, 1)

        def logits_stage(w, rows=rows, coords=coords, tq=tq, kw=kw):
            q0, k0, off = coords(w)
            q2 = q_ref[0, rows(q0, tq), :]
            k2 = k_ref[0, rows(k0, kw), :].astype(BF16)
            for hh in range(2):
                qh = jnp.where(keep[hh], q2, 0.0).astype(BF16)
                s = lax.dot_general(qh, k2, _NT, preferred_element_type=F32) + bias_ref[off, hh, :tq, :kw]
                s_ref[hh, :tq, :kw] = s
                mloc_ref[hh, :tq, :] = jnp.broadcast_to(jnp.max(s, axis=-1, keepdims=True), (tq, LANES))

        def value_stage(w, rows=rows, coords=coords, tq=tq, kw=kw):
            q0, k0, _ = coords(w)
            v2 = v_ref[0, rows(k0, kw), :].astype(BF16)
            ms = [mloc_ref[hh, :tq, :] for hh in range(2)]
            ps = [jnp.exp2(s_ref[hh, :tq, :kw] - jnp.concatenate([ms[hh]] * (kw // LANES), axis=-1))
                  for hh in range(2)]
            ls = [jnp.sum(p, axis=-1, keepdims=True) for p in ps]
            outs = [jnp.dot(p.astype(BF16), v2, preferred_element_type=F32) for p in ps]
            pick = lambda pair: jnp.where(first, pair[0], pair[1])
            m_t, l_t, o_t = pick(ms), pick(ls), pick(outs)
            q_rows = rows(q0, tq)
            m_old = m_ref[q_rows, :]
            m_new = jnp.maximum(m_old, m_t)
            a_old = jnp.exp2(m_old - m_new)
            a_new = jnp.exp2(m_t - m_new)
            acc_ref[q_rows, :] = a_old * acc_ref[q_rows, :] + a_new * o_t
            l_ref[q_rows, :] = a_old * l_ref[q_rows, :] + a_new * l_t
            m_ref[q_rows, :] = m_new

        def pipe_body(w, carry, logits_stage=logits_stage, value_stage=value_stage):
            value_stage(w - 1)
            logits_stage(w)
            return carry

        n_tiles = dil * nt
        logits_stage(0)
        lax.fori_loop(1, n_tiles, pipe_body, 0)
        value_stage(n_tiles - 1)

    o_ref[0] = (acc_ref[...] / l_ref[...]).astype(BF16)


def _dil_call(slopes, qa, ka, va):
    bsz, seq, wa = qa.shape
    w_res = DILATED_CONFIGS[0][0] // DILATED_CONFIGS[0][1]
    assert all(w // d == w_res for w, d in DILATED_CONFIGS)
    tile_shapes, plan = [], []
    for _, dil in DILATED_CONFIGS:
        n = seq // dil
        tq = min(256, n)
        kw = tq if n == tq else tq + w_res
        assert n % tq == 0 and tq >= w_res and kw <= n
        if (tq, kw) not in tile_shapes:
            tile_shapes.append((tq, kw))
        plan.append((dil, tq, tile_shapes.index((tq, kw))))
    tables = [_band_tables(tq, kw, w_res) for tq, kw in tile_shapes]
    tq_max = max(tq for tq, _ in tile_shapes)
    kw_max = max(kw for _, kw in tile_shapes)
    blk = pl.BlockSpec((1, seq, LANES), lambda b, h: (b, 0, h))
    return pl.pallas_call(
        functools.partial(_dil_kernel, w_res=w_res, plan=tuple(plan)),
        grid=(bsz, wa // LANES),
        in_specs=[pl.BlockSpec(memory_space=pltpu.SMEM), blk, blk, blk] + [_const_spec(t.shape) for t in tables],
        out_specs=blk,
        out_shape=jax.ShapeDtypeStruct((bsz, seq, wa), BF16),
        scratch_shapes=[pltpu.VMEM((seq, LANES), F32), pltpu.VMEM((seq, LANES), F32),
                        pltpu.VMEM((seq, LANES), F32),
                        pltpu.VMEM((2, 2, tq_max, kw_max), F32),
                        pltpu.VMEM((2, tq_max, kw_max), F32),
                        pltpu.VMEM((2, tq_max, LANES), F32)],
        compiler_params=_cparams(("parallel", "parallel")),
        name="dilated_attn",
    )(jnp.asarray(slopes, F32), qa, ka, va, *tables)


def _key_to_f32(key):
    bits = key ^ (lax.shift_right_arithmetic(key, 31) & jnp.int32(0x7FFFFFFF))
    return lax.bitcast_convert_type(bits, F32)


def _dsa_kernel(qit_ref, wit_ref, ki_ref, qlt_ref, qpos_ref, ckv_ref, ckvt_ref, wuv_ref, tri_ref, o_ref,
                sc_ref, sc16_ref, acc_ref, s_ref, *, topk, chunk):
    i = pl.program_id(1)
    t0 = i * BLOCK
    rank = ckvt_ref.shape[1]
    nh = qlt_ref.shape[1] // rank
    n_ch = (t0 + BLOCK + chunk - 1) // chunk
    kk = lax.broadcasted_iota(jnp.int32, (chunk, BLOCK), 0)
    qq = lax.broadcasted_iota(jnp.int32, (chunk, BLOCK), 1)
    rel = qq - kk
    k_start = lambda j: pl.multiple_of(j * chunk, chunk)
    heads = lambda ref, width: jnp.concatenate(
        [ref[0, hh * width:(hh + 1) * width, :] for hh in range(ref.shape[1] // width)], axis=-1)

    qit = heads(qit_ref, IDX_DIM)
    wt = wit_ref[0]

    idx_pair = 2 * BLOCK
    idx_cols = [slice(pp * idx_pair, (pp + 1) * idx_pair) for pp in range(IDX_HEADS * BLOCK // idx_pair)]

    def idx_logits(j, cols):
        s_ref[:, cols] = jnp.dot(ki_ref[0, pl.ds(k_start(j), chunk), :], qit[:, cols], preferred_element_type=F32)

    def idx_weigh(pp, cols):
        lg = s_ref[:, cols]
        return (wt[2 * pp:2 * pp + 1, :] * jnp.maximum(lg[:, :BLOCK], 0.0)
                + wt[2 * pp + 1:2 * pp + 2, :] * jnp.maximum(lg[:, BLOCK:], 0.0))

    def idx_store(j, score):
        k0 = k_start(j)
        score = jnp.where(score == 0.0, 0.0, score)
        score = jnp.where(rel + (t0 - k0) >= 0, score, -jnp.inf)
        sc_ref[pl.ds(k0, chunk), :] = score
        upper = lax.bitcast_convert_type(score, jnp.int32) & jnp.int32(-2 ** 16)
        sc16_ref[pl.ds(k0, chunk), :] = lax.bitcast_convert_type(upper, F32).astype(BF16)

    for cols in idx_cols:
        idx_logits(0, cols)

    def score_body(j, carry):
        score = jnp.zeros((chunk, BLOCK), F32)
        for pp, cols in enumerate(idx_cols):
            score = score + idx_weigh(pp, cols)
            idx_logits(j, cols)
        idx_store(j - 1, score)
        return carry

    lax.fori_loop(1, n_ch, score_body, 0)
    idx_store(n_ch - 1, sum((idx_weigh(pp, cols) for pp, cols in enumerate(idx_cols)),
                            jnp.zeros((chunk, BLOCK), F32)))

    wide, n_wide, w_start = chunk, n_ch, k_start

    def count(pred):
        def body(j, cnt):
            hit = jnp.where(pred(sc_ref[pl.ds(w_start(j), wide), :]), 1.0, 0.0)
            return cnt + jnp.sum(hit.reshape(-1, COUNT_CHAINS, SUBLANES, BLOCK), axis=0)
        cnt = lax.fori_loop(0, n_wide, body, jnp.zeros((COUNT_CHAINS, SUBLANES, BLOCK), F32))
        return jnp.sum(jnp.sum(cnt, axis=0), axis=0, keepdims=True)

    one16, zero16 = jnp.ones((), BF16), jnp.zeros((), BF16)
    packed_rows = 2 * SUBLANES

    def count16(cand):
        cand = jnp.broadcast_to(cand, (wide, BLOCK))

        def body(j, cnt):
            hit = jnp.where(sc16_ref[pl.ds(w_start(j), wide), :] >= cand, one16, zero16)
            hit = hit.reshape(-1, COUNT_CHAINS, packed_rows, BLOCK)
            part = hit[0]
            for g in range(1, hit.shape[0]):
                part = part + hit[g]
            return cnt + part
        cnt = lax.fori_loop(0, n_wide, body, jnp.zeros((COUNT_CHAINS, packed_rows, BLOCK), BF16))
        return jnp.sum(jnp.sum(cnt.astype(F32), axis=0), axis=0, keepdims=True)

    def key16_to_bf16(key16):
        bits = key16 ^ (lax.shift_right_arithmetic(key16, 15) & jnp.int32(0x7FFF))
        return lax.bitcast_convert_type(lax.shift_left(bits, 16), F32).astype(BF16)

    kf = float(topk)
    n_all = (n_wide * wide).astype(F32)
    n_pos = count16(jnp.zeros((1, BLOCK), BF16))
    pos_ok = n_pos >= kf
    key16 = jnp.where(pos_ok, jnp.int32(0), jnp.int32(-2 ** 15))
    n_key = jnp.where(pos_ok, n_pos, n_all)

    def coarse_body(b, carry):
        key16, n_key = carry
        trial = key16 | lax.shift_left(jnp.int32(1), 14 - b)
        n_trial = count16(key16_to_bf16(trial))
        ok = n_trial >= kf
        return jnp.where(ok, trial, key16), jnp.where(ok, n_trial, n_key)

    key16, n_key = lax.fori_loop(0, 15, coarse_body, (key16, n_key))

    key = jnp.where(key16 == 0, jnp.int32(-2 ** 16), lax.shift_left(key16, 16))
    key_top = lax.shift_left(key16, 16) + jnp.int32(2 ** 16 - 1)

    def bit_body(b, carry):
        key, n_key = carry
        trial = key + lax.shift_left(jnp.int32(1), 16 - b)
        cand = _key_to_f32(trial)
        n_trial = count(lambda s: s >= cand)
        ok = jnp.logical_and(n_trial >= kf, trial <= key_top)
        return jnp.where(ok, trial, key), jnp.where(ok, n_trial, n_key)

    tq = t0 + lax.broadcasted_iota(jnp.int32, (1, BLOCK), 1)
    few = tq < topk - 1
    key, n_key = lax.fori_loop(0, 17, bit_body, (key, n_key))
    tau = jnp.where(few, F32_MIN, _key_to_f32(key))

    @pl.when(jnp.max(jnp.where(few, 0.0, n_key - kf)) > 0.0)
    def _():
        n_tie = kf - count(lambda s: s > tau)

        def tie_body(j, seen):
            k0 = k_start(j)
            s = sc_ref[pl.ds(k0, chunk), :]
            eq = jnp.where(s == tau, 1.0, 0.0)
            rank_eq = seen + jnp.dot(tri_ref[...], eq.astype(BF16), preferred_element_type=F32)
            drop = jnp.where(rank_eq > n_tie, eq, 0.0)
            sc_ref[pl.ds(k0, chunk), :] = jnp.where(drop > 0.0, -jnp.inf, s)
            return seen + jnp.sum(eq, axis=0, keepdims=True)

        lax.fori_loop(0, n_ch, tie_body, jnp.zeros((1, BLOCK), F32))

    pad_rows = ckv_ref.shape[-1] - rank - POS_ROWS
    q_aug = jnp.concatenate([heads(qlt_ref, rank), qpos_ref[0],
                             jnp.zeros((pad_rows, nh * BLOCK), BF16)], axis=0)
    acc_ref[...] = jnp.zeros(acc_ref.shape, F32)
    pair = 2 * BLOCK
    n_pairs = nh * BLOCK // pair
    col_slices = [slice(pp * pair, (pp + 1) * pair) for pp in range(n_pairs)]

    def pair_mask(j):
        mask = jnp.where(sc_ref[pl.ds(k_start(j), chunk), :] >= tau, 0.0, NEG)
        return jnp.concatenate([mask, mask], axis=-1)

    def logits_step(j, cols, mask2):
        s = jnp.dot(ckv_ref[0, pl.ds(k_start(j), chunk), :], q_aug[:, cols], preferred_element_type=F32)
        s = s + mask2
        s_ref[:, cols] = s
        return jnp.max(s, axis=0, keepdims=True)

    def value_step(j, cols, m_use, alpha):
        p = jnp.exp2(s_ref[:, cols] - m_use)
        pv = jnp.dot(ckvt_ref[0, :, pl.ds(k_start(j), chunk)], p.astype(BF16), preferred_element_type=F32)
        acc_ref[:, cols] = alpha * acc_ref[:, cols] + pv
        return jnp.sum(p, axis=0, keepdims=True)

    cat = lambda parts: jnp.concatenate(parts, axis=-1)
    mask_first = pair_mask(0)
    m_first = cat([logits_step(0, cols, mask_first) for cols in col_slices])

    def att_body(j, carry):
        m_prev, alpha_prev, l_prev = carry
        mask2 = pair_mask(j)
        sums, cmax = [], []
        for cols in col_slices:
            sums.append(value_step(j - 1, cols, m_prev[:, cols], alpha_prev[:, cols]))
            cmax.append(logits_step(j, cols, mask2))
        m_new = jnp.maximum(m_prev, cat(cmax))
        return m_new, jnp.exp2(m_prev - m_new), alpha_prev * l_prev + cat(sums)

    zeros_row = jnp.zeros((1, nh * BLOCK), F32)
    m_last, alpha_last, l_part = lax.fori_loop(1, n_ch, att_body, (m_first, zeros_row, zeros_row))
    l_fin = alpha_last * l_part + cat(
        [value_step(n_ch - 1, cols, m_last[:, cols], alpha_last[:, cols]) for cols in col_slices])

    o_lat_t = acc_ref[...] / l_fin
    outs = [jnp.dot(o_lat_t[:, hh * BLOCK:(hh + 1) * BLOCK].T.astype(BF16), wuv_ref[hh],
                    preferred_element_type=F32) for hh in range(nh)]
    o_ref[0] = jnp.concatenate(outs, axis=-1).astype(BF16)


def _alibi_query_table(slopes, n_blocks):
    sl = np.asarray(slopes, np.float64) * LOG2E
    t0 = np.arange(n_blocks, dtype=np.float64)[:, None] * BLOCK
    coef = [np.broadcast_to(sl * POS_RADIX, (n_blocks, sl.size)),
            np.broadcast_to(sl, (n_blocks, sl.size)),
            -sl[None, :] * t0]
    rows = []
    for cf in coef:
        hi = jnp.asarray(cf, F32).astype(BF16)
        lo = (jnp.asarray(cf, F32) - hi.astype(F32)).astype(BF16)
        rows += [hi, lo]
    tab = jnp.stack(rows + [jnp.zeros_like(rows[0])] * (POS_ROWS - len(rows)), axis=1)
    return jnp.repeat(tab, BLOCK, axis=-1)


def _dsa_call(slopes, qit, wit, ki, qlt, ckv, ckvt, w_uv, *, chunk=256):
    bsz, seq, ckv_w = ckv.shape
    rank = ckvt.shape[1]
    nh, _, hd = w_uv.shape
    topk = min(TOPK_MAX, seq // 4)
    assert seq % chunk == 0 and seq // POS_RADIX <= 256
    assert seq // BLOCK <= 256
    tri = jnp.asarray(np.tril(np.ones((chunk, chunk), np.float32)), BF16)
    qpos = _alibi_query_table(slopes, seq // BLOCK)
    return pl.pallas_call(
        functools.partial(_dsa_kernel, topk=topk, chunk=chunk),
        grid=(bsz, seq // BLOCK),
        in_specs=[pl.BlockSpec((1, IDX_HEADS * IDX_DIM, BLOCK), lambda b, i: (b, 0, i)),
                  pl.BlockSpec((1, IDX_HEADS, BLOCK), lambda b, i: (b, 0, i)),
                  pl.BlockSpec((1, seq, IDX_DIM), lambda b, i: (b, 0, 0)),
                  pl.BlockSpec((1, nh * rank, BLOCK), lambda b, i: (b, 0, i)),
                  pl.BlockSpec((1, POS_ROWS, nh * BLOCK), lambda b, i: (i, 0, 0)),
                  pl.BlockSpec((1, seq, ckv_w), lambda b, i: (b, 0, 0)),
                  pl.BlockSpec((1, rank, seq), lambda b, i: (b, 0, 0)),
                  _const_spec((nh, rank, hd)),
                  _const_spec((chunk, chunk))],
        out_specs=pl.BlockSpec((1, BLOCK, nh * hd), lambda b, i: (b, i, 0)),
        out_shape=jax.ShapeDtypeStruct((bsz, seq, nh * hd), BF16),
        scratch_shapes=[pltpu.VMEM((seq, BLOCK), F32),
                        pltpu.VMEM((seq, BLOCK), BF16),
                        pltpu.VMEM((rank, nh * BLOCK), F32),
                        pltpu.VMEM((chunk, nh * BLOCK), F32)],
        compiler_params=_cparams(("parallel", "arbitrary")),
        name="sparse_attn",
    )(qit, wit, ki, qlt, qpos, ckv, ckvt, w_uv, tri)


def _out_kernel(x_ref, oa_ref, ob_ref, mod_ref, wo_ref, g2_ref, wgu_ref, wd_ref, gf_ref, o_ref,
                *, d_ff, ff_chunks, final_norm):
    mixed = jnp.concatenate([oa_ref[0], ob_ref[0]], axis=-1)
    x1 = x_ref[0] + mod_ref[0, 2:3, :] * jnp.dot(mixed, wo_ref[...], preferred_element_type=F32)
    h2 = _rms(x1) * g2_ref[...]
    h2 = (h2 * (1.0 + mod_ref[0, 4:5, :]) + mod_ref[0, 3:4, :]).astype(BF16)
    fc = d_ff // ff_chunks
    ffn = jnp.zeros(x1.shape, F32)
    for cc in range(ff_chunks):
        gate = jnp.dot(h2, wgu_ref[:, cc * fc:(cc + 1) * fc], preferred_element_type=F32)
        up = jnp.dot(h2, wgu_ref[:, d_ff + cc * fc:d_ff + (cc + 1) * fc], preferred_element_type=F32)
        act = gate * (1.0 / (1.0 + jnp.exp(-gate))) * up
        ffn = ffn + jnp.dot(act.astype(BF16), wd_ref[cc * fc:(cc + 1) * fc, :], preferred_element_type=F32)
    x2 = x1 + mod_ref[0, 5:6, :] * ffn
    if final_norm:
        x2 = _rms(x2) * gf_ref[...]
    o_ref[0] = x2


def _out_call(x, oa, ob, mod, w_out, g_ffn, w_gu, w_down, g_final, *, tm, final_norm):
    bsz, seq, d = x.shape
    d_ff = w_down.shape[0]
    ff_chunks = 2 if d_ff % (2 * LANES) == 0 else 1
    row = lambda w: pl.BlockSpec((1, tm, w), lambda b, i: (b, i, 0))
    return pl.pallas_call(
        functools.partial(_out_kernel, d_ff=d_ff, ff_chunks=ff_chunks, final_norm=final_norm),
        grid=(bsz, seq // tm),
        in_specs=[row(d), row(oa.shape[-1]), row(ob.shape[-1]),
                  pl.BlockSpec((1, 6, d), lambda b, i: (b, 0, 0)),
                  _const_spec((d, d)), _const_spec((1, d)),
                  _const_spec((d, 2 * d_ff)), _const_spec((d_ff, d)), _const_spec((1, d))],
        out_specs=row(d),
        out_shape=jax.ShapeDtypeStruct((bsz, seq, d), F32),
        compiler_params=_cparams(("parallel", "parallel")),
        name="out_proj_ffn",
    )(x, oa, ob, mod, w_out, g_ffn, w_gu, w_down, g_final)


def _alibi_slopes(n_heads):
    s = 2.0 ** (-8.0 * (np.arange(n_heads, dtype=np.float32) + 1.0) / n_heads)
    return s[0::2], s[1::2]


def kernel(x, c, w_ada, b_ada, g_attn, w_in, kv_norm_g, w_uk, w_uv, w_out, g_ffn, w_gu, w_down, g_final):
    bsz, seq, d = x.shape
    depth = w_ada.shape[0]
    nh = w_uk.shape[1]
    slopes_a, slopes_b = _alibi_slopes(2 * nh)
    d_in = w_in.shape[-1]
    n_pad = -(-d_in // LANES) * LANES
    tm = 512 if seq % 512 == 0 else BLOCK
    for l in range(depth):
        mod = _mod_call(c, w_ada[l], b_ada[l]).reshape(bsz, 6, d)
        w_in_p = jnp.pad(w_in[l], ((0, 0), (0, n_pad - d_in))).astype(BF16)
        qa, ka, va, qlt, ckv, ckvt, qit, ki, wit = _in_call(
            x, mod, g_attn[l].reshape(1, d), w_in_p, w_uk[l].astype(BF16),
            kv_norm_g[l].reshape(1, -1), nh=nh, tm=tm)
        out_a = _dil_call(slopes_a, qa, ka, va)
        out_b = _dsa_call(slopes_b, qit, wit, ki, qlt, ckv, ckvt, w_uv[l].astype(BF16))
        x = _out_call(x, out_a, out_b, mod, w_out[l].astype(BF16), g_ffn[l].reshape(1, d),
                      w_gu[l].astype(BF16), w_down[l].astype(BF16), g_final.reshape(1, d),
                      tm=tm, final_norm=(l == depth - 1))
    return x
```

```python
import functools

import numpy as np
import jax
import jax.numpy as jnp
from jax import lax
from jax.experimental import pallas as pl
from jax.experimental.pallas import tpu as pltpu

F32 = jnp.float32
BF16 = jnp.bfloat16

HEAD_DIM = 64
BLOCK = 128
DILATED_CONFIGS = ((128, 1), (512, 4), (2048, 16))
IDX_HEADS = 8
IDX_DIM = 64
TOPK_MAX = 256
EPS = 1e-6
NEG = -1e30
F32_MIN = float(np.finfo(np.float32).min)
LOG2E = float(np.log2(np.e))
POS_RADIX = 64
POS_ROWS = 16
LANES = 128
SUBLANES = 8
COUNT_CHAINS = 8
VMEM_LIMIT = 56 * 1024 * 1024

_NT = (((1,), (1,)), ((), ()))


def _cparams(sem, flags=None):
    return pltpu.CompilerParams(dimension_semantics=sem, vmem_limit_bytes=VMEM_LIMIT, flags=flags)


def _const_spec(shape):
    nd = len(shape)
    return pl.BlockSpec(shape, lambda *_: (0,) * nd, pipeline_mode=pl.Buffered(1))


def _mod_kernel(c_ref, w_ref, b_ref, o_ref):
    c = c_ref[...]
    c_act = c * (1.0 / (1.0 + jnp.exp(-c)))
    o_ref[...] = jnp.dot(c_act, w_ref[...], preferred_element_type=F32,
                         precision=lax.Precision.HIGHEST) + b_ref[...]


def _mod_call(c, w_ada, b_ada):
    bsz, d = c.shape
    n = w_ada.shape[1]
    tn = 1024
    return pl.pallas_call(
        _mod_kernel,
        grid=(n // tn,),
        in_specs=[pl.BlockSpec((bsz, d), lambda j: (0, 0)),
                  pl.BlockSpec((d, tn), lambda j: (0, j)),
                  pl.BlockSpec((1, tn), lambda j: (0, j))],
        out_specs=pl.BlockSpec((bsz, tn), lambda j: (0, j)),
        out_shape=jax.ShapeDtypeStruct((bsz, n), F32),
        compiler_params=_cparams(("arbitrary",)),
        name="adaln_mod",
    )(c, w_ada, b_ada.reshape(1, n))


def _rms(x):
    return x * lax.rsqrt(jnp.mean(x * x, axis=-1, keepdims=True) + EPS)


def _in_kernel(x_ref, mod_ref, g_ref, w_ref, wuk_ref, kvg_ref,
               qa_ref, ka_ref, va_ref, qlt_ref, ckv_ref, ckvt_ref, qit_ref, ki_ref, wit_ref,
               *, nh, idx_scale):
    hd = HEAD_DIM
    h = _rms(x_ref[0]) * g_ref[...]
    h = h * (1.0 + mod_ref[0, 1:2, :]) + mod_ref[0, 0:1, :]
    proj = jnp.dot(h.astype(BF16), w_ref[...], preferred_element_type=F32)
    wa = nh * hd
    rank = ckvt_ref.shape[1]
    tm = x_ref.shape[1]
    o_qb, o_ckv = 3 * wa, 4 * wa
    o_qi = o_ckv + rank
    o_ki = o_qi + IDX_HEADS * IDX_DIM
    qa_ref[0] = proj[:, 0:wa] * (hd ** -0.5 * LOG2E)
    ka_ref[0] = proj[:, wa:2 * wa]
    va_ref[0] = proj[:, 2 * wa:3 * wa]
    for hh in range(nh):
        qb_h = proj[:, o_qb + hh * hd:o_qb + (hh + 1) * hd].astype(BF16)
        q_lat = jnp.dot(qb_h, wuk_ref[hh], preferred_element_type=F32) * (hd ** -0.5 * LOG2E)
        qlt_ref[0, hh * rank:(hh + 1) * rank, :] = q_lat.T.astype(BF16)
    ckv = (_rms(proj[:, o_ckv:o_ckv + rank]) * kvg_ref[...])
    ckvt_ref[0] = ckv.T.astype(BF16)
    t = pl.program_id(1) * tm + lax.broadcasted_iota(jnp.int32, (tm, LANES), 0)
    lane = lax.broadcasted_iota(jnp.int32, (tm, LANES), 1)
    pos = jnp.where(lane < 2, t // POS_RADIX, jnp.where(lane < 4, t % POS_RADIX, jnp.where(lane < 6, 1, 0)))
    ckv_ref[0] = jnp.concatenate([ckv, pos.astype(F32)], axis=-1).astype(BF16)
    for pp in range(IDX_HEADS * IDX_DIM // LANES):
        qit_ref[0, pp * LANES:(pp + 1) * LANES, :] = proj[:, o_qi + pp * LANES:o_qi + (pp + 1) * LANES].T.astype(BF16)
    kiw = proj[:, o_ki:o_ki + LANES]
    ki_ref[0] = kiw[:, :IDX_DIM].astype(BF16)
    wit_ref[0] = kiw.T[IDX_DIM:IDX_DIM + IDX_HEADS, :] * idx_scale


def _in_call(x, mod, g, w_in_p, w_uk, kvg, *, nh, tm):
    bsz, seq, d = x.shape
    n_pad = w_in_p.shape[1]
    rank = w_uk.shape[-1]
    hd = HEAD_DIM
    wa = nh * hd
    idx_scale = float((IDX_HEADS * IDX_DIM) ** -0.5)
    row_spec = lambda w: pl.BlockSpec((1, tm, w), lambda b, i: (b, i, 0))
    col_spec = lambda r: pl.BlockSpec((1, r, tm), lambda b, i: (b, 0, i))
    return pl.pallas_call(
        functools.partial(_in_kernel, nh=nh, idx_scale=idx_scale),
        grid=(bsz, seq // tm),
        in_specs=[pl.BlockSpec((1, tm, d), lambda b, i: (b, i, 0)),
                  pl.BlockSpec((1, 6, d), lambda b, i: (b, 0, 0)),
                  _const_spec((1, d)),
                  _const_spec((d, n_pad)),
                  _const_spec((nh, hd, rank)),
                  _const_spec((1, rank))],
        out_specs=[row_spec(wa), row_spec(wa), row_spec(wa),
                   col_spec(nh * rank), row_spec(rank + LANES), col_spec(rank),
                   col_spec(IDX_HEADS * IDX_DIM), row_spec(IDX_DIM), col_spec(IDX_HEADS)],
        out_shape=[jax.ShapeDtypeStruct((bsz, seq, wa), F32),
                   jax.ShapeDtypeStruct((bsz, seq, wa), F32),
                   jax.ShapeDtypeStruct((bsz, seq, wa), F32),
                   jax.ShapeDtypeStruct((bsz, nh * rank, seq), BF16),
                   jax.ShapeDtypeStruct((bsz, seq, rank + LANES), BF16),
                   jax.ShapeDtypeStruct((bsz, rank, seq), BF16),
                   jax.ShapeDtypeStruct((bsz, IDX_HEADS * IDX_DIM, seq), BF16),
                   jax.ShapeDtypeStruct((bsz, seq, IDX_DIM), BF16),
                   jax.ShapeDtypeStruct((bsz, IDX_HEADS, seq), F32)],
        compiler_params=_cparams(("parallel", "parallel")),
        name="in_proj",
    )(x, mod, g, w_in_p, w_uk, kvg)


def _band_tables(tq, kw, w_res):
    r = np.arange(tq)[:, None]
    c = np.arange(kw)[None, :]
    clean = lambda d: np.where((d >= 0) & (d <= w_res), d, -1).astype(np.float32)
    return jnp.asarray(np.stack([clean(r - c), clean(r - c + w_res)]))


def _dil_kernel(slope_ref, q_ref, k_ref, v_ref, *refs, w_res, plan):
    n_tab = 1 + max(t for _, _, t in plan)
    dtab_refs = refs[:n_tab]
    o_ref, m_ref, l_ref, acc_ref, bias_ref, s_ref, mloc_ref = refs[n_tab:]
    seq = q_ref.shape[1]
    hp = pl.program_id(1)
    lane = lax.broadcasted_iota(jnp.int32, (1, LANES), 1)
    first = lane < HEAD_DIM
    keep = (first, jnp.logical_not(first))
    m_ref[...] = jnp.full(m_ref.shape, NEG, F32)
    l_ref[...] = jnp.zeros(l_ref.shape, F32)
    acc_ref[...] = jnp.zeros(acc_ref.shape, F32)

    for dil, tq, tab in plan:
        dtab = dtab_refs[tab]
        kw = dtab.shape[-1]
        nt = seq // dil // tq
        for hh in range(2):
            neg_slope = -slope_ref[hp * 2 + hh] * (float(dil) * LOG2E)
            for off in range(2):
                bias_ref[off, hh, :tq, :kw] = jnp.where(dtab[off] >= 0.0, neg_slope * dtab[off], NEG)

        def rows(start, size, dil=dil):
            return pl.ds(start, size) if dil == 1 else pl.ds(start, size, stride=dil)

        def coords(w, dil=dil, tq=tq, nt=nt):
            r = w // nt
            jt = w - r * nt
            q0 = jt * (tq * dil) + r
            k0 = jnp.maximum(jt * tq - w_res, 0) * dil + r
            return q0, k0, jnp.minimum(jt, 1)

        def logits_stage(w, rows=rows, coords=coords, tq=tq, kw=kw):
            q0, k0, off = coords(w)
            q2 = q_ref[0, rows(q0, tq), :]
            k2 = k_ref[0, rows(k0, kw), :].astype(BF16)
            for hh in range(2):
                qh = jnp.where(keep[hh], q2, 0.0).astype(BF16)
                s = lax.dot_general(qh, k2, _NT, preferred_element_type=F32) + bias_ref[off, hh, :tq, :kw]
                s_ref[hh, :tq, :kw] = s
                mloc_ref[hh, :tq, :] = jnp.broadcast_to(jnp.max(s, axis=-1, keepdims=True), (tq, LANES))

        def value_stage(w, rows=rows, coords=coords, tq=tq, kw=kw):
            q0, k0, _ = coords(w)
            v2 = v_ref[0, rows(k0, kw), :].astype(BF16)
            ms = [mloc_ref[hh, :tq, :] for hh in range(2)]
            ps = [jnp.exp2(s_ref[hh, :tq, :kw] - jnp.concatenate([ms[hh]] * (kw // LANES), axis=-1))
                  for hh in range(2)]
            ls = [jnp.sum(p, axis=-1, keepdims=True) for p in ps]
            outs = [jnp.dot(p.astype(BF16), v2, preferred_element_type=F32) for p in ps]
            pick = lambda pair: jnp.where(first, pair[0], pair[1])
            m_t, l_t, o_t = pick(ms), pick(ls), pick(outs)
            q_rows = rows(q0, tq)
            m_old = m_ref[q_rows, :]
            m_new = jnp.maximum(m_old, m_t)
            a_old = jnp.exp2(m_old - m_new)
            a_new = jnp.exp2(m_t - m_new)
            acc_ref[q_rows, :] = a_old * acc_ref[q_rows, :] + a_new * o_t
            l_ref[q_rows, :] = a_old * l_ref[q_rows, :] + a_new * l_t
            m_ref[q_rows, :] = m_new

        def pipe_body(w, carry, logits_stage=logits_stage, value_stage=value_stage):
            value_stage(w - 1)
            logits_stage(w)
            return carry

        n_tiles = dil * nt
        logits_stage(0)
        lax.fori_loop(1, n_tiles, pipe_body, 0)
        value_stage(n_tiles - 1)

    o_ref[0] = (acc_ref[...] / l_ref[...]).astype(BF16)


def _dil_call(slopes, qa, ka, va):
    bsz, seq, wa = qa.shape
    w_res = DILATED_CONFIGS[0][0] // DILATED_CONFIGS[0][1]
    assert all(w // d == w_res for w, d in DILATED_CONFIGS)
    tile_shapes, plan = [], []
    for _, dil in DILATED_CONFIGS:
        n = seq // dil
        tq = min(256, n)
        kw = tq if n == tq else tq + w_res
        assert n % tq == 0 and tq >= w_res and kw <= n
        if (tq, kw) not in tile_shapes:
            tile_shapes.append((tq, kw))
        plan.append((dil, tq, tile_shapes.index((tq, kw))))
    tables = [_band_tables(tq, kw, w_res) for tq, kw in tile_shapes]
    tq_max = max(tq for tq, _ in tile_shapes)
    kw_max = max(kw for _, kw in tile_shapes)
    blk = pl.BlockSpec((1, seq, LANES), lambda b, h: (b, 0, h))
    return pl.pallas_call(
        functools.partial(_dil_kernel, w_res=w_res, plan=tuple(plan)),
        grid=(bsz, wa // LANES),
        in_specs=[pl.BlockSpec(memory_space=pltpu.SMEM), blk, blk, blk] + [_const_spec(t.shape) for t in tables],
        out_specs=blk,
        out_shape=jax.ShapeDtypeStruct((bsz, seq, wa), BF16),
        scratch_shapes=[pltpu.VMEM((seq, LANES), F32), pltpu.VMEM((seq, LANES), F32),
                        pltpu.VMEM((seq, LANES), F32),
                        pltpu.VMEM((2, 2, tq_max, kw_max), F32),
                        pltpu.VMEM((2, tq_max, kw_max), F32),
                        pltpu.VMEM((2, tq_max, LANES), F32)],
        compiler_params=_cparams(("parallel", "parallel")),
        name="dilated_attn",
    )(jnp.asarray(slopes, F32), qa, ka, va, *tables)


def _key_to_f32(key):
    bits = key ^ (lax.shift_right_arithmetic(key, 31) & jnp.int32(0x7FFFFFFF))
    return lax.bitcast_convert_type(bits, F32)


def _dsa_kernel(qit_ref, wit_ref, ki_ref, qlt_ref, qpos_ref, ckv_ref, ckvt_ref, wuv_ref, tri_ref, o_ref,
                sc_ref, acc_ref, s_ref, s2_ref, l_ref, *, topk, chunk):
    i = pl.program_id(1)
    t0 = i * BLOCK
    rank = ckvt_ref.shape[1]
    nh = qlt_ref.shape[1] // rank
    n_ch = (t0 + BLOCK + chunk - 1) // chunk
    kk = lax.broadcasted_iota(jnp.int32, (chunk, BLOCK), 0)
    qq = lax.broadcasted_iota(jnp.int32, (chunk, BLOCK), 1)
    rel = qq - kk
    k_start = lambda j: pl.multiple_of(j * chunk, chunk)
    heads = lambda ref, width: jnp.concatenate(
        [ref[0, hh * width:(hh + 1) * width, :] for hh in range(ref.shape[1] // width)], axis=-1)

    qit = heads(qit_ref, IDX_DIM)
    wt = wit_ref[0]

    idx_pair = 2 * BLOCK
    idx_cols = [slice(pp * idx_pair, (pp + 1) * idx_pair) for pp in range(IDX_HEADS * BLOCK // idx_pair)]

    def idx_logits(j, buf):
        kc = ki_ref[0, pl.ds(k_start(j), chunk), :]
        for cols in idx_cols:
            buf[:, cols] = jnp.dot(kc, qit[:, cols], preferred_element_type=F32)

    def idx_scores(j, buf):
        k0 = k_start(j)
        score = jnp.zeros((chunk, BLOCK), F32)
        for pp, cols in enumerate(idx_cols):
            lg = buf[:, cols]
            score = score + (wt[2 * pp:2 * pp + 1, :] * jnp.maximum(lg[:, :BLOCK], 0.0)
                             + wt[2 * pp + 1:2 * pp + 2, :] * jnp.maximum(lg[:, BLOCK:], 0.0))
        score = jnp.where(score == 0.0, 0.0, score)
        sc_ref[pl.ds(k0, chunk), :] = jnp.where(rel + (t0 - k0) >= 0, score, -jnp.inf)

    def idx_advance(j, buf_next, buf_cur):
        idx_logits(j + 1, buf_next)
        idx_scores(j, buf_cur)

    def idx_double_step(t, carry):
        idx_advance(2 * t, s2_ref, s_ref)
        idx_advance(2 * t + 1, s_ref, s2_ref)
        return carry

    idx_logits(0, s_ref)
    n_double = (n_ch - 1) // 2
    lax.fori_loop(0, n_double, idx_double_step, 0)
    last = 2 * n_double

    @pl.when(last == n_ch - 1)
    def _():
        idx_scores(last, s_ref)

    @pl.when(last < n_ch - 1)
    def _():
        idx_advance(last, s2_ref, s_ref)
        idx_scores(last + 1, s2_ref)

    def count(pred):
        def body(j, cnt):
            hit = jnp.where(pred(sc_ref[pl.ds(k_start(j), chunk), :]), 1.0, 0.0)
            return cnt + jnp.sum(hit.reshape(-1, COUNT_CHAINS, SUBLANES, BLOCK), axis=0)
        cnt = lax.fori_loop(0, n_ch, body, jnp.zeros((COUNT_CHAINS, SUBLANES, BLOCK), F32))
        return jnp.sum(jnp.sum(cnt, axis=0), axis=0, keepdims=True)

    kf = float(topk)
    n_all = (n_ch * chunk).astype(F32)
    n_pos = count(lambda s: s >= 0.0)
    pos_ok = n_pos >= kf
    key = jnp.where(pos_ok, jnp.int32(0), jnp.int32(-2 ** 31))
    n_key = jnp.where(pos_ok, n_pos, n_all)

    def bit_body(b, carry):
        key, n_key = carry
        trial = key | lax.shift_left(jnp.int32(1), 30 - b)
        cand = _key_to_f32(trial)
        n_trial = count(lambda s: s >= cand)
        ok = n_trial >= kf
        return jnp.where(ok, trial, key), jnp.where(ok, n_trial, n_key)

    tq = t0 + lax.broadcasted_iota(jnp.int32, (1, BLOCK), 1)
    few = tq < topk - 1
    key, n_key = lax.fori_loop(0, 31, bit_body, (key, n_key))
    tau = jnp.where(few, F32_MIN, _key_to_f32(key))

    @pl.when(jnp.max(jnp.where(few, 0.0, n_key - kf)) > 0.0)
    def _():
        n_tie = kf - count(lambda s: s > tau)

        def tie_body(j, seen):
            k0 = k_start(j)
            s = sc_ref[pl.ds(k0, chunk), :]
            eq = jnp.where(s == tau, 1.0, 0.0)
            rank_eq = seen + jnp.dot(tri_ref[...], eq.astype(BF16), preferred_element_type=F32)
            drop = jnp.where(rank_eq > n_tie, eq, 0.0)
            sc_ref[pl.ds(k0, chunk), :] = jnp.where(drop > 0.0, -jnp.inf, s)
            return seen + jnp.sum(eq, axis=0, keepdims=True)

        lax.fori_loop(0, n_ch, tie_body, jnp.zeros((1, BLOCK), F32))

    pad_rows = ckv_ref.shape[-1] - rank - POS_ROWS
    q_aug = jnp.concatenate([heads(qlt_ref, rank), qpos_ref[0],
                             jnp.zeros((pad_rows, nh * BLOCK), BF16)], axis=0)
    acc_ref[...] = jnp.zeros(acc_ref.shape, F32)
    pair = 2 * BLOCK
    n_pairs = nh * BLOCK // pair
    col_slices = [slice(pp * pair, (pp + 1) * pair) for pp in range(n_pairs)]

    cat = lambda parts: jnp.concatenate(parts, axis=-1)

    def logits_stage(j, buf):
        mask = jnp.where(sc_ref[pl.ds(k_start(j), chunk), :] >= tau, 0.0, NEG)
        mask2 = jnp.concatenate([mask, mask], axis=-1)
        kc = ckv_ref[0, pl.ds(k_start(j), chunk), :]
        cmax = []
        for cols in col_slices:
            s = jnp.dot(kc, q_aug[:, cols], preferred_element_type=F32) + mask2
            buf[:, cols] = s
            cmax.append(jnp.max(s, axis=0, keepdims=True))
        return cat(cmax)

    def value_stage(j, buf, m_use, alpha):
        kct = ckvt_ref[0, :, pl.ds(k_start(j), chunk)]
        sums = []
        for cols in col_slices:
            p = jnp.exp2(buf[:, cols] - m_use[:, cols])
            pv = jnp.dot(kct, p.astype(BF16), preferred_element_type=F32)
            acc_ref[:, cols] = alpha[:, cols] * acc_ref[:, cols] + pv
            sums.append(jnp.sum(p, axis=0, keepdims=True))
        return cat(sums)

    def finish(j, buf_cur, carry):
        m_cur, alpha_cur, l_prev = carry
        return alpha_cur * l_prev + value_stage(j, buf_cur, m_cur, alpha_cur)

    def advance(j, buf_next, buf_cur, carry):
        m_cur = carry[0]
        m_next = jnp.maximum(m_cur, logits_stage(j + 1, buf_next))
        return m_next, jnp.exp2(m_cur - m_next), finish(j, buf_cur, carry)

    def double_step(t, carry):
        carry = advance(2 * t, s2_ref, s_ref, carry)
        return advance(2 * t + 1, s_ref, s2_ref, carry)

    zeros_row = jnp.zeros((1, nh * BLOCK), F32)
    n_double = (n_ch - 1) // 2
    carry = lax.fori_loop(0, n_double, double_step, (logits_stage(0, s_ref), zeros_row, zeros_row))
    last = 2 * n_double

    @pl.when(last == n_ch - 1)
    def _():
        l_ref[...] = finish(last, s_ref, carry)

    @pl.when(last < n_ch - 1)
    def _():
        l_ref[...] = finish(last + 1, s2_ref, advance(last, s2_ref, s_ref, carry))

    l_fin = l_ref[...]

    o_lat_t = acc_ref[...] / l_fin
    outs = [jnp.dot(o_lat_t[:, hh * BLOCK:(hh + 1) * BLOCK].T.astype(BF16), wuv_ref[hh],
                    preferred_element_type=F32) for hh in range(nh)]
    o_ref[0] = jnp.concatenate(outs, axis=-1).astype(BF16)


def _alibi_query_table(slopes, n_blocks):
    sl = np.asarray(slopes, np.float64) * LOG2E
    t0 = np.arange(n_blocks, dtype=np.float64)[:, None] * BLOCK
    coef = [np.broadcast_to(sl * POS_RADIX, (n_blocks, sl.size)),
            np.broadcast_to(sl, (n_blocks, sl.size)),
            -sl[None, :] * t0]
    rows = []
    for cf in coef:
        hi = jnp.asarray(cf, F32).astype(BF16)
        lo = (jnp.asarray(cf, F32) - hi.astype(F32)).astype(BF16)
        rows += [hi, lo]
    tab = jnp.stack(rows + [jnp.zeros_like(rows[0])] * (POS_ROWS - len(rows)), axis=1)
    return jnp.repeat(tab, BLOCK, axis=-1)


def _dsa_call(slopes, qit, wit, ki, qlt, ckv, ckvt, w_uv, *, chunk=256):
    bsz, seq, ckv_w = ckv.shape
    rank = ckvt.shape[1]
    nh, _, hd = w_uv.shape
    topk = min(TOPK_MAX, seq // 4)
    assert seq % chunk == 0 and seq // POS_RADIX <= 256
    tri = jnp.asarray(np.tril(np.ones((chunk, chunk), np.float32)), BF16)
    qpos = _alibi_query_table(slopes, seq // BLOCK)
    return pl.pallas_call(
        functools.partial(_dsa_kernel, topk=topk, chunk=chunk),
        grid=(bsz, seq // BLOCK),
        in_specs=[pl.BlockSpec((1, IDX_HEADS * IDX_DIM, BLOCK), lambda b, i: (b, 0, i)),
                  pl.BlockSpec((1, IDX_HEADS, BLOCK), lambda b, i: (b, 0, i)),
                  pl.BlockSpec((1, seq, IDX_DIM), lambda b, i: (b, 0, 0)),
                  pl.BlockSpec((1, nh * rank, BLOCK), lambda b, i: (b, 0, i)),
                  pl.BlockSpec((1, POS_ROWS, nh * BLOCK), lambda b, i: (i, 0, 0)),
                  pl.BlockSpec((1, seq, ckv_w), lambda b, i: (b, 0, 0)),
                  pl.BlockSpec((1, rank, seq), lambda b, i: (b, 0, 0)),
                  _const_spec((nh, rank, hd)),
                  _const_spec((chunk, chunk))],
        out_specs=pl.BlockSpec((1, BLOCK, nh * hd), lambda b, i: (b, i, 0)),
        out_shape=jax.ShapeDtypeStruct((bsz, seq, nh * hd), BF16),
        scratch_shapes=[pltpu.VMEM((seq, BLOCK), F32),
                        pltpu.VMEM((rank, nh * BLOCK), F32),
                        pltpu.VMEM((chunk, nh * BLOCK), F32),
                        pltpu.VMEM((chunk, nh * BLOCK), F32),
                        pltpu.VMEM((1, nh * BLOCK), F32)],
        compiler_params=_cparams(("parallel", "arbitrary")),
        name="sparse_attn",
    )(qit, wit, ki, qlt, qpos, ckv, ckvt, w_uv, tri)


def _out_kernel(x_ref, oa_ref, ob_ref, mod_ref, wo_ref, g2_ref, wgu_ref, wd_ref, gf_ref, o_ref,
                *, d_ff, ff_chunks, final_norm):
    mixed = jnp.concatenate([oa_ref[0], ob_ref[0]], axis=-1)
    x1 = x_ref[0] + mod_ref[0, 2:3, :] * jnp.dot(mixed, wo_ref[...], preferred_element_type=F32)
    h2 = _rms(x1) * g2_ref[...]
    h2 = (h2 * (1.0 + mod_ref[0, 4:5, :]) + mod_ref[0, 3:4, :]).astype(BF16)
    fc = d_ff // ff_chunks
    ffn = jnp.zeros(x1.shape, F32)
    for cc in range(ff_chunks):
        gate = jnp.dot(h2, wgu_ref[:, cc * fc:(cc + 1) * fc], preferred_element_type=F32)
        up = jnp.dot(h2, wgu_ref[:, d_ff + cc * fc:d_ff + (cc + 1) * fc], preferred_element_type=F32)
        act = gate * (1.0 / (1.0 + jnp.exp(-gate))) * up
        ffn = ffn + jnp.dot(act.astype(BF16), wd_ref[cc * fc:(cc + 1) * fc, :], preferred_element_type=F32)
    x2 = x1 + mod_ref[0, 5:6, :] * ffn
    if final_norm:
        x2 = _rms(x2) * gf_ref[...]
    o_ref[0] = x2


def _out_call(x, oa, ob, mod, w_out, g_ffn, w_gu, w_down, g_final, *, tm, final_norm):
    bsz, seq, d = x.shape
    d_ff = w_down.shape[0]
    ff_chunks = 2 if d_ff % (2 * LANES) == 0 else 1
    row = lambda w: pl.BlockSpec((1, tm, w), lambda b, i: (b, i, 0))
    return pl.pallas_call(
        functools.partial(_out_kernel, d_ff=d_ff, ff_chunks=ff_chunks, final_norm=final_norm),
        grid=(bsz, seq // tm),
        in_specs=[row(d), row(oa.shape[-1]), row(ob.shape[-1]),
                  pl.BlockSpec((1, 6, d), lambda b, i: (b, 0, 0)),
                  _const_spec((d, d)), _const_spec((1, d)),
                  _const_spec((d, 2 * d_ff)), _const_spec((d_ff, d)), _const_spec((1, d))],
        out_specs=row(d),
        out_shape=jax.ShapeDtypeStruct((bsz, seq, d), F32),
        compiler_params=_cparams(("parallel", "parallel")),
        name="out_proj_ffn",
    )(x, oa, ob, mod, w_out, g_ffn, w_gu, w_down, g_final)


def _alibi_slopes(n_heads):
    s = 2.0 ** (-8.0 * (np.arange(n_heads, dtype=np.float32) + 1.0) / n_heads)
    return s[0::2], s[1::2]


def kernel(x, c, w_ada, b_ada, g_attn, w_in, kv_norm_g, w_uk, w_uv, w_out, g_ffn, w_gu, w_down, g_final):
    bsz, seq, d = x.shape
    depth = w_ada.shape[0]
    nh = w_uk.shape[1]
    slopes_a, slopes_b = _alibi_slopes(2 * nh)
    d_in = w_in.shape[-1]
    n_pad = -(-d_in // LANES) * LANES
    tm = 512 if seq % 512 == 0 else BLOCK
    for l in range(depth):
        mod = _mod_call(c, w_ada[l], b_ada[l]).reshape(bsz, 6, d)
        w_in_p = jnp.pad(w_in[l], ((0, 0), (0, n_pad - d_in))).astype(BF16)
        qa, ka, va, qlt, ckv, ckvt, qit, ki, wit = _in_call(
            x, mod, g_attn[l].reshape(1, d), w_in_p, w_uk[l].astype(BF16),
            kv_norm_g[l].reshape(1, -1), nh=nh, tm=tm)
        out_a = _dil_call(slopes_a, qa, ka, va)
        out_b = _dsa_call(slopes_b, qit, wit, ki, qlt, ckv, ckvt, w_uv[l].astype(BF16))
        x = _out_call(x, out_a, out_b, mod, w_out[l].astype(BF16), g_ffn[l].reshape(1, d),
                      w_gu[l].astype(BF16), w_down[l].astype(BF16), g_final.reshape(1, d),
                      tm=tm, final_norm=(l == depth - 1))
    return x
```

```python
import functools

import numpy as np
import jax
import jax.numpy as jnp
from jax import lax
from jax.experimental import pallas as pl
from jax.experimental.pallas import tpu as pltpu

F32 = jnp.float32
BF16 = jnp.bfloat16

HEAD_DIM = 64
BLOCK = 128
DILATED_CONFIGS = ((128, 1), (512, 4), (2048, 16))
IDX_HEADS = 8
IDX_DIM = 64
TOPK_MAX = 256
EPS = 1e-6
NEG = -1e30
F32_MIN = float(np.finfo(np.float32).min)
LOG2E = float(np.log2(np.e))
POS_RADIX = 64
POS_ROWS = 16
LANES = 128
SUBLANES = 8
COUNT_CHAINS = 8
VMEM_LIMIT = 56 * 1024 * 1024

_NT = (((1,), (1,)), ((), ()))


def _cparams(sem, flags=None):
    return pltpu.CompilerParams(dimension_semantics=sem, vmem_limit_bytes=VMEM_LIMIT, flags=flags)


def _const_spec(shape):
    nd = len(shape)
    return pl.BlockSpec(shape, lambda *_: (0,) * nd, pipeline_mode=pl.Buffered(1))


def _mod_kernel(c_ref, w_ref, b_ref, o_ref):
    c = c_ref[...]
    c_act = c * (1.0 / (1.0 + jnp.exp(-c)))
    o_ref[...] = jnp.dot(c_act, w_ref[...], preferred_element_type=F32,
                         precision=lax.Precision.HIGHEST) + b_ref[...]


def _mod_call(c, w_ada, b_ada):
    bsz, d = c.shape
    n = w_ada.shape[1]
    tn = 1024
    return pl.pallas_call(
        _mod_kernel,
        grid=(n // tn,),
        in_specs=[pl.BlockSpec((bsz, d), lambda j: (0, 0)),
                  pl.BlockSpec((d, tn), lambda j: (0, j)),
                  pl.BlockSpec((1, tn), lambda j: (0, j))],
        out_specs=pl.BlockSpec((bsz, tn), lambda j: (0, j)),
        out_shape=jax.ShapeDtypeStruct((bsz, n), F32),
        compiler_params=_cparams(("arbitrary",)),
        name="adaln_mod",
    )(c, w_ada, b_ada.reshape(1, n))


def _rms(x):
    return x * lax.rsqrt(jnp.mean(x * x, axis=-1, keepdims=True) + EPS)


def _in_kernel(x_ref, mod_ref, g_ref, w_ref, wuk_ref, kvg_ref,
               qa_ref, ka_ref, va_ref, qlt_ref, ckv_ref, ckvt_ref, qit_ref, ki_ref, wit_ref,
               *, nh, idx_scale):
    hd = HEAD_DIM
    h = _rms(x_ref[0]) * g_ref[...]
    h = h * (1.0 + mod_ref[0, 1:2, :]) + mod_ref[0, 0:1, :]
    proj = jnp.dot(h.astype(BF16), w_ref[...], preferred_element_type=F32)
    wa = nh * hd
    rank = ckvt_ref.shape[1]
    tm = x_ref.shape[1]
    o_qb, o_ckv = 3 * wa, 4 * wa
    o_qi = o_ckv + rank
    o_ki = o_qi + IDX_HEADS * IDX_DIM
    qa_ref[0] = proj[:, 0:wa] * (hd ** -0.5 * LOG2E)
    ka_ref[0] = proj[:, wa:2 * wa]
    va_ref[0] = proj[:, 2 * wa:3 * wa]
    for hh in range(nh):
        qb_h = proj[:, o_qb + hh * hd:o_qb + (hh + 1) * hd].astype(BF16)
        q_lat = jnp.dot(qb_h, wuk_ref[hh], preferred_element_type=F32) * (hd ** -0.5 * LOG2E)
        qlt_ref[0, hh * rank:(hh + 1) * rank, :] = q_lat.T.astype(BF16)
    ckv = (_rms(proj[:, o_ckv:o_ckv + rank]) * kvg_ref[...])
    ckvt_ref[0] = ckv.T.astype(BF16)
    t = pl.program_id(1) * tm + lax.broadcasted_iota(jnp.int32, (tm, LANES), 0)
    lane = lax.broadcasted_iota(jnp.int32, (tm, LANES), 1)
    pos = jnp.where(lane < 2, t // POS_RADIX, jnp.where(lane < 4, t % POS_RADIX, jnp.where(lane < 6, 1, 0)))
    ckv_ref[0] = jnp.concatenate([ckv, pos.astype(F32)], axis=-1).astype(BF16)
    for pp in range(IDX_HEADS * IDX_DIM // LANES):
        qit_ref[0, pp * LANES:(pp + 1) * LANES, :] = proj[:, o_qi + pp * LANES:o_qi + (pp + 1) * LANES].T.astype(BF16)
    kiw = proj[:, o_ki:o_ki + LANES]
    ki_ref[0] = kiw[:, :IDX_DIM].astype(BF16)
    wit_ref[0] = kiw.T[IDX_DIM:IDX_DIM + IDX_HEADS, :] * idx_scale


def _in_call(x, mod, g, w_in_p, w_uk, kvg, *, nh, tm):
    bsz, seq, d = x.shape
    n_pad = w_in_p.shape[1]
    rank = w_uk.shape[-1]
    hd = HEAD_DIM
    wa = nh * hd
    idx_scale = float((IDX_HEADS * IDX_DIM) ** -0.5)
    row_spec = lambda w: pl.BlockSpec((1, tm, w), lambda b, i: (b, i, 0))
    col_spec = lambda r: pl.BlockSpec((1, r, tm), lambda b, i: (b, 0, i))
    return pl.pallas_call(
        functools.partial(_in_kernel, nh=nh, idx_scale=idx_scale),
        grid=(bsz, seq // tm),
        in_specs=[pl.BlockSpec((1, tm, d), lambda b, i: (b, i, 0)),
                  pl.BlockSpec((1, 6, d), lambda b, i: (b, 0, 0)),
                  _const_spec((1, d)),
                  _const_spec((d, n_pad)),
                  _const_spec((nh, hd, rank)),
                  _const_spec((1, rank))],
        out_specs=[row_spec(wa), row_spec(wa), row_spec(wa),
                   col_spec(nh * rank), row_spec(rank + LANES), col_spec(rank),
                   col_spec(IDX_HEADS * IDX_DIM), row_spec(IDX_DIM), col_spec(IDX_HEADS)],
        out_shape=[jax.ShapeDtypeStruct((bsz, seq, wa), F32),
                   jax.ShapeDtypeStruct((bsz, seq, wa), F32),
                   jax.ShapeDtypeStruct((bsz, seq, wa), F32),
                   jax.ShapeDtypeStruct((bsz, nh * rank, seq), BF16),
                   jax.ShapeDtypeStruct((bsz, seq, rank + LANES), BF16),
                   jax.ShapeDtypeStruct((bsz, rank, seq), BF16),
                   jax.ShapeDtypeStruct((bsz, IDX_HEADS * IDX_DIM, seq), BF16),
                   jax.ShapeDtypeStruct((bsz, seq, IDX_DIM), BF16),
                   jax.ShapeDtypeStruct((bsz, IDX_HEADS, seq), F32)],
        compiler_params=_cparams(("parallel", "parallel")),
        name="in_proj",
    )(x, mod, g, w_in_p, w_uk, kvg)


def _band_tables(tq, kw, w_res):
    r = np.arange(tq)[None, :]
    c = np.arange(kw)[:, None]
    clean = lambda d: np.where((d >= 0) & (d <= w_res), d, -1).astype(np.float32)
    return jnp.asarray(np.stack([clean(r - c), clean(r - c + w_res)]))


def _dil_kernel(slope_ref, q_ref, k_ref, v_ref, *refs, w_res, plan):
    n_tab = 1 + max(t for _, _, t in plan)
    dtab_refs = refs[:n_tab]
    o_ref, m_ref, l_ref, acc_ref, bias_ref, s_ref, mloc_ref = refs[n_tab:]
    seq = q_ref.shape[1]
    hp = pl.program_id(1)
    feat = lax.broadcasted_iota(jnp.int32, (LANES, 1), 0)
    first = feat < HEAD_DIM
    keep = (first, jnp.logical_not(first))
    m_ref[...] = jnp.full(m_ref.shape, NEG, F32)
    l_ref[...] = jnp.zeros(l_ref.shape, F32)
    acc_ref[...] = jnp.zeros(acc_ref.shape, F32)

    for dil, tq, tab in plan:
        dtab = dtab_refs[tab]
        kw = dtab.shape[1]
        nt = seq // dil // tq
        for hh in range(2):
            neg_slope = -slope_ref[hp * 2 + hh] * (float(dil) * LOG2E)
            for off in range(2):
                bias_ref[off, hh, :kw, :tq] = jnp.where(dtab[off] >= 0.0, neg_slope * dtab[off], NEG)

        def rows(start, size, dil=dil):
            return pl.ds(start, size) if dil == 1 else pl.ds(start, size, stride=dil)

        def coords(w, dil=dil, tq=tq, nt=nt):
            r = w // nt
            jt = w - r * nt
            q0 = jt * (tq * dil) + r
            k0 = jnp.maximum(jt * tq - w_res, 0) * dil + r
            return q0, k0, jnp.minimum(jt, 1)

        def logits_stage(w, rows=rows, coords=coords, tq=tq, kw=kw):
            q0, k0, off = coords(w)
            q_t = q_ref[0, rows(q0, tq), :].T
            k2 = k_ref[0, rows(k0, kw), :].astype(BF16)
            for hh in range(2):
                qh = jnp.where(keep[hh], q_t, 0.0).astype(BF16)
                s = jnp.dot(k2, qh, preferred_element_type=F32) + bias_ref[off, hh, :kw, :tq]
                s_ref[hh, :kw, :tq] = s
                mloc_ref[hh, :, :tq] = jnp.max(s, axis=0, keepdims=True)

        def value_stage(w, rows=rows, coords=coords, tq=tq, kw=kw):
            q0, k0, _ = coords(w)
            v_t = v_ref[0, rows(k0, kw), :].T.astype(BF16)
            ms = [mloc_ref[hh, :, :tq] for hh in range(2)]
            ps = [jnp.exp2(s_ref[hh, :kw, :tq] - ms[hh]) for hh in range(2)]
            ls = [jnp.sum(p, axis=0, keepdims=True) for p in ps]
            outs = [jnp.dot(v_t, p.astype(BF16), preferred_element_type=F32) for p in ps]
            pick = lambda pair: jnp.where(first, pair[0], pair[1]).T
            spread = lambda pair: [jnp.broadcast_to(x, (LANES, tq)) for x in pair]
            m_t, l_t, o_t = pick(spread(ms)), pick(spread(ls)), pick(outs)
            q_rows = rows(q0, tq)
            m_old = m_ref[q_rows, :]
            m_new = jnp.maximum(m_old, m_t)
            a_old = jnp.exp2(m_old - m_new)
            a_new = jnp.exp2(m_t - m_new)
            acc_ref[q_rows, :] = a_old * acc_ref[q_rows, :] + a_new * o_t
            l_ref[q_rows, :] = a_old * l_ref[q_rows, :] + a_new * l_t
            m_ref[q_rows, :] = m_new

        def pipe_body(w, carry, logits_stage=logits_stage, value_stage=value_stage):
            value_stage(w - 1)
            logits_stage(w)
            return carry

        n_tiles = dil * nt
        logits_stage(0)
        lax.fori_loop(1, n_tiles, pipe_body, 0)
        value_stage(n_tiles - 1)

    o_ref[0] = (acc_ref[...] / l_ref[...]).astype(BF16)


def _dil_call(slopes, qa, ka, va):
    bsz, seq, wa = qa.shape
    w_res = DILATED_CONFIGS[0][0] // DILATED_CONFIGS[0][1]
    assert all(w // d == w_res for w, d in DILATED_CONFIGS)
    tile_shapes, plan = [], []
    for _, dil in DILATED_CONFIGS:
        n = seq // dil
        tq = min(256, n)
        kw = tq if n == tq else tq + w_res
        assert n % tq == 0 and tq >= w_res and kw <= n
        if (tq, kw) not in tile_shapes:
            tile_shapes.append((tq, kw))
        plan.append((dil, tq, tile_shapes.index((tq, kw))))
    tables = [_band_tables(tq, kw, w_res) for tq, kw in tile_shapes]
    tq_max = max(tq for tq, _ in tile_shapes)
    kw_max = max(kw for _, kw in tile_shapes)
    blk = pl.BlockSpec((1, seq, LANES), lambda b, h: (b, 0, h))
    return pl.pallas_call(
        functools.partial(_dil_kernel, w_res=w_res, plan=tuple(plan)),
        grid=(bsz, wa // LANES),
        in_specs=[pl.BlockSpec(memory_space=pltpu.SMEM), blk, blk, blk] + [_const_spec(t.shape) for t in tables],
        out_specs=blk,
        out_shape=jax.ShapeDtypeStruct((bsz, seq, wa), BF16),
        scratch_shapes=[pltpu.VMEM((seq, LANES), F32), pltpu.VMEM((seq, LANES), F32),
                        pltpu.VMEM((seq, LANES), F32),
                        pltpu.VMEM((2, 2, kw_max, tq_max), F32),
                        pltpu.VMEM((2, kw_max, tq_max), F32),
                        pltpu.VMEM((2, 1, tq_max), F32)],
        compiler_params=_cparams(("parallel", "parallel")),
        name="dilated_attn",
    )(jnp.asarray(slopes, F32), qa, ka, va, *tables)


def _key_to_f32(key):
    bits = key ^ (lax.shift_right_arithmetic(key, 31) & jnp.int32(0x7FFFFFFF))
    return lax.bitcast_convert_type(bits, F32)


def _dsa_kernel(qit_ref, wit_ref, ki_ref, qlt_ref, qpos_ref, ckv_ref, ckvt_ref, wuv_ref, tri_ref, o_ref,
                sc_ref, acc_ref, s_ref, s2_ref, l_ref, *, topk, chunk):
    i = pl.program_id(1)
    t0 = i * BLOCK
    rank = ckvt_ref.shape[1]
    nh = qlt_ref.shape[1] // rank
    n_ch = (t0 + BLOCK + chunk - 1) // chunk
    kk = lax.broadcasted_iota(jnp.int32, (chunk, BLOCK), 0)
    qq = lax.broadcasted_iota(jnp.int32, (chunk, BLOCK), 1)
    rel = qq - kk
    k_start = lambda j: pl.multiple_of(j * chunk, chunk)
    heads = lambda ref, width: jnp.concatenate(
        [ref[0, hh * width:(hh + 1) * width, :] for hh in range(ref.shape[1] // width)], axis=-1)

    qit = heads(qit_ref, IDX_DIM)
    wt = wit_ref[0]

    idx_pair = 2 * BLOCK
    idx_cols = [slice(pp * idx_pair, (pp + 1) * idx_pair) for pp in range(IDX_HEADS * BLOCK // idx_pair)]

    def idx_logits(j, buf):
        kc = ki_ref[0, pl.ds(k_start(j), chunk), :]
        for cols in idx_cols:
            buf[:, cols] = jnp.dot(kc, qit[:, cols], preferred_element_type=F32)

    def idx_scores(j, buf):
        k0 = k_start(j)
        score = jnp.zeros((chunk, BLOCK), F32)
        for pp, cols in enumerate(idx_cols):
            lg = buf[:, cols]
            score = score + (wt[2 * pp:2 * pp + 1, :] * jnp.maximum(lg[:, :BLOCK], 0.0)
                             + wt[2 * pp + 1:2 * pp + 2, :] * jnp.maximum(lg[:, BLOCK:], 0.0))
        score = jnp.where(score == 0.0, 0.0, score)
        sc_ref[pl.ds(k0, chunk), :] = jnp.where(rel + (t0 - k0) >= 0, score, -jnp.inf)

    def idx_advance(j, buf_next, buf_cur):
        idx_logits(j + 1, buf_next)
        idx_scores(j, buf_cur)

    def idx_double_step(t, carry):
        idx_advance(2 * t, s2_ref, s_ref)
        idx_advance(2 * t + 1, s_ref, s2_ref)
        return carry

    idx_logits(0, s_ref)
    n_double = (n_ch - 1) // 2
    lax.fori_loop(0, n_double, idx_double_step, 0)
    last = 2 * n_double

    @pl.when(last == n_ch - 1)
    def _():
        idx_scores(last, s_ref)

    @pl.when(last < n_ch - 1)
    def _():
        idx_advance(last, s2_ref, s_ref)
        idx_scores(last + 1, s2_ref)

    def count(pred):
        def body(j, cnt):
            hit = jnp.where(pred(sc_ref[pl.ds(k_start(j), chunk), :]), 1.0, 0.0)
            return cnt + jnp.sum(hit.reshape(-1, COUNT_CHAINS, SUBLANES, BLOCK), axis=0)
        cnt = lax.fori_loop(0, n_ch, body, jnp.zeros((COUNT_CHAINS, SUBLANES, BLOCK), F32))
        return jnp.sum(jnp.sum(cnt, axis=0), axis=0, keepdims=True)

    kf = float(topk)
    n_all = (n_ch * chunk).astype(F32)
    n_pos = count(lambda s: s >= 0.0)
    pos_ok = n_pos >= kf
    key = jnp.where(pos_ok, jnp.int32(0), jnp.int32(-2 ** 31))
    n_key = jnp.where(pos_ok, n_pos, n_all)

    def bit_body(b, carry):
        key, n_key = carry
        trial = key | lax.shift_left(jnp.int32(1), 30 - b)
        cand = _key_to_f32(trial)
        n_trial = count(lambda s: s >= cand)
        ok = n_trial >= kf
        return jnp.where(ok, trial, key), jnp.where(ok, n_trial, n_key)

    tq = t0 + lax.broadcasted_iota(jnp.int32, (1, BLOCK), 1)
    few = tq < topk - 1
    key, n_key = lax.fori_loop(0, 31, bit_body, (key, n_key))
    tau = jnp.where(few, F32_MIN, _key_to_f32(key))

    @pl.when(jnp.max(jnp.where(few, 0.0, n_key - kf)) > 0.0)
    def _():
        n_tie = kf - count(lambda s: s > tau)

        def tie_body(j, seen):
            k0 = k_start(j)
            s = sc_ref[pl.ds(k0, chunk), :]
            eq = jnp.where(s == tau, 1.0, 0.0)
            rank_eq = seen + jnp.dot(tri_ref[...], eq.astype(BF16), preferred_element_type=F32)
            drop = jnp.where(rank_eq > n_tie, eq, 0.0)
            sc_ref[pl.ds(k0, chunk), :] = jnp.where(drop > 0.0, -jnp.inf, s)
            return seen + jnp.sum(eq, axis=0, keepdims=True)

        lax.fori_loop(0, n_ch, tie_body, jnp.zeros((1, BLOCK), F32))

    pad_rows = ckv_ref.shape[-1] - rank - POS_ROWS
    q_aug = jnp.concatenate([heads(qlt_ref, rank), qpos_ref[0],
                             jnp.zeros((pad_rows, nh * BLOCK), BF16)], axis=0)
    acc_ref[...] = jnp.zeros(acc_ref.shape, F32)
    pair = 2 * BLOCK
    n_pairs = nh * BLOCK // pair
    col_slices = [slice(pp * pair, (pp + 1) * pair) for pp in range(n_pairs)]

    cat = lambda parts: jnp.concatenate(parts, axis=-1)

    def logits_stage(j, buf):
        mask = jnp.where(sc_ref[pl.ds(k_start(j), chunk), :] >= tau, 0.0, NEG)
        mask2 = jnp.concatenate([mask, mask], axis=-1)
        kc = ckv_ref[0, pl.ds(k_start(j), chunk), :]
        cmax = []
        for cols in col_slices:
            s = jnp.dot(kc, q_aug[:, cols], preferred_element_type=F32) + mask2
            buf[:, cols] = s
            cmax.append(jnp.max(s, axis=0, keepdims=True))
        return cat(cmax)

    def value_stage(j, buf, m_use, alpha):
        kct = ckvt_ref[0, :, pl.ds(k_start(j), chunk)]
        sums = []
        for cols in col_slices:
            p = jnp.exp2(buf[:, cols] - m_use[:, cols])
            pv = jnp.dot(kct, p.astype(BF16), preferred_element_type=F32)
            acc_ref[:, cols] = alpha[:, cols] * acc_ref[:, cols] + pv
            sums.append(jnp.sum(p, axis=0, keepdims=True))
        return cat(sums)

    def finish(j, buf_cur, carry):
        m_cur, alpha_cur, l_prev = carry
        return alpha_cur * l_prev + value_stage(j, buf_cur, m_cur, alpha_cur)

    def advance(j, buf_next, buf_cur, carry):
        m_cur = carry[0]
        m_next = jnp.maximum(m_cur, logits_stage(j + 1, buf_next))
        return m_next, jnp.exp2(m_cur - m_next), finish(j, buf_cur, carry)

    def double_step(t, carry):
        carry = advance(2 * t, s2_ref, s_ref, carry)
        return advance(2 * t + 1, s_ref, s2_ref, carry)

    zeros_row = jnp.zeros((1, nh * BLOCK), F32)
    n_double = (n_ch - 1) // 2
    carry = lax.fori_loop(0, n_double, double_step, (logits_stage(0, s_ref), zeros_row, zeros_row))
    last = 2 * n_double

    @pl.when(last == n_ch - 1)
    def _():
        l_ref[...] = finish(last, s_ref, carry)

    @pl.when(last < n_ch - 1)
    def _():
        l_ref[...] = finish(last + 1, s2_ref, advance(last, s2_ref, s_ref, carry))

    l_fin = l_ref[...]

    o_lat_t = acc_ref[...] / l_fin
    outs = [jnp.dot(o_lat_t[:, hh * BLOCK:(hh + 1) * BLOCK].T.astype(BF16), wuv_ref[hh],
                    preferred_element_type=F32) for hh in range(nh)]
    o_ref[0] = jnp.concatenate(outs, axis=-1).astype(BF16)


def _alibi_query_table(slopes, n_blocks):
    sl = np.asarray(slopes, np.float64) * LOG2E
    t0 = np.arange(n_blocks, dtype=np.float64)[:, None] * BLOCK
    coef = [np.broadcast_to(sl * POS_RADIX, (n_blocks, sl.size)),
            np.broadcast_to(sl, (n_blocks, sl.size)),
            -sl[None, :] * t0]
    rows = []
    for cf in coef:
        hi = jnp.asarray(cf, F32).astype(BF16)
        lo = (jnp.asarray(cf, F32) - hi.astype(F32)).astype(BF16)
        rows += [hi, lo]
    tab = jnp.stack(rows + [jnp.zeros_like(rows[0])] * (POS_ROWS - len(rows)), axis=1)
    return jnp.repeat(tab, BLOCK, axis=-1)


def _dsa_call(slopes, qit, wit, ki, qlt, ckv, ckvt, w_uv, *, chunk=256):
    bsz, seq, ckv_w = ckv.shape
    rank = ckvt.shape[1]
    nh, _, hd = w_uv.shape
    topk = min(TOPK_MAX, seq // 4)
    assert seq % chunk == 0 and seq // POS_RADIX <= 256
    tri = jnp.asarray(np.tril(np.ones((chunk, chunk), np.float32)), BF16)
    qpos = _alibi_query_table(slopes, seq // BLOCK)
    return pl.pallas_call(
        functools.partial(_dsa_kernel, topk=topk, chunk=chunk),
        grid=(bsz, seq // BLOCK),
        in_specs=[pl.BlockSpec((1, IDX_HEADS * IDX_DIM, BLOCK), lambda b, i: (b, 0, i)),
                  pl.BlockSpec((1, IDX_HEADS, BLOCK), lambda b, i: (b, 0, i)),
                  pl.BlockSpec((1, seq, IDX_DIM), lambda b, i: (b, 0, 0)),
                  pl.BlockSpec((1, nh * rank, BLOCK), lambda b, i: (b, 0, i)),
                  pl.BlockSpec((1, POS_ROWS, nh * BLOCK), lambda b, i: (i, 0, 0)),
                  pl.BlockSpec((1, seq, ckv_w), lambda b, i: (b, 0, 0)),
                  pl.BlockSpec((1, rank, seq), lambda b, i: (b, 0, 0)),
                  _const_spec((nh, rank, hd)),
                  _const_spec((chunk, chunk))],
        out_specs=pl.BlockSpec((1, BLOCK, nh * hd), lambda b, i: (b, i, 0)),
        out_shape=jax.ShapeDtypeStruct((bsz, seq, nh * hd), BF16),
        scratch_shapes=[pltpu.VMEM((seq, BLOCK), F32),
                        pltpu.VMEM((rank, nh * BLOCK), F32),
                        pltpu.VMEM((chunk, nh * BLOCK), F32),
                        pltpu.VMEM((chunk, nh * BLOCK), F32),
                        pltpu.VMEM((1, nh * BLOCK), F32)],
        compiler_params=_cparams(("parallel", "arbitrary")),
        name="sparse_attn",
    )(qit, wit, ki, qlt, qpos, ckv, ckvt, w_uv, tri)


def _out_kernel(x_ref, oa_ref, ob_ref, mod_ref, wo_ref, g2_ref, wgu_ref, wd_ref, gf_ref, o_ref,
                *, d_ff, ff_chunks, final_norm):
    mixed = jnp.concatenate([oa_ref[0], ob_ref[0]], axis=-1)
    x1 = x_ref[0] + mod_ref[0, 2:3, :] * jnp.dot(mixed, wo_ref[...], preferred_element_type=F32)
    h2 = _rms(x1) * g2_ref[...]
    h2 = (h2 * (1.0 + mod_ref[0, 4:5, :]) + mod_ref[0, 3:4, :]).astype(BF16)
    fc = d_ff // ff_chunks
    ffn = jnp.zeros(x1.shape, F32)
    for cc in range(ff_chunks):
        gate = jnp.dot(h2, wgu_ref[:, cc * fc:(cc + 1) * fc], preferred_element_type=F32)
        up = jnp.dot(h2, wgu_ref[:, d_ff + cc * fc:d_ff + (cc + 1) * fc], preferred_element_type=F32)
        act = gate * (1.0 / (1.0 + jnp.exp(-gate))) * up
        ffn = ffn + jnp.dot(act.astype(BF16), wd_ref[cc * fc:(cc + 1) * fc, :], preferred_element_type=F32)
    x2 = x1 + mod_ref[0, 5:6, :] * ffn
    if final_norm:
        x2 = _rms(x2) * gf_ref[...]
    o_ref[0] = x2


def _out_call(x, oa, ob, mod, w_out, g_ffn, w_gu, w_down, g_final, *, tm, final_norm):
    bsz, seq, d = x.shape
    d_ff = w_down.shape[0]
    ff_chunks = 2 if d_ff % (2 * LANES) == 0 else 1
    row = lambda w: pl.BlockSpec((1, tm, w), lambda b, i: (b, i, 0))
    return pl.pallas_call(
        functools.partial(_out_kernel, d_ff=d_ff, ff_chunks=ff_chunks, final_norm=final_norm),
        grid=(bsz, seq // tm),
        in_specs=[row(d), row(oa.shape[-1]), row(ob.shape[-1]),
                  pl.BlockSpec((1, 6, d), lambda b, i: (b, 0, 0)),
                  _const_spec((d, d)), _const_spec((1, d)),
                  _const_spec((d, 2 * d_ff)), _const_spec((d_ff, d)), _const_spec((1, d))],
        out_specs=row(d),
        out_shape=jax.ShapeDtypeStruct((bsz, seq, d), F32),
        compiler_params=_cparams(("parallel", "parallel")),
        name="out_proj_ffn",
    )(x, oa, ob, mod, w_out, g_ffn, w_gu, w_down, g_final)


def _alibi_slopes(n_heads):
    s = 2.0 ** (-8.0 * (np.arange(n_heads, dtype=np.float32) + 1.0) / n_heads)
    return s[0::2], s[1::2]


def kernel(x, c, w_ada, b_ada, g_attn, w_in, kv_norm_g, w_uk, w_uv, w_out, g_ffn, w_gu, w_down, g_final):
    bsz, seq, d = x.shape
    depth = w_ada.shape[0]
    nh = w_uk.shape[1]
    slopes_a, slopes_b = _alibi_slopes(2 * nh)
    d_in = w_in.shape[-1]
    n_pad = -(-d_in // LANES) * LANES
    tm = 512 if seq % 512 == 0 else BLOCK
    for l in range(depth):
        mod = _mod_call(c, w_ada[l], b_ada[l]).reshape(bsz, 6, d)
        w_in_p = jnp.pad(w_in[l], ((0, 0), (0, n_pad - d_in))).astype(BF16)
        qa, ka, va, qlt, ckv, ckvt, qit, ki, wit = _in_call(
            x, mod, g_attn[l].reshape(1, d), w_in_p, w_uk[l].astype(BF16),
            kv_norm_g[l].reshape(1, -1), nh=nh, tm=tm)
        out_a = _dil_call(slopes_a, qa, ka, va)
        out_b = _dsa_call(slopes_b, qit, wit, ki, qlt, ckv, ckvt, w_uv[l].astype(BF16))
        x = _out_call(x, out_a, out_b, mod, w_out[l].astype(BF16), g_ffn[l].reshape(1, d),
                      w_gu[l].astype(BF16), w_down[l].astype(BF16), g_final.reshape(1, d),
                      tm=tm, final_norm=(l == depth - 1))
    return x
```

```python
import functools

import numpy as np
import jax
import jax.numpy as jnp
from jax import lax
from jax.experimental import pallas as pl
from jax.experimental.pallas import tpu as pltpu

F32 = jnp.float32
BF16 = jnp.bfloat16

HEAD_DIM = 64
BLOCK = 128
DILATED_CONFIGS = ((128, 1), (512, 4), (2048, 16))
IDX_HEADS = 8
IDX_DIM = 64
TOPK_MAX = 256
EPS = 1e-6
NEG = -1e30
F32_MIN = float(np.finfo(np.float32).min)
LOG2E = float(np.log2(np.e))
POS_RADIX = 64
POS_ROWS = 16
LANES = 128
SUBLANES = 8
COUNT_CHAINS = 8
VMEM_LIMIT = 56 * 1024 * 1024

_NT = (((1,), (1,)), ((), ()))


def _cparams(sem, flags=None):
    return pltpu.CompilerParams(dimension_semantics=sem, vmem_limit_bytes=VMEM_LIMIT, flags=flags)


def _const_spec(shape):
    nd = len(shape)
    return pl.BlockSpec(shape, lambda *_: (0,) * nd, pipeline_mode=pl.Buffered(1))


def _mod_kernel(c_ref, w_ref, b_ref, o_ref):
    c = c_ref[...]
    c_act = c * (1.0 / (1.0 + jnp.exp(-c)))
    o_ref[...] = jnp.dot(c_act, w_ref[...], preferred_element_type=F32,
                         precision=lax.Precision.HIGHEST) + b_ref[...]


def _mod_call(c, w_ada, b_ada):
    bsz, d = c.shape
    n = w_ada.shape[1]
    tn = 1024
    return pl.pallas_call(
        _mod_kernel,
        grid=(n // tn,),
        in_specs=[pl.BlockSpec((bsz, d), lambda j: (0, 0)),
                  pl.BlockSpec((d, tn), lambda j: (0, j)),
                  pl.BlockSpec((1, tn), lambda j: (0, j))],
        out_specs=pl.BlockSpec((bsz, tn), lambda j: (0, j)),
        out_shape=jax.ShapeDtypeStruct((bsz, n), F32),
        compiler_params=_cparams(("arbitrary",)),
        name="adaln_mod",
    )(c, w_ada, b_ada.reshape(1, n))


def _rms(x):
    return x * lax.rsqrt(jnp.mean(x * x, axis=-1, keepdims=True) + EPS)


def _in_kernel(x_ref, mod_ref, g_ref, w_ref, wuk_ref, kvg_ref,
               qa_ref, ka_ref, va_ref, qlt_ref, ckv_ref, ckvt_ref, qit_ref, ki_ref, wit_ref,
               *, nh, idx_scale):
    hd = HEAD_DIM
    h = _rms(x_ref[0]) * g_ref[...]
    h = h * (1.0 + mod_ref[0, 1:2, :]) + mod_ref[0, 0:1, :]
    proj = jnp.dot(h.astype(BF16), w_ref[...], preferred_element_type=F32)
    wa = nh * hd
    rank = ckvt_ref.shape[1]
    tm = x_ref.shape[1]
    o_qb, o_ckv = 3 * wa, 4 * wa
    o_qi = o_ckv + rank
    o_ki = o_qi + IDX_HEADS * IDX_DIM
    qa_ref[0] = proj[:, 0:wa] * (hd ** -0.5 * LOG2E)
    ka_ref[0] = proj[:, wa:2 * wa]
    va_ref[0] = proj[:, 2 * wa:3 * wa]
    for hh in range(nh):
        qb_h = proj[:, o_qb + hh * hd:o_qb + (hh + 1) * hd].astype(BF16)
        q_lat = jnp.dot(qb_h, wuk_ref[hh], preferred_element_type=F32) * (hd ** -0.5 * LOG2E)
        qlt_ref[0, hh * rank:(hh + 1) * rank, :] = q_lat.T.astype(BF16)
    ckv = (_rms(proj[:, o_ckv:o_ckv + rank]) * kvg_ref[...])
    ckvt_ref[0] = ckv.T.astype(BF16)
    t = pl.program_id(1) * tm + lax.broadcasted_iota(jnp.int32, (tm, LANES), 0)
    lane = lax.broadcasted_iota(jnp.int32, (tm, LANES), 1)
    pos = jnp.where(lane < 2, t // POS_RADIX, jnp.where(lane < 4, t % POS_RADIX, jnp.where(lane < 6, 1, 0)))
    ckv_ref[0] = jnp.concatenate([ckv, pos.astype(F32)], axis=-1).astype(BF16)
    for pp in range(IDX_HEADS * IDX_DIM // LANES):
        qit_ref[0, pp * LANES:(pp + 1) * LANES, :] = proj[:, o_qi + pp * LANES:o_qi + (pp + 1) * LANES].T.astype(BF16)
    kiw = proj[:, o_ki:o_ki + LANES]
    ki_ref[0] = kiw[:, :IDX_DIM].astype(BF16)
    wit_ref[0] = kiw.T[IDX_DIM:IDX_DIM + IDX_HEADS, :] * idx_scale


def _in_call(x, mod, g, w_in_p, w_uk, kvg, *, nh, tm):
    bsz, seq, d = x.shape
    n_pad = w_in_p.shape[1]
    rank = w_uk.shape[-1]
    hd = HEAD_DIM
    wa = nh * hd
    idx_scale = float((IDX_HEADS * IDX_DIM) ** -0.5)
    row_spec = lambda w: pl.BlockSpec((1, tm, w), lambda b, i: (b, i, 0))
    col_spec = lambda r: pl.BlockSpec((1, r, tm), lambda b, i: (b, 0, i))
    return pl.pallas_call(
        functools.partial(_in_kernel, nh=nh, idx_scale=idx_scale),
        grid=(bsz, seq // tm),
        in_specs=[pl.BlockSpec((1, tm, d), lambda b, i: (b, i, 0)),
                  pl.BlockSpec((1, 6, d), lambda b, i: (b, 0, 0)),
                  _const_spec((1, d)),
                  _const_spec((d, n_pad)),
                  _const_spec((nh, hd, rank)),
                  _const_spec((1, rank))],
        out_specs=[row_spec(wa), row_spec(wa), row_spec(wa),
                   col_spec(nh * rank), row_spec(rank + LANES), col_spec(rank),
                   col_spec(IDX_HEADS * IDX_DIM), row_spec(IDX_DIM), col_spec(IDX_HEADS)],
        out_shape=[jax.ShapeDtypeStruct((bsz, seq, wa), F32),
                   jax.ShapeDtypeStruct((bsz, seq, wa), F32),
                   jax.ShapeDtypeStruct((bsz, seq, wa), F32),
                   jax.ShapeDtypeStruct((bsz, nh * rank, seq), BF16),
                   jax.ShapeDtypeStruct((bsz, seq, rank + LANES), BF16),
                   jax.ShapeDtypeStruct((bsz, rank, seq), BF16),
                   jax.ShapeDtypeStruct((bsz, IDX_HEADS * IDX_DIM, seq), BF16),
                   jax.ShapeDtypeStruct((bsz, seq, IDX_DIM), BF16),
                   jax.ShapeDtypeStruct((bsz, IDX_HEADS, seq), F32)],
        compiler_params=_cparams(("parallel", "parallel")),
        name="in_proj",
    )(x, mod, g, w_in_p, w_uk, kvg)


def _band_tables(tq, kw, w_res):
    r = np.arange(tq)[None, :]
    c = np.arange(kw)[:, None]
    clean = lambda d: np.where((d >= 0) & (d <= w_res), d, -1).astype(np.float32)
    return jnp.asarray(np.stack([clean(r - c), clean(r - c + w_res)]))


def _dil_kernel(slope_ref, q_ref, k_ref, v_ref, *refs, w_res, plan):
    n_tab = 1 + max(t for _, _, t in plan)
    dtab_refs = refs[:n_tab]
    o_ref, m_ref, l_ref, acc_ref, bias_ref, s_ref, mloc_ref = refs[n_tab:]
    seq = q_ref.shape[1]
    hp = pl.program_id(1)
    feat = lax.broadcasted_iota(jnp.int32, (LANES, 1), 0)
    first = feat < HEAD_DIM
    keep = (first, jnp.logical_not(first))
    m_ref[...] = jnp.full(m_ref.shape, NEG, F32)
    l_ref[...] = jnp.zeros(l_ref.shape, F32)
    acc_ref[...] = jnp.zeros(acc_ref.shape, F32)

    for dil, tq, tab in plan:
        dtab = dtab_refs[tab]
        kw = dtab.shape[1]
        nt = seq // dil // tq
        for hh in range(2):
            neg_slope = -slope_ref[hp * 2 + hh] * (float(dil) * LOG2E)
            for off in range(2):
                bias_ref[off, hh, :kw, :tq] = jnp.where(dtab[off] >= 0.0, neg_slope * dtab[off], NEG)

        def rows(start, size, dil=dil):
            return pl.ds(start, size) if dil == 1 else pl.ds(start, size, stride=dil)

        def coords(w, dil=dil, tq=tq, nt=nt):
            r = w // nt
            jt = w - r * nt
            q0 = jt * (tq * dil) + r
            k0 = jnp.maximum(jt * tq - w_res, 0) * dil + r
            return q0, k0, jnp.minimum(jt, 1)

        def logits_stage(w, rows=rows, coords=coords, tq=tq, kw=kw):
            q0, k0, off = coords(w)
            q_t = q_ref[0, rows(q0, tq), :].T
            k2 = k_ref[0, rows(k0, kw), :].astype(BF16)
            for hh in range(2):
                qh = jnp.where(keep[hh], q_t, 0.0).astype(BF16)
                s = jnp.dot(k2, qh, preferred_element_type=F32) + bias_ref[off, hh, :kw, :tq]
                s_ref[hh, :kw, :tq] = s
                mloc_ref[hh, :, :tq] = jnp.max(s, axis=0, keepdims=True)

        def value_stage(w, rows=rows, coords=coords, tq=tq, kw=kw):
            q0, k0, _ = coords(w)
            v_t = v_ref[0, rows(k0, kw), :].T.astype(BF16)
            ms = [mloc_ref[hh, :, :tq] for hh in range(2)]
            ps = [jnp.exp2(s_ref[hh, :kw, :tq] - ms[hh]) for hh in range(2)]
            ls = [jnp.sum(p, axis=0, keepdims=True) for p in ps]
            outs = [jnp.dot(v_t, p.astype(BF16), preferred_element_type=F32) for p in ps]
            pick = lambda pair: jnp.where(first, pair[0], pair[1]).T
            spread = lambda pair: [jnp.broadcast_to(x, (LANES, tq)) for x in pair]
            m_t, l_t, o_t = pick(spread(ms)), pick(spread(ls)), pick(outs)
            q_rows = rows(q0, tq)
            m_old = m_ref[q_rows, :]
            m_new = jnp.maximum(m_old, m_t)
            a_old = jnp.exp2(m_old - m_new)
            a_new = jnp.exp2(m_t - m_new)
            acc_ref[q_rows, :] = a_old * acc_ref[q_rows, :] + a_new * o_t
            l_ref[q_rows, :] = a_old * l_ref[q_rows, :] + a_new * l_t
            m_ref[q_rows, :] = m_new

        def pipe_body(w, carry, logits_stage=logits_stage, value_stage=value_stage):
            value_stage(w - 1)
            logits_stage(w)
            return carry

        n_tiles = dil * nt
        logits_stage(0)
        lax.fori_loop(1, n_tiles, pipe_body, 0)
        value_stage(n_tiles - 1)

    o_ref[0] = (acc_ref[...] / l_ref[...]).astype(BF16)


def _dil_call(slopes, qa, ka, va):
    bsz, seq, wa = qa.shape
    w_res = DILATED_CONFIGS[0][0] // DILATED_CONFIGS[0][1]
    assert all(w // d == w_res for w, d in DILATED_CONFIGS)
    tile_shapes, plan = [], []
    for _, dil in DILATED_CONFIGS:
        n = seq // dil
        tq = min(256, n)
        kw = tq if n == tq else tq + w_res
        assert n % tq == 0 and tq >= w_res and kw <= n
        if (tq, kw) not in tile_shapes:
            tile_shapes.append((tq, kw))
        plan.append((dil, tq, tile_shapes.index((tq, kw))))
    tables = [_band_tables(tq, kw, w_res) for tq, kw in tile_shapes]
    tq_max = max(tq for tq, _ in tile_shapes)
    kw_max = max(kw for _, kw in tile_shapes)
    blk = pl.BlockSpec((1, seq, LANES), lambda b, h: (b, 0, h))
    return pl.pallas_call(
        functools.partial(_dil_kernel, w_res=w_res, plan=tuple(plan)),
        grid=(bsz, wa // LANES),
        in_specs=[pl.BlockSpec(memory_space=pltpu.SMEM), blk, blk, blk] + [_const_spec(t.shape) for t in tables],
        out_specs=blk,
        out_shape=jax.ShapeDtypeStruct((bsz, seq, wa), BF16),
        scratch_shapes=[pltpu.VMEM((seq, LANES), F32), pltpu.VMEM((seq, LANES), F32),
                        pltpu.VMEM((seq, LANES), F32),
                        pltpu.VMEM((2, 2, kw_max, tq_max), F32),
                        pltpu.VMEM((2, kw_max, tq_max), F32),
                        pltpu.VMEM((2, 1, tq_max), F32)],
        compiler_params=_cparams(("parallel", "parallel")),
        name="dilated_attn",
    )(jnp.asarray(slopes, F32), qa, ka, va, *tables)


def _key_to_f32(key):
    bits = key ^ (lax.shift_right_arithmetic(key, 31) & jnp.int32(0x7FFFFFFF))
    return lax.bitcast_convert_type(bits, F32)


def _dsa_kernel(qit_ref, wit_ref, ki_ref, qlt_ref, qpos_ref, ckv_ref, ckvt_ref, wuv_ref, tri_ref, o_ref,
                sc_ref, acc_ref, s_ref, s2_ref, l_ref, *, topk, chunk):
    i = pl.program_id(1)
    t0 = i * BLOCK
    rank = ckvt_ref.shape[1]
    nh = qlt_ref.shape[1] // rank
    n_ch = (t0 + BLOCK + chunk - 1) // chunk
    kk = lax.broadcasted_iota(jnp.int32, (chunk, BLOCK), 0)
    qq = lax.broadcasted_iota(jnp.int32, (chunk, BLOCK), 1)
    rel = qq - kk
    k_start = lambda j: pl.multiple_of(j * chunk, chunk)
    heads = lambda ref, width: jnp.concatenate(
        [ref[0, hh * width:(hh + 1) * width, :] for hh in range(ref.shape[1] // width)], axis=-1)

    qit = heads(qit_ref, IDX_DIM)
    wt = wit_ref[0]

    idx_pair = 2 * BLOCK
    idx_cols = [slice(pp * idx_pair, (pp + 1) * idx_pair) for pp in range(IDX_HEADS * BLOCK // idx_pair)]

    def idx_logits(j, buf):
        kc = ki_ref[0, pl.ds(k_start(j), chunk), :]
        for cols in idx_cols:
            buf[:, cols] = jnp.dot(kc, qit[:, cols], preferred_element_type=F32)

    def idx_scores(j, buf):
        k0 = k_start(j)
        score = jnp.zeros((chunk, BLOCK), F32)
        for pp, cols in enumerate(idx_cols):
            lg = buf[:, cols]
            score = score + (wt[2 * pp:2 * pp + 1, :] * jnp.maximum(lg[:, :BLOCK], 0.0)
                             + wt[2 * pp + 1:2 * pp + 2, :] * jnp.maximum(lg[:, BLOCK:], 0.0))
        score = jnp.where(score == 0.0, 0.0, score)
        sc_ref[pl.ds(k0, chunk), :] = jnp.where(rel + (t0 - k0) >= 0, score, -jnp.inf)

    def idx_advance(j, buf_next, buf_cur):
        idx_logits(j + 1, buf_next)
        idx_scores(j, buf_cur)

    def idx_double_step(t, carry):
        idx_advance(2 * t, s2_ref, s_ref)
        idx_advance(2 * t + 1, s_ref, s2_ref)
        return carry

    idx_logits(0, s_ref)
    n_double = (n_ch - 1) // 2
    lax.fori_loop(0, n_double, idx_double_step, 0)
    last = 2 * n_double

    @pl.when(last == n_ch - 1)
    def _():
        idx_scores(last, s_ref)

    @pl.when(last < n_ch - 1)
    def _():
        idx_advance(last, s2_ref, s_ref)
        idx_scores(last + 1, s2_ref)

    def count(pred):
        def body(j, cnt):
            hit = jnp.where(pred(sc_ref[pl.ds(k_start(j), chunk), :]), 1.0, 0.0)
            return cnt + jnp.sum(hit.reshape(-1, COUNT_CHAINS, SUBLANES, BLOCK), axis=0)
        cnt = lax.fori_loop(0, n_ch, body, jnp.zeros((COUNT_CHAINS, SUBLANES, BLOCK), F32))
        return jnp.sum(jnp.sum(cnt, axis=0), axis=0, keepdims=True)

    kf = float(topk)
    n_all = (n_ch * chunk).astype(F32)
    n_pos = count(lambda s: s >= 0.0)
    pos_ok = n_pos >= kf
    key = jnp.where(pos_ok, jnp.int32(0), jnp.int32(-2 ** 31))
    n_key = jnp.where(pos_ok, n_pos, n_all)

    def bit_body(b, carry):
        key, n_key = carry
        trial = key | lax.shift_left(jnp.int32(1), 30 - b)
        cand = _key_to_f32(trial)
        n_trial = count(lambda s: s >= cand)
        ok = n_trial >= kf
        return jnp.where(ok, trial, key), jnp.where(ok, n_trial, n_key)

    tq = t0 + lax.broadcasted_iota(jnp.int32, (1, BLOCK), 1)
    few = tq < topk - 1
    key, n_key = lax.fori_loop(0, 31, bit_body, (key, n_key))
    tau = jnp.where(few, F32_MIN, _key_to_f32(key))

    @pl.when(jnp.max(jnp.where(few, 0.0, n_key - kf)) > 0.0)
    def _():
        n_tie = kf - count(lambda s: s > tau)

        def tie_body(j, seen):
            k0 = k_start(j)
            s = sc_ref[pl.ds(k0, chunk), :]
            eq = jnp.where(s == tau, 1.0, 0.0)
            rank_eq = seen + jnp.dot(tri_ref[...], eq.astype(BF16), preferred_element_type=F32)
            drop = jnp.where(rank_eq > n_tie, eq, 0.0)
            sc_ref[pl.ds(k0, chunk), :] = jnp.where(drop > 0.0, -jnp.inf, s)
            return seen + jnp.sum(eq, axis=0, keepdims=True)

        lax.fori_loop(0, n_ch, tie_body, jnp.zeros((1, BLOCK), F32))

    pad_rows = ckv_ref.shape[-1] - rank - POS_ROWS
    q_aug = jnp.concatenate([heads(qlt_ref, rank), qpos_ref[0],
                             jnp.zeros((pad_rows, nh * BLOCK), BF16)], axis=0)
    acc_ref[...] = jnp.zeros(acc_ref.shape, F32)
    pair = 2 * BLOCK
    n_pairs = nh * BLOCK // pair
    col_slices = [slice(pp * pair, (pp + 1) * pair) for pp in range(n_pairs)]

    cat = lambda parts: jnp.concatenate(parts, axis=-1)

    def logits_stage(j, buf):
        mask = jnp.where(sc_ref[pl.ds(k_start(j), chunk), :] >= tau, 0.0, NEG)
        mask2 = jnp.concatenate([mask, mask], axis=-1)
        kc = ckv_ref[0, pl.ds(k_start(j), chunk), :]
        cmax = []
        for cols in col_slices:
            s = jnp.dot(kc, q_aug[:, cols], preferred_element_type=F32) + mask2
            buf[:, cols] = s
            cmax.append(jnp.max(s, axis=0, keepdims=True))
        return cat(cmax)

    ones_rows = jnp.ones((2 * SUBLANES, chunk), BF16)

    def value_stage(j, buf, m_use, alpha):
        kct = jnp.concatenate([ckvt_ref[0, :, pl.ds(k_start(j), chunk)], ones_rows], axis=0)
        sums = []
        for cols in col_slices:
            p = jnp.exp2(buf[:, cols] - m_use[:, cols])
            pv = jnp.dot(kct, p.astype(BF16), preferred_element_type=F32)
            acc_ref[:, cols] = alpha[:, cols] * acc_ref[:, cols] + pv[:rank]
            sums.append(pv[rank:rank + 1])
        return cat(sums)

    def finish(j, buf_cur, carry):
        m_cur, alpha_cur, l_prev = carry
        return alpha_cur * l_prev + value_stage(j, buf_cur, m_cur, alpha_cur)

    def advance(j, buf_next, buf_cur, carry):
        m_cur = carry[0]
        m_next = jnp.maximum(m_cur, logits_stage(j + 1, buf_next))
        return m_next, jnp.exp2(m_cur - m_next), finish(j, buf_cur, carry)

    def double_step(t, carry):
        carry = advance(2 * t, s2_ref, s_ref, carry)
        return advance(2 * t + 1, s_ref, s2_ref, carry)

    zeros_row = jnp.zeros((1, nh * BLOCK), F32)
    n_double = (n_ch - 1) // 2
    carry = lax.fori_loop(0, n_double, double_step, (logits_stage(0, s_ref), zeros_row, zeros_row))
    last = 2 * n_double

    @pl.when(last == n_ch - 1)
    def _():
        l_ref[...] = finish(last, s_ref, carry)

    @pl.when(last < n_ch - 1)
    def _():
        l_ref[...] = finish(last + 1, s2_ref, advance(last, s2_ref, s_ref, carry))

    l_fin = l_ref[...]

    o_lat_t = acc_ref[...] / l_fin
    outs = [jnp.dot(o_lat_t[:, hh * BLOCK:(hh + 1) * BLOCK].T.astype(BF16), wuv_ref[hh],
                    preferred_element_type=F32) for hh in range(nh)]
    o_ref[0] = jnp.concatenate(outs, axis=-1).astype(BF16)


def _alibi_query_table(slopes, n_blocks):
    sl = np.asarray(slopes, np.float64) * LOG2E
    t0 = np.arange(n_blocks, dtype=np.float64)[:, None] * BLOCK
    coef = [np.broadcast_to(sl * POS_RADIX, (n_blocks, sl.size)),
            np.broadcast_to(sl, (n_blocks, sl.size)),
            -sl[None, :] * t0]
    rows = []
    for cf in coef:
        hi = jnp.asarray(cf, F32).astype(BF16)
        lo = (jnp.asarray(cf, F32) - hi.astype(F32)).astype(BF16)
        rows += [hi, lo]
    tab = jnp.stack(rows + [jnp.zeros_like(rows[0])] * (POS_ROWS - len(rows)), axis=1)
    return jnp.repeat(tab, BLOCK, axis=-1)


def _dsa_call(slopes, qit, wit, ki, qlt, ckv, ckvt, w_uv, *, chunk=256):
    bsz, seq, ckv_w = ckv.shape
    rank = ckvt.shape[1]
    nh, _, hd = w_uv.shape
    topk = min(TOPK_MAX, seq // 4)
    assert seq % chunk == 0 and seq // POS_RADIX <= 256
    tri = jnp.asarray(np.tril(np.ones((chunk, chunk), np.float32)), BF16)
    qpos = _alibi_query_table(slopes, seq // BLOCK)
    return pl.pallas_call(
        functools.partial(_dsa_kernel, topk=topk, chunk=chunk),
        grid=(bsz, seq // BLOCK),
        in_specs=[pl.BlockSpec((1, IDX_HEADS * IDX_DIM, BLOCK), lambda b, i: (b, 0, i)),
                  pl.BlockSpec((1, IDX_HEADS, BLOCK), lambda b, i: (b, 0, i)),
                  pl.BlockSpec((1, seq, IDX_DIM), lambda b, i: (b, 0, 0)),
                  pl.BlockSpec((1, nh * rank, BLOCK), lambda b, i: (b, 0, i)),
                  pl.BlockSpec((1, POS_ROWS, nh * BLOCK), lambda b, i: (i, 0, 0)),
                  pl.BlockSpec((1, seq, ckv_w), lambda b, i: (b, 0, 0)),
                  pl.BlockSpec((1, rank, seq), lambda b, i: (b, 0, 0)),
                  _const_spec((nh, rank, hd)),
                  _const_spec((chunk, chunk))],
        out_specs=pl.BlockSpec((1, BLOCK, nh * hd), lambda b, i: (b, i, 0)),
        out_shape=jax.ShapeDtypeStruct((bsz, seq, nh * hd), BF16),
        scratch_shapes=[pltpu.VMEM((seq, BLOCK), F32),
                        pltpu.VMEM((rank, nh * BLOCK), F32),
                        pltpu.VMEM((chunk, nh * BLOCK), F32),
                        pltpu.VMEM((chunk, nh * BLOCK), F32),
                        pltpu.VMEM((1, nh * BLOCK), F32)],
        compiler_params=_cparams(("parallel", "arbitrary")),
        name="sparse_attn",
    )(qit, wit, ki, qlt, qpos, ckv, ckvt, w_uv, tri)


def _out_kernel(x_ref, oa_ref, ob_ref, mod_ref, wo_ref, g2_ref, wgu_ref, wd_ref, gf_ref, o_ref,
                *, d_ff, ff_chunks, final_norm):
    mixed = jnp.concatenate([oa_ref[0], ob_ref[0]], axis=-1)
    x1 = x_ref[0] + mod_ref[0, 2:3, :] * jnp.dot(mixed, wo_ref[...], preferred_element_type=F32)
    h2 = _rms(x1) * g2_ref[...]
    h2 = (h2 * (1.0 + mod_ref[0, 4:5, :]) + mod_ref[0, 3:4, :]).astype(BF16)
    fc = d_ff // ff_chunks
    ffn = jnp.zeros(x1.shape, F32)
    for cc in range(ff_chunks):
        gate = jnp.dot(h2, wgu_ref[:, cc * fc:(cc + 1) * fc], preferred_element_type=F32)
        up = jnp.dot(h2, wgu_ref[:, d_ff + cc * fc:d_ff + (cc + 1) * fc], preferred_element_type=F32)
        act = gate * (1.0 / (1.0 + jnp.exp(-gate))) * up
        ffn = ffn + jnp.dot(act.astype(BF16), wd_ref[cc * fc:(cc + 1) * fc, :], preferred_element_type=F32)
    x2 = x1 + mod_ref[0, 5:6, :] * ffn
    if final_norm:
        x2 = _rms(x2) * gf_ref[...]
    o_ref[0] = x2


def _out_call(x, oa, ob, mod, w_out, g_ffn, w_gu, w_down, g_final, *, tm, final_norm):
    bsz, seq, d = x.shape
    d_ff = w_down.shape[0]
    ff_chunks = 2 if d_ff % (2 * LANES) == 0 else 1
    row = lambda w: pl.BlockSpec((1, tm, w), lambda b, i: (b, i, 0))
    return pl.pallas_call(
        functools.partial(_out_kernel, d_ff=d_ff, ff_chunks=ff_chunks, final_norm=final_norm),
        grid=(bsz, seq // tm),
        in_specs=[row(d), row(oa.shape[-1]), row(ob.shape[-1]),
                  pl.BlockSpec((1, 6, d), lambda b, i: (b, 0, 0)),
                  _const_spec((d, d)), _const_spec((1, d)),
                  _const_spec((d, 2 * d_ff)), _const_spec((d_ff, d)), _const_spec((1, d))],
        out_specs=row(d),
        out_shape=jax.ShapeDtypeStruct((bsz, seq, d), F32),
        compiler_params=_cparams(("parallel", "parallel")),
        name="out_proj_ffn",
    )(x, oa, ob, mod, w_out, g_ffn, w_gu, w_down, g_final)


def _alibi_slopes(n_heads):
    s = 2.0 ** (-8.0 * (np.arange(n_heads, dtype=np.float32) + 1.0) / n_heads)
    return s[0::2], s[1::2]


def kernel(x, c, w_ada, b_ada, g_attn, w_in, kv_norm_g, w_uk, w_uv, w_out, g_ffn, w_gu, w_down, g_final):
    bsz, seq, d = x.shape
    depth = w_ada.shape[0]
    nh = w_uk.shape[1]
    slopes_a, slopes_b = _alibi_slopes(2 * nh)
    d_in = w_in.shape[-1]
    n_pad = -(-d_in // LANES) * LANES
    tm = 512 if seq % 512 == 0 else BLOCK
    for l in range(depth):
        mod = _mod_call(c, w_ada[l], b_ada[l]).reshape(bsz, 6, d)
        w_in_p = jnp.pad(w_in[l], ((0, 0), (0, n_pad - d_in))).astype(BF16)
        qa, ka, va, qlt, ckv, ckvt, qit, ki, wit = _in_call(
            x, mod, g_attn[l].reshape(1, d), w_in_p, w_uk[l].astype(BF16),
            kv_norm_g[l].reshape(1, -1), nh=nh, tm=tm)
        out_a = _dil_call(slopes_a, qa, ka, va)
        out_b = _dsa_call(slopes_b, qit, wit, ki, qlt, ckv, ckvt, w_uv[l].astype(BF16))
        x = _out_call(x, out_a, out_b, mod, w_out[l].astype(BF16), g_ffn[l].reshape(1, d),
                      w_gu[l].astype(BF16), w_down[l].astype(BF16), g_final.reshape(1, d),
                      tm=tm, final_norm=(l == depth - 1))
    return x
```

```python
import functools

import numpy as np
import jax
import jax.numpy as jnp
from jax import lax
from jax.experimental import pallas as pl
from jax.experimental.pallas import tpu as pltpu

F32 = jnp.float32
BF16 = jnp.bfloat16

HEAD_DIM = 64
BLOCK = 128
DILATED_CONFIGS = ((128, 1), (512, 4), (2048, 16))
IDX_HEADS = 8
IDX_DIM = 64
TOPK_MAX = 256
EPS = 1e-6
NEG = -1e30
F32_MIN = float(np.finfo(np.float32).min)
LOG2E = float(np.log2(np.e))
POS_RADIX = 64
POS_ROWS = 16
LANES = 128
SUBLANES = 8
COUNT_CHAINS = 8
VMEM_LIMIT = 56 * 1024 * 1024


def _cparams(sem):
    return pltpu.CompilerParams(dimension_semantics=sem, vmem_limit_bytes=VMEM_LIMIT)


def _const_spec(shape):
    nd = len(shape)
    return pl.BlockSpec(shape, lambda *_: (0,) * nd, pipeline_mode=pl.Buffered(1))


def _mod_kernel(c_ref, w_ref, b_ref, o_ref):
    c = c_ref[...]
    c_act = c * (1.0 / (1.0 + jnp.exp(-c)))
    o_ref[...] = jnp.dot(c_act, w_ref[...], preferred_element_type=F32,
                         precision=lax.Precision.HIGHEST) + b_ref[...]


def _mod_call(c, w_ada, b_ada):
    bsz, d = c.shape
    n = w_ada.shape[1]
    tn = 1024
    return pl.pallas_call(
        _mod_kernel,
        grid=(n // tn,),
        in_specs=[pl.BlockSpec((bsz, d), lambda j: (0, 0)),
                  pl.BlockSpec((d, tn), lambda j: (0, j)),
                  pl.BlockSpec((1, tn), lambda j: (0, j))],
        out_specs=pl.BlockSpec((bsz, tn), lambda j: (0, j)),
        out_shape=jax.ShapeDtypeStruct((bsz, n), F32),
        compiler_params=_cparams(("arbitrary",)),
        name="adaln_mod",
    )(c, w_ada, b_ada.reshape(1, n))


def _rms(x):
    return x * lax.rsqrt(jnp.mean(x * x, axis=-1, keepdims=True) + EPS)


def _in_kernel(x_ref, mod_ref, g_ref, w_ref, wuk_ref, kvg_ref,
               qa_ref, ka_ref, va_ref, qlt_ref, ckv_ref, ckvt_ref, qit_ref, ki_ref, wit_ref,
               *, nh, idx_scale):
    hd = HEAD_DIM
    h = _rms(x_ref[0]) * g_ref[...]
    h = h * (1.0 + mod_ref[0, 1:2, :]) + mod_ref[0, 0:1, :]
    proj = jnp.dot(h.astype(BF16), w_ref[...], preferred_element_type=F32)
    wa = nh * hd
    rank = ckvt_ref.shape[1]
    tm = x_ref.shape[1]
    o_qb, o_ckv = 3 * wa, 4 * wa
    o_qi = o_ckv + rank
    o_ki = o_qi + IDX_HEADS * IDX_DIM
    qa_ref[0] = proj[:, 0:wa] * (hd ** -0.5 * LOG2E)
    ka_ref[0] = proj[:, wa:2 * wa]
    va_ref[0] = proj[:, 2 * wa:3 * wa]
    for hh in range(nh):
        qb_h = proj[:, o_qb + hh * hd:o_qb + (hh + 1) * hd].astype(BF16)
        q_lat = jnp.dot(qb_h, wuk_ref[hh], preferred_element_type=F32) * (hd ** -0.5 * LOG2E)
        qlt_ref[0, hh * rank:(hh + 1) * rank, :] = q_lat.T.astype(BF16)
    ckv = (_rms(proj[:, o_ckv:o_ckv + rank]) * kvg_ref[...])
    ckvt_ref[0] = ckv.T.astype(BF16)
    t = pl.program_id(1) * tm + lax.broadcasted_iota(jnp.int32, (tm, LANES), 0)
    lane = lax.broadcasted_iota(jnp.int32, (tm, LANES), 1)
    pos = jnp.where(lane < 2, t // POS_RADIX, jnp.where(lane < 4, t % POS_RADIX, jnp.where(lane < 6, 1, 0)))
    ckv_ref[0] = jnp.concatenate([ckv, pos.astype(F32)], axis=-1).astype(BF16)
    for pp in range(IDX_HEADS * IDX_DIM // LANES):
        qit_ref[0, pp * LANES:(pp + 1) * LANES, :] = proj[:, o_qi + pp * LANES:o_qi + (pp + 1) * LANES].T.astype(BF16)
    kiw = proj[:, o_ki:o_ki + LANES]
    ki_ref[0] = kiw[:, :IDX_DIM].astype(BF16)
    wit_ref[0] = kiw.T[IDX_DIM:IDX_DIM + IDX_HEADS, :] * idx_scale


def _in_call(x, mod, g, w_in_p, w_uk, kvg, *, nh, tm):
    bsz, seq, d = x.shape
    n_pad = w_in_p.shape[1]
    rank = w_uk.shape[-1]
    hd = HEAD_DIM
    wa = nh * hd
    idx_scale = float((IDX_HEADS * IDX_DIM) ** -0.5)
    row_spec = lambda w: pl.BlockSpec((1, tm, w), lambda b, i: (b, i, 0))
    col_spec = lambda r: pl.BlockSpec((1, r, tm), lambda b, i: (b, 0, i))
    return pl.pallas_call(
        functools.partial(_in_kernel, nh=nh, idx_scale=idx_scale),
        grid=(bsz, seq // tm),
        in_specs=[pl.BlockSpec((1, tm, d), lambda b, i: (b, i, 0)),
                  pl.BlockSpec((1, 6, d), lambda b, i: (b, 0, 0)),
                  _const_spec((1, d)),
                  _const_spec((d, n_pad)),
                  _const_spec((nh, hd, rank)),
                  _const_spec((1, rank))],
        out_specs=[row_spec(wa), row_spec(wa), row_spec(wa),
                   col_spec(nh * rank), row_spec(rank + LANES), col_spec(rank),
                   col_spec(IDX_HEADS * IDX_DIM), row_spec(IDX_DIM), col_spec(IDX_HEADS)],
        out_shape=[jax.ShapeDtypeStruct((bsz, seq, wa), F32),
                   jax.ShapeDtypeStruct((bsz, seq, wa), F32),
                   jax.ShapeDtypeStruct((bsz, seq, wa), F32),
                   jax.ShapeDtypeStruct((bsz, nh * rank, seq), BF16),
                   jax.ShapeDtypeStruct((bsz, seq, rank + LANES), BF16),
                   jax.ShapeDtypeStruct((bsz, rank, seq), BF16),
                   jax.ShapeDtypeStruct((bsz, IDX_HEADS * IDX_DIM, seq), BF16),
                   jax.ShapeDtypeStruct((bsz, seq, IDX_DIM), BF16),
                   jax.ShapeDtypeStruct((bsz, IDX_HEADS, seq), F32)],
        compiler_params=_cparams(("parallel", "parallel")),
        name="in_proj",
    )(x, mod, g, w_in_p, w_uk, kvg)


def _band_tables(tq, kw, w_res):
    r = np.arange(tq)[None, :]
    c = np.arange(kw)[:, None]
    clean = lambda d: np.where((d >= 0) & (d <= w_res), d, -1).astype(np.float32)
    return jnp.asarray(np.stack([clean(r - c), clean(r - c + w_res)]))


def _dil_kernel(slope_ref, q_ref, k_ref, v_ref, *refs, w_res, plan):
    n_tab = 1 + max(t for _, _, t in plan)
    dtab_refs = refs[:n_tab]
    o_ref, m_ref, l_ref, acc_ref, bias_ref, s_ref, mloc_ref = refs[n_tab:]
    seq = q_ref.shape[1]
    hp = pl.program_id(1)
    feat = lax.broadcasted_iota(jnp.int32, (LANES, 1), 0)
    first = feat < HEAD_DIM
    keep = (first, jnp.logical_not(first))
    m_ref[...] = jnp.full(m_ref.shape, NEG, F32)
    l_ref[...] = jnp.zeros(l_ref.shape, F32)
    acc_ref[...] = jnp.zeros(acc_ref.shape, F32)

    for dil, tq, tab in plan:
        dtab = dtab_refs[tab]
        kw = dtab.shape[1]
        nt = seq // dil // tq
        for hh in range(2):
            neg_slope = -slope_ref[hp * 2 + hh] * (float(dil) * LOG2E)
            for off in range(2):
                bias_ref[off, hh, :kw, :tq] = jnp.where(dtab[off] >= 0.0, neg_slope * dtab[off], NEG)

        def rows(start, size, dil=dil):
            return pl.ds(start, size) if dil == 1 else pl.ds(start, size, stride=dil)

        def coords(w, dil=dil, tq=tq, nt=nt):
            r = w // nt
            jt = w - r * nt
            q0 = jt * (tq * dil) + r
            k0 = jnp.maximum(jt * tq - w_res, 0) * dil + r
            return q0, k0, jnp.minimum(jt, 1)

        def logits_stage(w, rows=rows, coords=coords, tq=tq, kw=kw):
            q0, k0, off = coords(w)
            q_t = q_ref[0, rows(q0, tq), :].T
            k2 = k_ref[0, rows(k0, kw), :].astype(BF16)
            for hh in range(2):
                qh = jnp.where(keep[hh], q_t, 0.0).astype(BF16)
                s = jnp.dot(k2, qh, preferred_element_type=F32) + bias_ref[off, hh, :kw, :tq]
                s_ref[hh, :kw, :tq] = s
                mloc_ref[hh, :, :tq] = jnp.max(s, axis=0, keepdims=True)

        def value_stage(w, rows=rows, coords=coords, tq=tq, kw=kw):
            q0, k0, _ = coords(w)
            v_t = v_ref[0, rows(k0, kw), :].T.astype(BF16)
            ms = [mloc_ref[hh, :, :tq] for hh in range(2)]
            ps = [jnp.exp2(s_ref[hh, :kw, :tq] - ms[hh]) for hh in range(2)]
            ls = [jnp.sum(p, axis=0, keepdims=True) for p in ps]
            outs = [jnp.dot(v_t, p.astype(BF16), preferred_element_type=F32) for p in ps]
            pick = lambda pair: jnp.where(first, pair[0], pair[1]).T
            spread = lambda pair: [jnp.broadcast_to(x, (LANES, tq)) for x in pair]
            m_t, l_t, o_t = pick(spread(ms)), pick(spread(ls)), pick(outs)
            q_rows = rows(q0, tq)
            m_old = m_ref[q_rows, :]
            m_new = jnp.maximum(m_old, m_t)
            a_old = jnp.exp2(m_old - m_new)
            a_new = jnp.exp2(m_t - m_new)
            acc_ref[q_rows, :] = a_old * acc_ref[q_rows, :] + a_new * o_t
            l_ref[q_rows, :] = a_old * l_ref[q_rows, :] + a_new * l_t
            m_ref[q_rows, :] = m_new

        def pipe_body(w, carry, logits_stage=logits_stage, value_stage=value_stage):
            value_stage(w - 1)
            logits_stage(w)
            return carry

        n_tiles = dil * nt
        logits_stage(0)
        lax.fori_loop(1, n_tiles, pipe_body, 0)
        value_stage(n_tiles - 1)

    o_ref[0] = (acc_ref[...] / l_ref[...]).astype(BF16)


def _dil_call(slopes, qa, ka, va):
    bsz, seq, wa = qa.shape
    w_res = DILATED_CONFIGS[0][0] // DILATED_CONFIGS[0][1]
    assert all(w // d == w_res for w, d in DILATED_CONFIGS)
    tile_shapes, plan = [], []
    for _, dil in DILATED_CONFIGS:
        n = seq // dil
        tq = min(256, n)
        kw = tq if n == tq else tq + w_res
        assert n % tq == 0 and tq >= w_res and kw <= n
        if (tq, kw) not in tile_shapes:
            tile_shapes.append((tq, kw))
        plan.append((dil, tq, tile_shapes.index((tq, kw))))
    tables = [_band_tables(tq, kw, w_res) for tq, kw in tile_shapes]
    tq_max = max(tq for tq, _ in tile_shapes)
    kw_max = max(kw for _, kw in tile_shapes)
    blk = pl.BlockSpec((1, seq, LANES), lambda b, h: (b, 0, h))
    return pl.pallas_call(
        functools.partial(_dil_kernel, w_res=w_res, plan=tuple(plan)),
        grid=(bsz, wa // LANES),
        in_specs=[pl.BlockSpec(memory_space=pltpu.SMEM), blk, blk, blk] + [_const_spec(t.shape) for t in tables],
        out_specs=blk,
        out_shape=jax.ShapeDtypeStruct((bsz, seq, wa), BF16),
        scratch_shapes=[pltpu.VMEM((seq, LANES), F32), pltpu.VMEM((seq, LANES), F32),
                        pltpu.VMEM((seq, LANES), F32),
                        pltpu.VMEM((2, 2, kw_max, tq_max), F32),
                        pltpu.VMEM((2, kw_max, tq_max), F32),
                        pltpu.VMEM((2, 1, tq_max), F32)],
        compiler_params=_cparams(("parallel", "parallel")),
        name="dilated_attn",
    )(jnp.asarray(slopes, F32), qa, ka, va, *tables)


def _key_to_f32(key):
    bits = key ^ (lax.shift_right_arithmetic(key, 31) & jnp.int32(0x7FFFFFFF))
    return lax.bitcast_convert_type(bits, F32)


def _dsa_kernel(qit_ref, wit_ref, ki_ref, qlt_ref, qpos_ref, ckv_ref, ckvt_ref, wuv_ref, tri_ref, o_ref,
                sc_ref, acc_ref, s_ref, s2_ref, l_ref, *, topk, chunk):
    i = pl.program_id(1)
    t0 = i * BLOCK
    rank = ckvt_ref.shape[1]
    nh = qlt_ref.shape[1] // rank
    n_ch = (t0 + BLOCK + chunk - 1) // chunk
    kk = lax.broadcasted_iota(jnp.int32, (chunk, BLOCK), 0)
    qq = lax.broadcasted_iota(jnp.int32, (chunk, BLOCK), 1)
    rel = qq - kk
    k_start = lambda j: pl.multiple_of(j * chunk, chunk)
    heads = lambda ref, width: jnp.concatenate(
        [ref[0, hh * width:(hh + 1) * width, :] for hh in range(ref.shape[1] // width)], axis=-1)

    qit = heads(qit_ref, IDX_DIM)
    wt = wit_ref[0]

    idx_pair = 2 * BLOCK
    idx_cols = [slice(pp * idx_pair, (pp + 1) * idx_pair) for pp in range(IDX_HEADS * BLOCK // idx_pair)]

    def idx_logits(j, buf):
        kc = ki_ref[0, pl.ds(k_start(j), chunk), :]
        for cols in idx_cols:
            buf[:, cols] = jnp.dot(kc, qit[:, cols], preferred_element_type=F32)

    def idx_scores(j, buf):
        k0 = k_start(j)
        score = jnp.zeros((chunk, BLOCK), F32)
        for pp, cols in enumerate(idx_cols):
            lg = buf[:, cols]
            score = score + (wt[2 * pp:2 * pp + 1, :] * jnp.maximum(lg[:, :BLOCK], 0.0)
                             + wt[2 * pp + 1:2 * pp + 2, :] * jnp.maximum(lg[:, BLOCK:], 0.0))
        score = jnp.where(score == 0.0, 0.0, score)
        sc_ref[pl.ds(k0, chunk), :] = jnp.where(rel + (t0 - k0) >= 0, score, -jnp.inf)

    def idx_advance(j, buf_next, buf_cur):
        idx_logits(j + 1, buf_next)
        idx_scores(j, buf_cur)

    def idx_double_step(t, carry):
        idx_advance(2 * t, s2_ref, s_ref)
        idx_advance(2 * t + 1, s_ref, s2_ref)
        return carry

    idx_logits(0, s_ref)
    n_double = (n_ch - 1) // 2
    lax.fori_loop(0, n_double, idx_double_step, 0)
    last = 2 * n_double

    @pl.when(last == n_ch - 1)
    def _():
        idx_scores(last, s_ref)

    @pl.when(last < n_ch - 1)
    def _():
        idx_advance(last, s2_ref, s_ref)
        idx_scores(last + 1, s2_ref)

    def count(pred):
        def body(j, cnt):
            hit = jnp.where(pred(sc_ref[pl.ds(k_start(j), chunk), :]), 1.0, 0.0)
            return cnt + jnp.sum(hit.reshape(-1, COUNT_CHAINS, SUBLANES, BLOCK), axis=0)
        cnt = lax.fori_loop(0, n_ch, body, jnp.zeros((COUNT_CHAINS, SUBLANES, BLOCK), F32))
        return jnp.sum(jnp.sum(cnt, axis=0), axis=0, keepdims=True)

    kf = float(topk)
    n_all = (n_ch * chunk).astype(F32)
    n_pos = count(lambda s: s >= 0.0)
    pos_ok = n_pos >= kf
    key = jnp.where(pos_ok, jnp.int32(0), jnp.int32(-2 ** 31))
    n_key = jnp.where(pos_ok, n_pos, n_all)

    def bit_body(b, carry):
        key, n_key = carry
        trial = key | lax.shift_left(jnp.int32(1), 30 - b)
        cand = _key_to_f32(trial)
        n_trial = count(lambda s: s >= cand)
        ok = n_trial >= kf
        return jnp.where(ok, trial, key), jnp.where(ok, n_trial, n_key)

    tq = t0 + lax.broadcasted_iota(jnp.int32, (1, BLOCK), 1)
    few = tq < topk - 1
    key, n_key = lax.fori_loop(0, 31, bit_body, (key, n_key))
    tau = jnp.where(few, F32_MIN, _key_to_f32(key))

    @pl.when(jnp.max(jnp.where(few, 0.0, n_key - kf)) > 0.0)
    def _():
        n_tie = kf - count(lambda s: s > tau)

        def tie_body(j, seen):
            k0 = k_start(j)
            s = sc_ref[pl.ds(k0, chunk), :]
            eq = jnp.where(s == tau, 1.0, 0.0)
            rank_eq = seen + jnp.dot(tri_ref[...], eq.astype(BF16), preferred_element_type=F32)
            drop = jnp.where(rank_eq > n_tie, eq, 0.0)
            sc_ref[pl.ds(k0, chunk), :] = jnp.where(drop > 0.0, -jnp.inf, s)
            return seen + jnp.sum(eq, axis=0, keepdims=True)

        lax.fori_loop(0, n_ch, tie_body, jnp.zeros((1, BLOCK), F32))

    pad_rows = ckv_ref.shape[-1] - rank - POS_ROWS
    q_aug = jnp.concatenate([heads(qlt_ref, rank), qpos_ref[0],
                             jnp.zeros((pad_rows, nh * BLOCK), BF16)], axis=0)
    acc_ref[...] = jnp.zeros(acc_ref.shape, F32)
    pair = 2 * BLOCK
    n_pairs = nh * BLOCK // pair
    col_slices = [slice(pp * pair, (pp + 1) * pair) for pp in range(n_pairs)]

    cat = lambda parts: jnp.concatenate(parts, axis=-1)

    def logits_stage(j, buf):
        mask = jnp.where(sc_ref[pl.ds(k_start(j), chunk), :] >= tau, 0.0, NEG)
        mask2 = jnp.concatenate([mask, mask], axis=-1)
        kc = ckv_ref[0, pl.ds(k_start(j), chunk), :]
        cmax = []
        for cols in col_slices:
            s = jnp.dot(kc, q_aug[:, cols], preferred_element_type=F32) + mask2
            buf[:, cols] = s
            cmax.append(jnp.max(s, axis=0, keepdims=True))
        return cat(cmax)

    ones_rows = jnp.ones((2 * SUBLANES, chunk), BF16)

    def value_stage(j, buf, m_use, alpha):
        kct = jnp.concatenate([ckvt_ref[0, :, pl.ds(k_start(j), chunk)], ones_rows], axis=0)
        sums = []
        for cols in col_slices:
            p = jnp.exp2(buf[:, cols] - m_use[:, cols])
            pv = jnp.dot(kct, p.astype(BF16), preferred_element_type=F32)
            acc_ref[:, cols] = alpha[:, cols] * acc_ref[:, cols] + pv[:rank]
            sums.append(pv[rank:rank + 1])
        return cat(sums)

    def finish(j, buf_cur, carry):
        m_cur, alpha_cur, l_prev = carry
        return alpha_cur * l_prev + value_stage(j, buf_cur, m_cur, alpha_cur)

    def advance(j, buf_next, buf_cur, carry):
        m_cur = carry[0]
        m_next = jnp.maximum(m_cur, logits_stage(j + 1, buf_next))
        return m_next, jnp.exp2(m_cur - m_next), finish(j, buf_cur, carry)

    def double_step(t, carry):
        carry = advance(2 * t, s2_ref, s_ref, carry)
        return advance(2 * t + 1, s_ref, s2_ref, carry)

    zeros_row = jnp.zeros((1, nh * BLOCK), F32)
    n_double = (n_ch - 1) // 2
    carry = lax.fori_loop(0, n_double, double_step, (logits_stage(0, s_ref), zeros_row, zeros_row))
    last = 2 * n_double

    @pl.when(last == n_ch - 1)
    def _():
        l_ref[...] = finish(last, s_ref, carry)

    @pl.when(last < n_ch - 1)
    def _():
        l_ref[...] = finish(last + 1, s2_ref, advance(last, s2_ref, s_ref, carry))

    l_fin = l_ref[...]

    o_lat_t = acc_ref[...] / l_fin
    outs = [jnp.dot(o_lat_t[:, hh * BLOCK:(hh + 1) * BLOCK].T.astype(BF16), wuv_ref[hh],
                    preferred_element_type=F32) for hh in range(nh)]
    o_ref[0] = jnp.concatenate(outs, axis=-1).astype(BF16)


def _alibi_query_table(slopes, n_blocks):
    sl = np.asarray(slopes, np.float64) * LOG2E
    t0 = np.arange(n_blocks, dtype=np.float64)[:, None] * BLOCK
    coef = [np.broadcast_to(sl * POS_RADIX, (n_blocks, sl.size)),
            np.broadcast_to(sl, (n_blocks, sl.size)),
            -sl[None, :] * t0]
    rows = []
    for cf in coef:
        hi = jnp.asarray(cf, F32).astype(BF16)
        lo = (jnp.asarray(cf, F32) - hi.astype(F32)).astype(BF16)
        rows += [hi, lo]
    tab = jnp.stack(rows + [jnp.zeros_like(rows[0])] * (POS_ROWS - len(rows)), axis=1)
    return jnp.repeat(tab, BLOCK, axis=-1)


def _dsa_call(slopes, qit, wit, ki, qlt, ckv, ckvt, w_uv, *, chunk=256):
    bsz, seq, ckv_w = ckv.shape
    rank = ckvt.shape[1]
    nh, _, hd = w_uv.shape
    topk = min(TOPK_MAX, seq // 4)
    assert seq % chunk == 0 and seq // POS_RADIX <= 256
    tri = jnp.asarray(np.tril(np.ones((chunk, chunk), np.float32)), BF16)
    qpos = _alibi_query_table(slopes, seq // BLOCK)
    return pl.pallas_call(
        functools.partial(_dsa_kernel, topk=topk, chunk=chunk),
        grid=(bsz, seq // BLOCK),
        in_specs=[pl.BlockSpec((1, IDX_HEADS * IDX_DIM, BLOCK), lambda b, i: (b, 0, i)),
                  pl.BlockSpec((1, IDX_HEADS, BLOCK), lambda b, i: (b, 0, i)),
                  pl.BlockSpec((1, seq, IDX_DIM), lambda b, i: (b, 0, 0)),
                  pl.BlockSpec((1, nh * rank, BLOCK), lambda b, i: (b, 0, i)),
                  pl.BlockSpec((1, POS_ROWS, nh * BLOCK), lambda b, i: (i, 0, 0)),
                  pl.BlockSpec((1, seq, ckv_w), lambda b, i: (b, 0, 0)),
                  pl.BlockSpec((1, rank, seq), lambda b, i: (b, 0, 0)),
                  _const_spec((nh, rank, hd)),
                  _const_spec((chunk, chunk))],
        out_specs=pl.BlockSpec((1, BLOCK, nh * hd), lambda b, i: (b, i, 0)),
        out_shape=jax.ShapeDtypeStruct((bsz, seq, nh * hd), BF16),
        scratch_shapes=[pltpu.VMEM((seq, BLOCK), F32),
                        pltpu.VMEM((rank, nh * BLOCK), F32),
                        pltpu.VMEM((chunk, nh * BLOCK), F32),
                        pltpu.VMEM((chunk, nh * BLOCK), F32),
                        pltpu.VMEM((1, nh * BLOCK), F32)],
        compiler_params=_cparams(("parallel", "arbitrary")),
        name="sparse_attn",
    )(qit, wit, ki, qlt, qpos, ckv, ckvt, w_uv, tri)


def _out_kernel(x_ref, oa_ref, ob_ref, mod_ref, wo_ref, g2_ref, wgu_ref, wd_ref, gf_ref, o_ref,
                *, d_ff, ff_chunks, final_norm):
    mixed = jnp.concatenate([oa_ref[0], ob_ref[0]], axis=-1)
    x1 = x_ref[0] + mod_ref[0, 2:3, :] * jnp.dot(mixed, wo_ref[...], preferred_element_type=F32)
    h2 = _rms(x1) * g2_ref[...]
    h2 = (h2 * (1.0 + mod_ref[0, 4:5, :]) + mod_ref[0, 3:4, :]).astype(BF16)
    fc = d_ff // ff_chunks
    ffn = jnp.zeros(x1.shape, F32)
    for cc in range(ff_chunks):
        gate = jnp.dot(h2, wgu_ref[:, cc * fc:(cc + 1) * fc], preferred_element_type=F32)
        up = jnp.dot(h2, wgu_ref[:, d_ff + cc * fc:d_ff + (cc + 1) * fc], preferred_element_type=F32)
        act = gate * (1.0 / (1.0 + jnp.exp(-gate))) * up
        ffn = ffn + jnp.dot(act.astype(BF16), wd_ref[cc * fc:(cc + 1) * fc, :], preferred_element_type=F32)
    x2 = x1 + mod_ref[0, 5:6, :] * ffn
    if final_norm:
        x2 = _rms(x2) * gf_ref[...]
    o_ref[0] = x2


def _out_call(x, oa, ob, mod, w_out, g_ffn, w_gu, w_down, g_final, *, tm, final_norm):
    bsz, seq, d = x.shape
    d_ff = w_down.shape[0]
    ff_chunks = 2 if d_ff % (2 * LANES) == 0 else 1
    row = lambda w: pl.BlockSpec((1, tm, w), lambda b, i: (b, i, 0))
    return pl.pallas_call(
        functools.partial(_out_kernel, d_ff=d_ff, ff_chunks=ff_chunks, final_norm=final_norm),
        grid=(bsz, seq // tm),
        in_specs=[row(d), row(oa.shape[-1]), row(ob.shape[-1]),
                  pl.BlockSpec((1, 6, d), lambda b, i: (b, 0, 0)),
                  _const_spec((d, d)), _const_spec((1, d)),
                  _const_spec((d, 2 * d_ff)), _const_spec((d_ff, d)), _const_spec((1, d))],
        out_specs=row(d),
        out_shape=jax.ShapeDtypeStruct((bsz, seq, d), F32),
        compiler_params=_cparams(("parallel", "parallel")),
        name="out_proj_ffn",
    )(x, oa, ob, mod, w_out, g_ffn, w_gu, w_down, g_final)


def _alibi_slopes(n_heads):
    s = 2.0 ** (-8.0 * (np.arange(n_heads, dtype=np.float32) + 1.0) / n_heads)
    return s[0::2], s[1::2]


def kernel(x, c, w_ada, b_ada, g_attn, w_in, kv_norm_g, w_uk, w_uv, w_out, g_ffn, w_gu, w_down, g_final):
    bsz, seq, d = x.shape
    depth = w_ada.shape[0]
    nh = w_uk.shape[1]
    slopes_a, slopes_b = _alibi_slopes(2 * nh)
    d_in = w_in.shape[-1]
    n_pad = -(-d_in // LANES) * LANES
    tm = 512 if seq % 512 == 0 else BLOCK
    for l in range(depth):
        mod = _mod_call(c, w_ada[l], b_ada[l]).reshape(bsz, 6, d)
        w_in_p = jnp.pad(w_in[l], ((0, 0), (0, n_pad - d_in))).astype(BF16)
        qa, ka, va, qlt, ckv, ckvt, qit, ki, wit = _in_call(
            x, mod, g_attn[l].reshape(1, d), w_in_p, w_uk[l].astype(BF16),
            kv_norm_g[l].reshape(1, -1), nh=nh, tm=tm)
        out_a = _dil_call(slopes_a, qa, ka, va)
        out_b = _dsa_call(slopes_b, qit, wit, ki, qlt, ckv, ckvt, w_uv[l].astype(BF16))
        x = _out_call(x, out_a, out_b, mod, w_out[l].astype(BF16), g_ffn[l].reshape(1, d),
                      w_gu[l].astype(BF16), w_down[l].astype(BF16), g_final.reshape(1, d),
                      tm=tm, final_norm=(l == depth - 1))
    return x
```

```python
import functools

import numpy as np
import jax
import jax.numpy as jnp
from jax import lax
from jax.experimental import pallas as pl
from jax.experimental.pallas import tpu as pltpu

F32 = jnp.float32
BF16 = jnp.bfloat16

HEAD_DIM = 64
BLOCK = 128
DILATED_CONFIGS = ((128, 1), (512, 4), (2048, 16))
IDX_HEADS = 8
IDX_DIM = 64
TOPK_MAX = 256
EPS = 1e-6
NEG = -1e30
F32_MIN = float(np.finfo(np.float32).min)
LOG2E = float(np.log2(np.e))
POS_RADIX = 64
POS_ROWS = 16
LANES = 128
SUBLANES = 8
COUNT_CHAINS = 8
VMEM_LIMIT = 56 * 1024 * 1024


def _cparams(sem):
    return pltpu.CompilerParams(dimension_semantics=sem, vmem_limit_bytes=VMEM_LIMIT)


def _const_spec(shape):
    nd = len(shape)
    return pl.BlockSpec(shape, lambda *_: (0,) * nd, pipeline_mode=pl.Buffered(1))


def _mod_kernel(c_ref, w_ref, b_ref, o_ref):
    c = c_ref[...]
    c_act = c * (1.0 / (1.0 + jnp.exp(-c)))
    o_ref[...] = jnp.dot(c_act, w_ref[...], preferred_element_type=F32,
                         precision=lax.Precision.HIGHEST) + b_ref[...]


def _mod_call(c, w_ada, b_ada):
    bsz, d = c.shape
    n = w_ada.shape[1]
    tn = 1024
    return pl.pallas_call(
        _mod_kernel,
        grid=(n // tn,),
        in_specs=[pl.BlockSpec((bsz, d), lambda j: (0, 0)),
                  pl.BlockSpec((d, tn), lambda j: (0, j)),
                  pl.BlockSpec((1, tn), lambda j: (0, j))],
        out_specs=pl.BlockSpec((bsz, tn), lambda j: (0, j)),
        out_shape=jax.ShapeDtypeStruct((bsz, n), F32),
        compiler_params=_cparams(("arbitrary",)),
        name="adaln_mod",
    )(c, w_ada, b_ada.reshape(1, n))


def _rms(x):
    return x * lax.rsqrt(jnp.mean(x * x, axis=-1, keepdims=True) + EPS)


def _in_kernel(x_ref, mod_ref, g_ref, w_ref, wuk_ref, kvg_ref,
               qa_ref, ka_ref, va_ref, qlt_ref, ckv_ref, ckvt_ref, qit_ref, ki_ref, wit_ref,
               *, nh, idx_scale):
    hd = HEAD_DIM
    h = _rms(x_ref[0]) * g_ref[...]
    h = h * (1.0 + mod_ref[0, 1:2, :]) + mod_ref[0, 0:1, :]
    proj = jnp.dot(h.astype(BF16), w_ref[...], preferred_element_type=F32)
    wa = nh * hd
    rank = ckvt_ref.shape[1]
    tm = x_ref.shape[1]
    o_qb, o_ckv = 3 * wa, 4 * wa
    o_qi = o_ckv + rank
    o_ki = o_qi + IDX_HEADS * IDX_DIM
    qa_ref[0] = proj[:, 0:wa] * (hd ** -0.5 * LOG2E)
    ka_ref[0] = proj[:, wa:2 * wa]
    va_ref[0] = proj[:, 2 * wa:3 * wa]
    for hh in range(nh):
        qb_h = proj[:, o_qb + hh * hd:o_qb + (hh + 1) * hd].astype(BF16)
        q_lat = jnp.dot(qb_h, wuk_ref[hh], preferred_element_type=F32) * (hd ** -0.5 * LOG2E)
        qlt_ref[0, hh * rank:(hh + 1) * rank, :] = q_lat.T.astype(BF16)
    ckv = (_rms(proj[:, o_ckv:o_ckv + rank]) * kvg_ref[...])
    ckvt_ref[0] = ckv.T.astype(BF16)
    t = pl.program_id(1) * tm + lax.broadcasted_iota(jnp.int32, (tm, LANES), 0)
    lane = lax.broadcasted_iota(jnp.int32, (tm, LANES), 1)
    pos = jnp.where(lane < 2, t // POS_RADIX, jnp.where(lane < 4, t % POS_RADIX, jnp.where(lane < 6, 1, 0)))
    ckv_ref[0] = jnp.concatenate([ckv, pos.astype(F32)], axis=-1).astype(BF16)
    for pp in range(IDX_HEADS * IDX_DIM // LANES):
        qit_ref[0, pp * LANES:(pp + 1) * LANES, :] = proj[:, o_qi + pp * LANES:o_qi + (pp + 1) * LANES].T.astype(BF16)
    kiw = proj[:, o_ki:o_ki + LANES]
    ki_ref[0] = kiw[:, :IDX_DIM].astype(BF16)
    wit_ref[0] = kiw.T[IDX_DIM:IDX_DIM + IDX_HEADS, :] * idx_scale


def _in_call(x, mod, g, w_in_p, w_uk, kvg, *, nh, tm):
    bsz, seq, d = x.shape
    n_pad = w_in_p.shape[1]
    rank = w_uk.shape[-1]
    hd = HEAD_DIM
    wa = nh * hd
    idx_scale = float((IDX_HEADS * IDX_DIM) ** -0.5)
    row_spec = lambda w: pl.BlockSpec((1, tm, w), lambda b, i: (b, i, 0))
    col_spec = lambda r: pl.BlockSpec((1, r, tm), lambda b, i: (b, 0, i))
    return pl.pallas_call(
        functools.partial(_in_kernel, nh=nh, idx_scale=idx_scale),
        grid=(bsz, seq // tm),
        in_specs=[pl.BlockSpec((1, tm, d), lambda b, i: (b, i, 0)),
                  pl.BlockSpec((1, 6, d), lambda b, i: (b, 0, 0)),
                  _const_spec((1, d)),
                  _const_spec((d, n_pad)),
                  _const_spec((nh, hd, rank)),
                  _const_spec((1, rank))],
        out_specs=[row_spec(wa), row_spec(wa), row_spec(wa),
                   col_spec(nh * rank), row_spec(rank + LANES), col_spec(rank),
                   col_spec(IDX_HEADS * IDX_DIM), row_spec(IDX_DIM), col_spec(IDX_HEADS)],
        out_shape=[jax.ShapeDtypeStruct((bsz, seq, wa), F32),
                   jax.ShapeDtypeStruct((bsz, seq, wa), F32),
                   jax.ShapeDtypeStruct((bsz, seq, wa), F32),
                   jax.ShapeDtypeStruct((bsz, nh * rank, seq), BF16),
                   jax.ShapeDtypeStruct((bsz, seq, rank + LANES), BF16),
                   jax.ShapeDtypeStruct((bsz, rank, seq), BF16),
                   jax.ShapeDtypeStruct((bsz, IDX_HEADS * IDX_DIM, seq), BF16),
                   jax.ShapeDtypeStruct((bsz, seq, IDX_DIM), BF16),
                   jax.ShapeDtypeStruct((bsz, IDX_HEADS, seq), F32)],
        compiler_params=_cparams(("parallel", "parallel")),
        name="in_proj",
    )(x, mod, g, w_in_p, w_uk, kvg)


def _band_tables(tq, kw, w_res):
    r = np.arange(tq)[None, :]
    c = np.arange(kw)[:, None]
    clean = lambda d: np.where((d >= 0) & (d <= w_res), d, -1).astype(np.float32)
    return jnp.asarray(np.stack([clean(r - c), clean(r - c + w_res)]))


def _dil_kernel(slope_ref, q_ref, k_ref, v_ref, *refs, w_res, plan):
    n_tab = 1 + max(t for _, _, t in plan)
    dtab_refs = refs[:n_tab]
    o_ref, m_ref, l_ref, acc_ref, bias_ref, s_ref, mloc_ref = refs[n_tab:]
    seq = q_ref.shape[1]
    hp = pl.program_id(1)
    feat = lax.broadcasted_iota(jnp.int32, (LANES, 1), 0)
    first = feat < HEAD_DIM
    keep = (first, jnp.logical_not(first))

    for branch, (dil, tq, tab) in enumerate(plan):
        fresh = branch == 0
        dtab = dtab_refs[tab]
        kw = dtab.shape[1]
        nt = seq // dil // tq
        for hh in range(2):
            neg_slope = -slope_ref[hp * 2 + hh] * (float(dil) * LOG2E)
            for off in range(2):
                bias_ref[off, hh, :kw, :tq] = jnp.where(dtab[off] >= 0.0, neg_slope * dtab[off], NEG)

        def rows(start, size, dil=dil):
            return pl.ds(start, size) if dil == 1 else pl.ds(start, size, stride=dil)

        def coords(w, dil=dil, tq=tq, nt=nt):
            r = w // nt
            jt = w - r * nt
            q0 = jt * (tq * dil) + r
            k0 = jnp.maximum(jt * tq - w_res, 0) * dil + r
            return q0, k0, jnp.minimum(jt, 1)

        def logits_stage(w, rows=rows, coords=coords, tq=tq, kw=kw):
            q0, k0, off = coords(w)
            q_t = q_ref[0, rows(q0, tq), :].T
            k2 = k_ref[0, rows(k0, kw), :].astype(BF16)
            for hh in range(2):
                qh = jnp.where(keep[hh], q_t, 0.0).astype(BF16)
                s = jnp.dot(k2, qh, preferred_element_type=F32) + bias_ref[off, hh, :kw, :tq]
                s_ref[hh, :kw, :tq] = s
                mloc_ref[hh, :, :tq] = jnp.max(s, axis=0, keepdims=True)

        def value_stage(w, rows=rows, coords=coords, tq=tq, kw=kw, fresh=fresh):
            q0, k0, _ = coords(w)
            v_t = v_ref[0, rows(k0, kw), :].T.astype(BF16)
            ms = [mloc_ref[hh, :, :tq] for hh in range(2)]
            ps = [jnp.exp2(s_ref[hh, :kw, :tq] - ms[hh]) for hh in range(2)]
            ls = [jnp.sum(p, axis=0, keepdims=True) for p in ps]
            outs = [jnp.dot(v_t, p.astype(BF16), preferred_element_type=F32) for p in ps]
            pick = lambda pair: jnp.where(first, pair[0], pair[1]).T
            spread = lambda pair: [jnp.broadcast_to(x, (LANES, tq)) for x in pair]
            m_t, l_t, o_t = pick(spread(ms)), pick(spread(ls)), pick(outs)
            q_rows = rows(q0, tq)
            if fresh:
                acc_ref[q_rows, :] = o_t
                l_ref[q_rows, :] = l_t
                m_ref[q_rows, :] = m_t
                return
            m_old = m_ref[q_rows, :]
            m_new = jnp.maximum(m_old, m_t)
            a_old = jnp.exp2(m_old - m_new)
            a_new = jnp.exp2(m_t - m_new)
            acc_ref[q_rows, :] = a_old * acc_ref[q_rows, :] + a_new * o_t
            l_ref[q_rows, :] = a_old * l_ref[q_rows, :] + a_new * l_t
            m_ref[q_rows, :] = m_new

        def pipe_body(w, carry, logits_stage=logits_stage, value_stage=value_stage):
            value_stage(w - 1)
            logits_stage(w)
            return carry

        n_tiles = dil * nt
        logits_stage(0)
        lax.fori_loop(1, n_tiles, pipe_body, 0)
        value_stage(n_tiles - 1)

    o_ref[0] = (acc_ref[...] / l_ref[...]).astype(BF16)


def _dil_call(slopes, qa, ka, va):
    bsz, seq, wa = qa.shape
    w_res = DILATED_CONFIGS[0][0] // DILATED_CONFIGS[0][1]
    assert all(w // d == w_res for w, d in DILATED_CONFIGS)
    tile_shapes, plan = [], []
    for _, dil in DILATED_CONFIGS:
        n = seq // dil
        tq = min(256, n)
        kw = tq if n == tq else tq + w_res
        assert n % tq == 0 and tq >= w_res and kw <= n
        if (tq, kw) not in tile_shapes:
            tile_shapes.append((tq, kw))
        plan.append((dil, tq, tile_shapes.index((tq, kw))))
    tables = [_band_tables(tq, kw, w_res) for tq, kw in tile_shapes]
    tq_max = max(tq for tq, _ in tile_shapes)
    kw_max = max(kw for _, kw in tile_shapes)
    blk = pl.BlockSpec((1, seq, LANES), lambda b, h: (b, 0, h))
    return pl.pallas_call(
        functools.partial(_dil_kernel, w_res=w_res, plan=tuple(plan)),
        grid=(bsz, wa // LANES),
        in_specs=[pl.BlockSpec(memory_space=pltpu.SMEM), blk, blk, blk] + [_const_spec(t.shape) for t in tables],
        out_specs=blk,
        out_shape=jax.ShapeDtypeStruct((bsz, seq, wa), BF16),
        scratch_shapes=[pltpu.VMEM((seq, LANES), F32), pltpu.VMEM((seq, LANES), F32),
                        pltpu.VMEM((seq, LANES), F32),
                        pltpu.VMEM((2, 2, kw_max, tq_max), F32),
                        pltpu.VMEM((2, kw_max, tq_max), F32),
                        pltpu.VMEM((2, 1, tq_max), F32)],
        compiler_params=_cparams(("parallel", "parallel")),
        name="dilated_attn",
    )(jnp.asarray(slopes, F32), qa, ka, va, *tables)


def _key_to_f32(key):
    bits = key ^ (lax.shift_right_arithmetic(key, 31) & jnp.int32(0x7FFFFFFF))
    return lax.bitcast_convert_type(bits, F32)


def _dsa_kernel(qit_ref, wit_ref, ki_ref, qlt_ref, qpos_ref, ckv_ref, ckvt_ref, wuv_ref, tri_ref, o_ref,
                sc_ref, acc_ref, s_ref, s2_ref, l_ref, *, topk, chunk):
    i = pl.program_id(1)
    t0 = i * BLOCK
    rank = ckvt_ref.shape[1]
    nh = qlt_ref.shape[1] // rank
    n_ch = (t0 + BLOCK + chunk - 1) // chunk
    kk = lax.broadcasted_iota(jnp.int32, (chunk, BLOCK), 0)
    qq = lax.broadcasted_iota(jnp.int32, (chunk, BLOCK), 1)
    rel = qq - kk
    k_start = lambda j: pl.multiple_of(j * chunk, chunk)
    heads = lambda ref, width: jnp.concatenate(
        [ref[0, hh * width:(hh + 1) * width, :] for hh in range(ref.shape[1] // width)], axis=-1)

    qit = heads(qit_ref, IDX_DIM)
    wt = wit_ref[0]

    idx_pair = 2 * BLOCK
    idx_cols = [slice(pp * idx_pair, (pp + 1) * idx_pair) for pp in range(IDX_HEADS * BLOCK // idx_pair)]

    def idx_logits(j, buf):
        kc = ki_ref[0, pl.ds(k_start(j), chunk), :]
        for cols in idx_cols:
            buf[:, cols] = jnp.dot(kc, qit[:, cols], preferred_element_type=F32)

    def idx_scores(j, buf):
        k0 = k_start(j)
        score = jnp.zeros((chunk, BLOCK), F32)
        for pp, cols in enumerate(idx_cols):
            lg = buf[:, cols]
            score = score + (wt[2 * pp:2 * pp + 1, :] * jnp.maximum(lg[:, :BLOCK], 0.0)
                             + wt[2 * pp + 1:2 * pp + 2, :] * jnp.maximum(lg[:, BLOCK:], 0.0))
        score = jnp.where(score == 0.0, 0.0, score)
        sc_ref[pl.ds(k0, chunk), :] = jnp.where(rel + (t0 - k0) >= 0, score, -jnp.inf)

    def idx_advance(j, buf_next, buf_cur):
        idx_logits(j + 1, buf_next)
        idx_scores(j, buf_cur)

    def idx_double_step(t, carry):
        idx_advance(2 * t, s2_ref, s_ref)
        idx_advance(2 * t + 1, s_ref, s2_ref)
        return carry

    idx_logits(0, s_ref)
    n_double = (n_ch - 1) // 2
    lax.fori_loop(0, n_double, idx_double_step, 0)
    last = 2 * n_double

    @pl.when(last == n_ch - 1)
    def _():
        idx_scores(last, s_ref)

    @pl.when(last < n_ch - 1)
    def _():
        idx_advance(last, s2_ref, s_ref)
        idx_scores(last + 1, s2_ref)

    def count(pred):
        def body(j, cnt):
            hit = jnp.where(pred(sc_ref[pl.ds(k_start(j), chunk), :]), 1.0, 0.0)
            return cnt + jnp.sum(hit.reshape(-1, COUNT_CHAINS, SUBLANES, BLOCK), axis=0)

        def two_chunks(t, cnt):
            return body(2 * t + 1, body(2 * t, cnt))

        cnt = lax.fori_loop(0, n_ch // 2, two_chunks, jnp.zeros((COUNT_CHAINS, SUBLANES, BLOCK), F32))
        cnt = lax.fori_loop(2 * (n_ch // 2), n_ch, body, cnt)
        return jnp.sum(jnp.sum(cnt, axis=0), axis=0, keepdims=True)

    kf = float(topk)
    n_all = (n_ch * chunk).astype(F32)
    n_pos = count(lambda s: s >= 0.0)
    pos_ok = n_pos >= kf
    key = jnp.where(pos_ok, jnp.int32(0), jnp.int32(-2 ** 31))
    n_key = jnp.where(pos_ok, n_pos, n_all)

    def bit_body(b, carry):
        key, n_key = carry
        trial = key | lax.shift_left(jnp.int32(1), 30 - b)
        cand = _key_to_f32(trial)
        n_trial = count(lambda s: s >= cand)
        ok = n_trial >= kf
        return jnp.where(ok, trial, key), jnp.where(ok, n_trial, n_key)

    tq = t0 + lax.broadcasted_iota(jnp.int32, (1, BLOCK), 1)
    few = tq < topk - 1
    key, n_key = lax.fori_loop(0, 31, bit_body, (key, n_key))
    tau = jnp.where(few, F32_MIN, _key_to_f32(key))

    @pl.when(jnp.max(jnp.where(few, 0.0, n_key - kf)) > 0.0)
    def _():
        n_tie = kf - count(lambda s: s > tau)

        def tie_body(j, seen):
            k0 = k_start(j)
            s = sc_ref[pl.ds(k0, chunk), :]
            eq = jnp.where(s == tau, 1.0, 0.0)
            rank_eq = seen + jnp.dot(tri_ref[...], eq.astype(BF16), preferred_element_type=F32)
            drop = jnp.where(rank_eq > n_tie, eq, 0.0)
            sc_ref[pl.ds(k0, chunk), :] = jnp.where(drop > 0.0, -jnp.inf, s)
            return seen + jnp.sum(eq, axis=0, keepdims=True)

        lax.fori_loop(0, n_ch, tie_body, jnp.zeros((1, BLOCK), F32))

    pad_rows = ckv_ref.shape[-1] - rank - POS_ROWS
    q_aug = jnp.concatenate([heads(qlt_ref, rank), qpos_ref[0],
                             jnp.zeros((pad_rows, nh * BLOCK), BF16)], axis=0)
    acc_ref[...] = jnp.zeros(acc_ref.shape, F32)
    pair = 2 * BLOCK
    n_pairs = nh * BLOCK // pair
    col_slices = [slice(pp * pair, (pp + 1) * pair) for pp in range(n_pairs)]

    cat = lambda parts: jnp.concatenate(parts, axis=-1)

    def logits_stage(j, buf):
        mask = jnp.where(sc_ref[pl.ds(k_start(j), chunk), :] >= tau, 0.0, NEG)
        mask2 = jnp.concatenate([mask, mask], axis=-1)
        kc = ckv_ref[0, pl.ds(k_start(j), chunk), :]
        cmax = []
        for cols in col_slices:
            s = jnp.dot(kc, q_aug[:, cols], preferred_element_type=F32) + mask2
            buf[:, cols] = s
            cmax.append(jnp.max(s, axis=0, keepdims=True))
        return cat(cmax)

    ones_rows = jnp.ones((2 * SUBLANES, chunk), BF16)

    def value_stage(j, buf, m_use, alpha):
        kct = jnp.concatenate([ckvt_ref[0, :, pl.ds(k_start(j), chunk)], ones_rows], axis=0)
        sums = []
        for cols in col_slices:
            p = jnp.exp2(buf[:, cols] - m_use[:, cols])
            pv = jnp.dot(kct, p.astype(BF16), preferred_element_type=F32)
            acc_ref[:, cols] = alpha[:, cols] * acc_ref[:, cols] + pv[:rank]
            sums.append(pv[rank:rank + 1])
        return cat(sums)

    def finish(j, buf_cur, carry):
        m_cur, alpha_cur, l_prev = carry
        return alpha_cur * l_prev + value_stage(j, buf_cur, m_cur, alpha_cur)

    def advance(j, buf_next, buf_cur, carry):
        m_cur = carry[0]
        m_next = jnp.maximum(m_cur, logits_stage(j + 1, buf_next))
        return m_next, jnp.exp2(m_cur - m_next), finish(j, buf_cur, carry)

    def double_step(t, carry):
        carry = advance(2 * t, s2_ref, s_ref, carry)
        return advance(2 * t + 1, s_ref, s2_ref, carry)

    zeros_row = jnp.zeros((1, nh * BLOCK), F32)
    n_double = (n_ch - 1) // 2
    carry = lax.fori_loop(0, n_double, double_step, (logits_stage(0, s_ref), zeros_row, zeros_row))
    last = 2 * n_double

    @pl.when(last == n_ch - 1)
    def _():
        l_ref[...] = finish(last, s_ref, carry)

    @pl.when(last < n_ch - 1)
    def _():
        l_ref[...] = finish(last + 1, s2_ref, advance(last, s2_ref, s_ref, carry))

    l_fin = l_ref[...]

    o_lat_t = acc_ref[...] / l_fin
    outs = [jnp.dot(o_lat_t[:, hh * BLOCK:(hh + 1) * BLOCK].T.astype(BF16), wuv_ref[hh],
                    preferred_element_type=F32) for hh in range(nh)]
    o_ref[0] = jnp.concatenate(outs, axis=-1).astype(BF16)


def _alibi_query_table(slopes, n_blocks):
    sl = np.asarray(slopes, np.float64) * LOG2E
    t0 = np.arange(n_blocks, dtype=np.float64)[:, None] * BLOCK
    coef = [np.broadcast_to(sl * POS_RADIX, (n_blocks, sl.size)),
            np.broadcast_to(sl, (n_blocks, sl.size)),
            -sl[None, :] * t0]
    rows = []
    for cf in coef:
        hi = jnp.asarray(cf, F32).astype(BF16)
        lo = (jnp.asarray(cf, F32) - hi.astype(F32)).astype(BF16)
        rows += [hi, lo]
    tab = jnp.stack(rows + [jnp.zeros_like(rows[0])] * (POS_ROWS - len(rows)), axis=1)
    return jnp.repeat(tab, BLOCK, axis=-1)


def _dsa_call(slopes, qit, wit, ki, qlt, ckv, ckvt, w_uv, *, chunk=256):
    bsz, seq, ckv_w = ckv.shape
    rank = ckvt.shape[1]
    nh, _, hd = w_uv.shape
    topk = min(TOPK_MAX, seq // 4)
    assert seq % chunk == 0 and seq // POS_RADIX <= 256
    tri = jnp.asarray(np.tril(np.ones((chunk, chunk), np.float32)), BF16)
    qpos = _alibi_query_table(slopes, seq // BLOCK)
    return pl.pallas_call(
        functools.partial(_dsa_kernel, topk=topk, chunk=chunk),
        grid=(bsz, seq // BLOCK),
        in_specs=[pl.BlockSpec((1, IDX_HEADS * IDX_DIM, BLOCK), lambda b, i: (b, 0, i)),
                  pl.BlockSpec((1, IDX_HEADS, BLOCK), lambda b, i: (b, 0, i)),
                  pl.BlockSpec((1, seq, IDX_DIM), lambda b, i: (b, 0, 0)),
                  pl.BlockSpec((1, nh * rank, BLOCK), lambda b, i: (b, 0, i)),
                  pl.BlockSpec((1, POS_ROWS, nh * BLOCK), lambda b, i: (i, 0, 0)),
                  pl.BlockSpec((1, seq, ckv_w), lambda b, i: (b, 0, 0)),
                  pl.BlockSpec((1, rank, seq), lambda b, i: (b, 0, 0)),
                  _const_spec((nh, rank, hd)),
                  _const_spec((chunk, chunk))],
        out_specs=pl.BlockSpec((1, BLOCK, nh * hd), lambda b, i: (b, i, 0)),
        out_shape=jax.ShapeDtypeStruct((bsz, seq, nh * hd), BF16),
        scratch_shapes=[pltpu.VMEM((seq, BLOCK), F32),
                        pltpu.VMEM((rank, nh * BLOCK), F32),
                        pltpu.VMEM((chunk, nh * BLOCK), F32),
                        pltpu.VMEM((chunk, nh * BLOCK), F32),
                        pltpu.VMEM((1, nh * BLOCK), F32)],
        compiler_params=_cparams(("parallel", "arbitrary")),
        name="sparse_attn",
    )(qit, wit, ki, qlt, qpos, ckv, ckvt, w_uv, tri)


def _out_kernel(x_ref, oa_ref, ob_ref, mod_ref, wo_ref, g2_ref, wgu_ref, wd_ref, gf_ref, o_ref,
                *, d_ff, ff_chunks, final_norm):
    mixed = jnp.concatenate([oa_ref[0], ob_ref[0]], axis=-1)
    x1 = x_ref[0] + mod_ref[0, 2:3, :] * jnp.dot(mixed, wo_ref[...], preferred_element_type=F32)
    h2 = _rms(x1) * g2_ref[...]
    h2 = (h2 * (1.0 + mod_ref[0, 4:5, :]) + mod_ref[0, 3:4, :]).astype(BF16)
    fc = d_ff // ff_chunks
    ffn = jnp.zeros(x1.shape, F32)
    for cc in range(ff_chunks):
        gate = jnp.dot(h2, wgu_ref[:, cc * fc:(cc + 1) * fc], preferred_element_type=F32)
        up = jnp.dot(h2, wgu_ref[:, d_ff + cc * fc:d_ff + (cc + 1) * fc], preferred_element_type=F32)
        act = gate * (1.0 / (1.0 + jnp.exp(-gate))) * up
        ffn = ffn + jnp.dot(act.astype(BF16), wd_ref[cc * fc:(cc + 1) * fc, :], preferred_element_type=F32)
    x2 = x1 + mod_ref[0, 5:6, :] * ffn
    if final_norm:
        x2 = _rms(x2) * gf_ref[...]
    o_ref[0] = x2


def _out_call(x, oa, ob, mod, w_out, g_ffn, w_gu, w_down, g_final, *, tm, final_norm):
    bsz, seq, d = x.shape
    d_ff = w_down.shape[0]
    ff_chunks = 2 if d_ff % (2 * LANES) == 0 else 1
    row = lambda w: pl.BlockSpec((1, tm, w), lambda b, i: (b, i, 0))
    return pl.pallas_call(
        functools.partial(_out_kernel, d_ff=d_ff, ff_chunks=ff_chunks, final_norm=final_norm),
        grid=(bsz, seq // tm),
        in_specs=[row(d), row(oa.shape[-1]), row(ob.shape[-1]),
                  pl.BlockSpec((1, 6, d), lambda b, i: (b, 0, 0)),
                  _const_spec((d, d)), _const_spec((1, d)),
                  _const_spec((d, 2 * d_ff)), _const_spec((d_ff, d)), _const_spec((1, d))],
        out_specs=row(d),
        out_shape=jax.ShapeDtypeStruct((bsz, seq, d), F32),
        compiler_params=_cparams(("parallel", "parallel")),
        name="out_proj_ffn",
    )(x, oa, ob, mod, w_out, g_ffn, w_gu, w_down, g_final)


def _alibi_slopes(n_heads):
    s = 2.0 ** (-8.0 * (np.arange(n_heads, dtype=np.float32) + 1.0) / n_heads)
    return s[0::2], s[1::2]


def kernel(x, c, w_ada, b_ada, g_attn, w_in, kv_norm_g, w_uk, w_uv, w_out, g_ffn, w_gu, w_down, g_final):
    bsz, seq, d = x.shape
    depth = w_ada.shape[0]
    nh = w_uk.shape[1]
    slopes_a, slopes_b = _alibi_slopes(2 * nh)
    d_in = w_in.shape[-1]
    n_pad = -(-d_in // LANES) * LANES
    tm = 512 if seq % 512 == 0 else BLOCK
    for l in range(depth):
        mod = _mod_call(c, w_ada[l], b_ada[l]).reshape(bsz, 6, d)
        w_in_p = jnp.pad(w_in[l], ((0, 0), (0, n_pad - d_in))).astype(BF16)
        qa, ka, va, qlt, ckv, ckvt, qit, ki, wit = _in_call(
            x, mod, g_attn[l].reshape(1, d), w_in_p, w_uk[l].astype(BF16),
            kv_norm_g[l].reshape(1, -1), nh=nh, tm=tm)
        out_a = _dil_call(slopes_a, qa, ka, va)
        out_b = _dsa_call(slopes_b, qit, wit, ki, qlt, ckv, ckvt, w_uv[l].astype(BF16))
        x = _out_call(x, out_a, out_b, mod, w_out[l].astype(BF16), g_ffn[l].reshape(1, d),
                      w_gu[l].astype(BF16), w_down[l].astype(BF16), g_final.reshape(1, d),
                      tm=tm, final_norm=(l == depth - 1))
    return x
```

```python
import functools

import numpy as np
import jax
import jax.numpy as jnp
from jax import lax
from jax.experimental import pallas as pl
from jax.experimental.pallas import tpu as pltpu

F32 = jnp.float32
BF16 = jnp.bfloat16

HEAD_DIM = 64
BLOCK = 128
DILATED_CONFIGS = ((128, 1), (512, 4), (2048, 16))
IDX_HEADS = 8
IDX_DIM = 64
TOPK_MAX = 256
EPS = 1e-6
NEG = -1e30
F32_MIN = float(np.finfo(np.float32).min)
LOG2E = float(np.log2(np.e))
POS_RADIX = 64
POS_ROWS = 16
LANES = 128
SUBLANES = 8
COUNT_CHAINS = 8
VMEM_LIMIT = 56 * 1024 * 1024


def _cparams(sem):
    return pltpu.CompilerParams(dimension_semantics=sem, vmem_limit_bytes=VMEM_LIMIT)


def _const_spec(shape):
    nd = len(shape)
    return pl.BlockSpec(shape, lambda *_: (0,) * nd, pipeline_mode=pl.Buffered(1))


def _mod_kernel(c_ref, w_ref, b_ref, o_ref):
    c = c_ref[...]
    c_act = c * (1.0 / (1.0 + jnp.exp(-c)))
    o_ref[...] = jnp.dot(c_act, w_ref[...], preferred_element_type=F32,
                         precision=lax.Precision.HIGHEST) + b_ref[...]


def _mod_call(c, w_ada, b_ada):
    bsz, d = c.shape
    n = w_ada.shape[1]
    tn = 1024
    return pl.pallas_call(
        _mod_kernel,
        grid=(n // tn,),
        in_specs=[pl.BlockSpec((bsz, d), lambda j: (0, 0)),
                  pl.BlockSpec((d, tn), lambda j: (0, j)),
                  pl.BlockSpec((1, tn), lambda j: (0, j))],
        out_specs=pl.BlockSpec((bsz, tn), lambda j: (0, j)),
        out_shape=jax.ShapeDtypeStruct((bsz, n), F32),
        compiler_params=_cparams(("arbitrary",)),
        name="adaln_mod",
    )(c, w_ada, b_ada.reshape(1, n))


def _rms(x):
    return x * lax.rsqrt(jnp.mean(x * x, axis=-1, keepdims=True) + EPS)


def _in_kernel(x_ref, mod_ref, g_ref, w_ref, wuk_ref, kvg_ref,
               qa_ref, ka_ref, va_ref, qlt_ref, ckv_ref, ckvt_ref, qit_ref, ki_ref, wit_ref,
               *, nh, idx_scale):
    hd = HEAD_DIM
    h = _rms(x_ref[0]) * g_ref[...]
    h = h * (1.0 + mod_ref[0, 1:2, :]) + mod_ref[0, 0:1, :]
    proj = jnp.dot(h.astype(BF16), w_ref[...], preferred_element_type=F32)
    wa = nh * hd
    rank = ckvt_ref.shape[1]
    tm = x_ref.shape[1]
    o_qb, o_ckv = 3 * wa, 4 * wa
    o_qi = o_ckv + rank
    o_ki = o_qi + IDX_HEADS * IDX_DIM
    qa_ref[0] = proj[:, 0:wa] * (hd ** -0.5 * LOG2E)
    ka_ref[0] = proj[:, wa:2 * wa]
    va_ref[0] = proj[:, 2 * wa:3 * wa]
    for hh in range(nh):
        qb_h = proj[:, o_qb + hh * hd:o_qb + (hh + 1) * hd].astype(BF16)
        q_lat = jnp.dot(qb_h, wuk_ref[hh], preferred_element_type=F32) * (hd ** -0.5 * LOG2E)
        qlt_ref[0, hh * rank:(hh + 1) * rank, :] = q_lat.T.astype(BF16)
    ckv = (_rms(proj[:, o_ckv:o_ckv + rank]) * kvg_ref[...])
    ckvt_ref[0] = ckv.T.astype(BF16)
    t = pl.program_id(1) * tm + lax.broadcasted_iota(jnp.int32, (tm, LANES), 0)
    lane = lax.broadcasted_iota(jnp.int32, (tm, LANES), 1)
    pos = jnp.where(lane < 2, t // POS_RADIX, jnp.where(lane < 4, t % POS_RADIX, jnp.where(lane < 6, 1, 0)))
    ckv_ref[0] = jnp.concatenate([ckv, pos.astype(F32)], axis=-1).astype(BF16)
    for pp in range(IDX_HEADS * IDX_DIM // LANES):
        qit_ref[0, pp * LANES:(pp + 1) * LANES, :] = proj[:, o_qi + pp * LANES:o_qi + (pp + 1) * LANES].T.astype(BF16)
    kiw = proj[:, o_ki:o_ki + LANES]
    ki_ref[0] = kiw[:, :IDX_DIM].astype(BF16)
    wit_ref[0] = kiw.T[IDX_DIM:IDX_DIM + IDX_HEADS, :] * idx_scale


def _in_call(x, mod, g, w_in_p, w_uk, kvg, *, nh, tm):
    bsz, seq, d = x.shape
    n_pad = w_in_p.shape[1]
    rank = w_uk.shape[-1]
    hd = HEAD_DIM
    wa = nh * hd
    idx_scale = float((IDX_HEADS * IDX_DIM) ** -0.5)
    row_spec = lambda w: pl.BlockSpec((1, tm, w), lambda b, i: (b, i, 0))
    col_spec = lambda r: pl.BlockSpec((1, r, tm), lambda b, i: (b, 0, i))
    return pl.pallas_call(
        functools.partial(_in_kernel, nh=nh, idx_scale=idx_scale),
        grid=(bsz, seq // tm),
        in_specs=[pl.BlockSpec((1, tm, d), lambda b, i: (b, i, 0)),
                  pl.BlockSpec((1, 6, d), lambda b, i: (b, 0, 0)),
                  _const_spec((1, d)),
                  _const_spec((d, n_pad)),
                  _const_spec((nh, hd, rank)),
                  _const_spec((1, rank))],
        out_specs=[row_spec(wa), row_spec(wa), row_spec(wa),
                   col_spec(nh * rank), row_spec(rank + LANES), col_spec(rank),
                   col_spec(IDX_HEADS * IDX_DIM), row_spec(IDX_DIM), col_spec(IDX_HEADS)],
        out_shape=[jax.ShapeDtypeStruct((bsz, seq, wa), F32),
                   jax.ShapeDtypeStruct((bsz, seq, wa), F32),
                   jax.ShapeDtypeStruct((bsz, seq, wa), F32),
                   jax.ShapeDtypeStruct((bsz, nh * rank, seq), BF16),
                   jax.ShapeDtypeStruct((bsz, seq, rank + LANES), BF16),
                   jax.ShapeDtypeStruct((bsz, rank, seq), BF16),
                   jax.ShapeDtypeStruct((bsz, IDX_HEADS * IDX_DIM, seq), BF16),
                   jax.ShapeDtypeStruct((bsz, seq, IDX_DIM), BF16),
                   jax.ShapeDtypeStruct((bsz, IDX_HEADS, seq), F32)],
        compiler_params=_cparams(("parallel", "parallel")),
        name="in_proj",
    )(x, mod, g, w_in_p, w_uk, kvg)


def _band_tables(tq, kw, w_res):
    r = np.arange(tq)[None, :]
    c = np.arange(kw)[:, None]
    clean = lambda d: np.where((d >= 0) & (d <= w_res), d, -1).astype(np.float32)
    return jnp.asarray(np.stack([clean(r - c), clean(r - c + w_res)]))


def _dil_kernel(slope_ref, q_ref, k_ref, v_ref, *refs, w_res, plan):
    n_tab = 1 + max(t for _, _, t in plan)
    dtab_refs = refs[:n_tab]
    o_ref, m_ref, l_ref, acc_ref, bias_ref, s_ref, mloc_ref = refs[n_tab:]
    seq = q_ref.shape[1]
    hp = pl.program_id(1)
    feat = lax.broadcasted_iota(jnp.int32, (LANES, 1), 0)
    first = feat < HEAD_DIM
    keep = (first, jnp.logical_not(first))

    for branch, (dil, tq, tab) in enumerate(plan):
        fresh = branch == 0
        dtab = dtab_refs[tab]
        kw = dtab.shape[1]
        nt = seq // dil // tq
        for hh in range(2):
            neg_slope = -slope_ref[hp * 2 + hh] * (float(dil) * LOG2E)
            for off in range(2):
                bias_ref[off, hh, :kw, :tq] = jnp.where(dtab[off] >= 0.0, neg_slope * dtab[off], NEG)

        def rows(start, size, dil=dil):
            return pl.ds(start, size) if dil == 1 else pl.ds(start, size, stride=dil)

        def coords(w, dil=dil, tq=tq, nt=nt):
            r = w // nt
            jt = w - r * nt
            q0 = jt * (tq * dil) + r
            k0 = jnp.maximum(jt * tq - w_res, 0) * dil + r
            return q0, k0, jnp.minimum(jt, 1)

        def logits_stage(w, rows=rows, coords=coords, tq=tq, kw=kw):
            q0, k0, off = coords(w)
            q_t = q_ref[0, rows(q0, tq), :].T
            k2 = k_ref[0, rows(k0, kw), :].astype(BF16)
            for hh in range(2):
                qh = jnp.where(keep[hh], q_t, 0.0).astype(BF16)
                s = jnp.dot(k2, qh, preferred_element_type=F32) + bias_ref[off, hh, :kw, :tq]
                s_ref[hh, :kw, :tq] = s
                mloc_ref[hh, :, :tq] = jnp.max(s, axis=0, keepdims=True)

        def value_stage(w, rows=rows, coords=coords, tq=tq, kw=kw, fresh=fresh):
            q0, k0, _ = coords(w)
            v_t = v_ref[0, rows(k0, kw), :].T.astype(BF16)
            ms = [mloc_ref[hh, :, :tq] for hh in range(2)]
            ps = [jnp.exp2(s_ref[hh, :kw, :tq] - ms[hh]) for hh in range(2)]
            ls = [jnp.sum(p, axis=0, keepdims=True) for p in ps]
            outs = [jnp.dot(v_t, p.astype(BF16), preferred_element_type=F32) for p in ps]
            pick = lambda pair: jnp.where(first, pair[0], pair[1]).T
            spread = lambda pair: [jnp.broadcast_to(x, (LANES, tq)) for x in pair]
            m_t, l_t, o_t = pick(spread(ms)), pick(spread(ls)), pick(outs)
            q_rows = rows(q0, tq)
            if fresh:
                acc_ref[q_rows, :] = o_t
                l_ref[q_rows, :] = l_t
                m_ref[q_rows, :] = m_t
                return
            m_old = m_ref[q_rows, :]
            m_new = jnp.maximum(m_old, m_t)
            a_old = jnp.exp2(m_old - m_new)
            a_new = jnp.exp2(m_t - m_new)
            acc_ref[q_rows, :] = a_old * acc_ref[q_rows, :] + a_new * o_t
            l_ref[q_rows, :] = a_old * l_ref[q_rows, :] + a_new * l_t
            m_ref[q_rows, :] = m_new

        def pipe_body(w, carry, logits_stage=logits_stage, value_stage=value_stage):
            value_stage(w - 1)
            logits_stage(w)
            return carry

        n_tiles = dil * nt
        logits_stage(0)
        lax.fori_loop(1, n_tiles, pipe_body, 0)
        value_stage(n_tiles - 1)

    o_ref[0] = (acc_ref[...] / l_ref[...]).astype(BF16)


def _dil_call(slopes, qa, ka, va):
    bsz, seq, wa = qa.shape
    w_res = DILATED_CONFIGS[0][0] // DILATED_CONFIGS[0][1]
    assert all(w // d == w_res for w, d in DILATED_CONFIGS)
    tile_shapes, plan = [], []
    for _, dil in DILATED_CONFIGS:
        n = seq // dil
        tq = min(256, n)
        kw = tq if n == tq else tq + w_res
        assert n % tq == 0 and tq >= w_res and kw <= n
        if (tq, kw) not in tile_shapes:
            tile_shapes.append((tq, kw))
        plan.append((dil, tq, tile_shapes.index((tq, kw))))
    tables = [_band_tables(tq, kw, w_res) for tq, kw in tile_shapes]
    tq_max = max(tq for tq, _ in tile_shapes)
    kw_max = max(kw for _, kw in tile_shapes)
    blk = pl.BlockSpec((1, seq, LANES), lambda b, h: (b, 0, h))
    return pl.pallas_call(
        functools.partial(_dil_kernel, w_res=w_res, plan=tuple(plan)),
        grid=(bsz, wa // LANES),
        in_specs=[pl.BlockSpec(memory_space=pltpu.SMEM), blk, blk, blk] + [_const_spec(t.shape) for t in tables],
        out_specs=blk,
        out_shape=jax.ShapeDtypeStruct((bsz, seq, wa), BF16),
        scratch_shapes=[pltpu.VMEM((seq, LANES), F32), pltpu.VMEM((seq, LANES), F32),
                        pltpu.VMEM((seq, LANES), F32),
                        pltpu.VMEM((2, 2, kw_max, tq_max), F32),
                        pltpu.VMEM((2, kw_max, tq_max), F32),
                        pltpu.VMEM((2, 1, tq_max), F32)],
        compiler_params=_cparams(("parallel", "parallel")),
        name="dilated_attn",
    )(jnp.asarray(slopes, F32), qa, ka, va, *tables)


def _key_to_f32(key):
    bits = key ^ (lax.shift_right_arithmetic(key, 31) & jnp.int32(0x7FFFFFFF))
    return lax.bitcast_convert_type(bits, F32)


def _dsa_kernel(qit_ref, wit_ref, ki_ref, qlt_ref, qpos_ref, ckv_ref, ckvt_ref, wuv_ref, tri_ref, o_ref,
                sc_ref, acc_ref, s_ref, s2_ref, l_ref, *, topk, chunk):
    i = pl.program_id(1)
    t0 = i * BLOCK
    rank = ckvt_ref.shape[1]
    nh = qlt_ref.shape[1] // rank
    n_ch = (t0 + BLOCK + chunk - 1) // chunk
    kk = lax.broadcasted_iota(jnp.int32, (chunk, BLOCK), 0)
    qq = lax.broadcasted_iota(jnp.int32, (chunk, BLOCK), 1)
    rel = qq - kk
    k_start = lambda j: pl.multiple_of(j * chunk, chunk)
    heads = lambda ref, width: jnp.concatenate(
        [ref[0, hh * width:(hh + 1) * width, :] for hh in range(ref.shape[1] // width)], axis=-1)

    qit = heads(qit_ref, IDX_DIM)
    wt = wit_ref[0]

    idx_pair = 2 * BLOCK
    idx_cols = [slice(pp * idx_pair, (pp + 1) * idx_pair) for pp in range(IDX_HEADS * BLOCK // idx_pair)]

    def idx_logits(j, buf):
        kc = ki_ref[0, pl.ds(k_start(j), chunk), :]
        for cols in idx_cols:
            buf[:, cols] = jnp.dot(kc, qit[:, cols], preferred_element_type=F32)

    def idx_scores(j, buf):
        k0 = k_start(j)
        score = jnp.zeros((chunk, BLOCK), F32)
        for pp, cols in enumerate(idx_cols):
            lg = buf[:, cols]
            score = score + (wt[2 * pp:2 * pp + 1, :] * jnp.maximum(lg[:, :BLOCK], 0.0)
                             + wt[2 * pp + 1:2 * pp + 2, :] * jnp.maximum(lg[:, BLOCK:], 0.0))
        score = jnp.where(score == 0.0, 0.0, score)
        sc_ref[pl.ds(k0, chunk), :] = jnp.where(rel + (t0 - k0) >= 0, score, -jnp.inf)

    def idx_advance(j, buf_next, buf_cur):
        idx_logits(j + 1, buf_next)
        idx_scores(j, buf_cur)

    def idx_double_step(t, carry):
        idx_advance(2 * t, s2_ref, s_ref)
        idx_advance(2 * t + 1, s_ref, s2_ref)
        return carry

    idx_logits(0, s_ref)
    n_double = (n_ch - 1) // 2
    lax.fori_loop(0, n_double, idx_double_step, 0)
    last = 2 * n_double

    @pl.when(last == n_ch - 1)
    def _():
        idx_scores(last, s_ref)

    @pl.when(last < n_ch - 1)
    def _():
        idx_advance(last, s2_ref, s_ref)
        idx_scores(last + 1, s2_ref)

    def count(pred):
        def body(j, cnt):
            hit = jnp.where(pred(sc_ref[pl.ds(k_start(j), chunk), :]), 1.0, 0.0)
            return cnt + jnp.sum(hit.reshape(-1, COUNT_CHAINS, SUBLANES, BLOCK), axis=0)

        def two_chunks(t, cnt):
            return body(2 * t + 1, body(2 * t, cnt))

        cnt = lax.fori_loop(0, n_ch // 2, two_chunks, jnp.zeros((COUNT_CHAINS, SUBLANES, BLOCK), F32))
        cnt = lax.fori_loop(2 * (n_ch // 2), n_ch, body, cnt)
        parts = [cnt[c] for c in range(COUNT_CHAINS)]
        while len(parts) > 1:
            parts = [parts[c] + parts[c + 1] for c in range(0, len(parts), 2)]
        return jnp.sum(parts[0], axis=0, keepdims=True)

    kf = float(topk)
    n_all = (n_ch * chunk).astype(F32)
    n_pos = count(lambda s: s >= 0.0)
    pos_ok = n_pos >= kf
    key = jnp.where(pos_ok, jnp.int32(0), jnp.int32(-2 ** 31))
    n_key = jnp.where(pos_ok, n_pos, n_all)

    def bit_body(b, carry):
        key, n_key = carry
        trial = key | lax.shift_left(jnp.int32(1), 30 - b)
        cand = _key_to_f32(trial)
        n_trial = count(lambda s: s >= cand)
        ok = n_trial >= kf
        return jnp.where(ok, trial, key), jnp.where(ok, n_trial, n_key)

    tq = t0 + lax.broadcasted_iota(jnp.int32, (1, BLOCK), 1)
    few = tq < topk - 1
    key, n_key = lax.fori_loop(0, 31, bit_body, (key, n_key))
    tau = jnp.where(few, F32_MIN, _key_to_f32(key))

    @pl.when(jnp.max(jnp.where(few, 0.0, n_key - kf)) > 0.0)
    def _():
        n_tie = kf - count(lambda s: s > tau)

        def tie_body(j, seen):
            k0 = k_start(j)
            s = sc_ref[pl.ds(k0, chunk), :]
            eq = jnp.where(s == tau, 1.0, 0.0)
            rank_eq = seen + jnp.dot(tri_ref[...], eq.astype(BF16), preferred_element_type=F32)
            drop = jnp.where(rank_eq > n_tie, eq, 0.0)
            sc_ref[pl.ds(k0, chunk), :] = jnp.where(drop > 0.0, -jnp.inf, s)
            return seen + jnp.sum(eq, axis=0, keepdims=True)

        lax.fori_loop(0, n_ch, tie_body, jnp.zeros((1, BLOCK), F32))

    pad_rows = ckv_ref.shape[-1] - rank - POS_ROWS
    q_aug = jnp.concatenate([heads(qlt_ref, rank), qpos_ref[0],
                             jnp.zeros((pad_rows, nh * BLOCK), BF16)], axis=0)
    acc_ref[...] = jnp.zeros(acc_ref.shape, F32)
    pair = 2 * BLOCK
    n_pairs = nh * BLOCK // pair
    col_slices = [slice(pp * pair, (pp + 1) * pair) for pp in range(n_pairs)]

    cat = lambda parts: jnp.concatenate(parts, axis=-1)

    def logits_stage(j, buf):
        mask = jnp.where(sc_ref[pl.ds(k_start(j), chunk), :] >= tau, 0.0, NEG)
        mask2 = jnp.concatenate([mask, mask], axis=-1)
        kc = ckv_ref[0, pl.ds(k_start(j), chunk), :]
        cmax = []
        for cols in col_slices:
            s = jnp.dot(kc, q_aug[:, cols], preferred_element_type=F32) + mask2
            buf[:, cols] = s
            cmax.append(jnp.max(s, axis=0, keepdims=True))
        return cat(cmax)

    ones_rows = jnp.ones((2 * SUBLANES, chunk), BF16)

    def value_stage(j, buf, m_use, alpha):
        kct = jnp.concatenate([ckvt_ref[0, :, pl.ds(k_start(j), chunk)], ones_rows], axis=0)
        sums = []
        for cols in col_slices:
            p = jnp.exp2(buf[:, cols] - m_use[:, cols])
            pv = jnp.dot(kct, p.astype(BF16), preferred_element_type=F32)
            acc_ref[:, cols] = alpha[:, cols] * acc_ref[:, cols] + pv[:rank]
            sums.append(pv[rank:rank + 1])
        return cat(sums)

    def finish(j, buf_cur, carry):
        m_cur, alpha_cur, l_prev = carry
        return alpha_cur * l_prev + value_stage(j, buf_cur, m_cur, alpha_cur)

    def advance(j, buf_next, buf_cur, carry):
        m_cur = carry[0]
        m_next = jnp.maximum(m_cur, logits_stage(j + 1, buf_next))
        return m_next, jnp.exp2(m_cur - m_next), finish(j, buf_cur, carry)

    def double_step(t, carry):
        carry = advance(2 * t, s2_ref, s_ref, carry)
        return advance(2 * t + 1, s_ref, s2_ref, carry)

    zeros_row = jnp.zeros((1, nh * BLOCK), F32)
    n_double = (n_ch - 1) // 2
    carry = lax.fori_loop(0, n_double, double_step, (logits_stage(0, s_ref), zeros_row, zeros_row))
    last = 2 * n_double

    @pl.when(last == n_ch - 1)
    def _():
        l_ref[...] = finish(last, s_ref, carry)

    @pl.when(last < n_ch - 1)
    def _():
        l_ref[...] = finish(last + 1, s2_ref, advance(last, s2_ref, s_ref, carry))

    l_fin = l_ref[...]

    o_lat_t = acc_ref[...] / l_fin
    outs = [jnp.dot(o_lat_t[:, hh * BLOCK:(hh + 1) * BLOCK].T.astype(BF16), wuv_ref[hh],
                    preferred_element_type=F32) for hh in range(nh)]
    o_ref[0] = jnp.concatenate(outs, axis=-1).astype(BF16)


def _alibi_query_table(slopes, n_blocks):
    sl = np.asarray(slopes, np.float64) * LOG2E
    t0 = np.arange(n_blocks, dtype=np.float64)[:, None] * BLOCK
    coef = [np.broadcast_to(sl * POS_RADIX, (n_blocks, sl.size)),
            np.broadcast_to(sl, (n_blocks, sl.size)),
            -sl[None, :] * t0]
    rows = []
    for cf in coef:
        hi = jnp.asarray(cf, F32).astype(BF16)
        lo = (jnp.asarray(cf, F32) - hi.astype(F32)).astype(BF16)
        rows += [hi, lo]
    tab = jnp.stack(rows + [jnp.zeros_like(rows[0])] * (POS_ROWS - len(rows)), axis=1)
    return jnp.repeat(tab, BLOCK, axis=-1)


def _dsa_call(slopes, qit, wit, ki, qlt, ckv, ckvt, w_uv, *, chunk=256):
    bsz, seq, ckv_w = ckv.shape
    rank = ckvt.shape[1]
    nh, _, hd = w_uv.shape
    topk = min(TOPK_MAX, seq // 4)
    assert seq % chunk == 0 and seq // POS_RADIX <= 256
    tri = jnp.asarray(np.tril(np.ones((chunk, chunk), np.float32)), BF16)
    qpos = _alibi_query_table(slopes, seq // BLOCK)
    return pl.pallas_call(
        functools.partial(_dsa_kernel, topk=topk, chunk=chunk),
        grid=(bsz, seq // BLOCK),
        in_specs=[pl.BlockSpec((1, IDX_HEADS * IDX_DIM, BLOCK), lambda b, i: (b, 0, i)),
                  pl.BlockSpec((1, IDX_HEADS, BLOCK), lambda b, i: (b, 0, i)),
                  pl.BlockSpec((1, seq, IDX_DIM), lambda b, i: (b, 0, 0)),
                  pl.BlockSpec((1, nh * rank, BLOCK), lambda b, i: (b, 0, i)),
                  pl.BlockSpec((1, POS_ROWS, nh * BLOCK), lambda b, i: (i, 0, 0)),
                  pl.BlockSpec((1, seq, ckv_w), lambda b, i: (b, 0, 0)),
                  pl.BlockSpec((1, rank, seq), lambda b, i: (b, 0, 0)),
                  _const_spec((nh, rank, hd)),
                  _const_spec((chunk, chunk))],
        out_specs=pl.BlockSpec((1, BLOCK, nh * hd), lambda b, i: (b, i, 0)),
        out_shape=jax.ShapeDtypeStruct((bsz, seq, nh * hd), BF16),
        scratch_shapes=[pltpu.VMEM((seq, BLOCK), F32),
                        pltpu.VMEM((rank, nh * BLOCK), F32),
                        pltpu.VMEM((chunk, nh * BLOCK), F32),
                        pltpu.VMEM((chunk, nh * BLOCK), F32),
                        pltpu.VMEM((1, nh * BLOCK), F32)],
        compiler_params=_cparams(("parallel", "arbitrary")),
        name="sparse_attn",
    )(qit, wit, ki, qlt, qpos, ckv, ckvt, w_uv, tri)


def _out_kernel(x_ref, oa_ref, ob_ref, mod_ref, wo_ref, g2_ref, wgu_ref, wd_ref, gf_ref, o_ref,
                *, d_ff, ff_chunks, final_norm):
    mixed = jnp.concatenate([oa_ref[0], ob_ref[0]], axis=-1)
    x1 = x_ref[0] + mod_ref[0, 2:3, :] * jnp.dot(mixed, wo_ref[...], preferred_element_type=F32)
    h2 = _rms(x1) * g2_ref[...]
    h2 = (h2 * (1.0 + mod_ref[0, 4:5, :]) + mod_ref[0, 3:4, :]).astype(BF16)
    fc = d_ff // ff_chunks
    ffn = jnp.zeros(x1.shape, F32)
    for cc in range(ff_chunks):
        gate = jnp.dot(h2, wgu_ref[:, cc * fc:(cc + 1) * fc], preferred_element_type=F32)
        up = jnp.dot(h2, wgu_ref[:, d_ff + cc * fc:d_ff + (cc + 1) * fc], preferred_element_type=F32)
        act = gate * (1.0 / (1.0 + jnp.exp(-gate))) * up
        ffn = ffn + jnp.dot(act.astype(BF16), wd_ref[cc * fc:(cc + 1) * fc, :], preferred_element_type=F32)
    x2 = x1 + mod_ref[0, 5:6, :] * ffn
    if final_norm:
        x2 = _rms(x2) * gf_ref[...]
    o_ref[0] = x2


def _out_call(x, oa, ob, mod, w_out, g_ffn, w_gu, w_down, g_final, *, tm, final_norm):
    bsz, seq, d = x.shape
    d_ff = w_down.shape[0]
    ff_chunks = 2 if d_ff % (2 * LANES) == 0 else 1
    row = lambda w: pl.BlockSpec((1, tm, w), lambda b, i: (b, i, 0))
    return pl.pallas_call(
        functools.partial(_out_kernel, d_ff=d_ff, ff_chunks=ff_chunks, final_norm=final_norm),
        grid=(bsz, seq // tm),
        in_specs=[row(d), row(oa.shape[-1]), row(ob.shape[-1]),
                  pl.BlockSpec((1, 6, d), lambda b, i: (b, 0, 0)),
                  _const_spec((d, d)), _const_spec((1, d)),
                  _const_spec((d, 2 * d_ff)), _const_spec((d_ff, d)), _const_spec((1, d))],
        out_specs=row(d),
        out_shape=jax.ShapeDtypeStruct((bsz, seq, d), F32),
        compiler_params=_cparams(("parallel", "parallel")),
        name="out_proj_ffn",
    )(x, oa, ob, mod, w_out, g_ffn, w_gu, w_down, g_final)


def _alibi_slopes(n_heads):
    s = 2.0 ** (-8.0 * (np.arange(n_heads, dtype=np.float32) + 1.0) / n_heads)
    return s[0::2], s[1::2]


def kernel(x, c, w_ada, b_ada, g_attn, w_in, kv_norm_g, w_uk, w_uv, w_out, g_ffn, w_gu, w_down, g_final):
    bsz, seq, d = x.shape
    depth = w_ada.shape[0]
    nh = w_uk.shape[1]
    slopes_a, slopes_b = _alibi_slopes(2 * nh)
    d_in = w_in.shape[-1]
    n_pad = -(-d_in // LANES) * LANES
    tm = 512 if seq % 512 == 0 else BLOCK
    for l in range(depth):
        mod = _mod_call(c, w_ada[l], b_ada[l]).reshape(bsz, 6, d)
        w_in_p = jnp.pad(w_in[l], ((0, 0), (0, n_pad - d_in))).astype(BF16)
        qa, ka, va, qlt, ckv, ckvt, qit, ki, wit = _in_call(
            x, mod, g_attn[l].reshape(1, d), w_in_p, w_uk[l].astype(BF16),
            kv_norm_g[l].reshape(1, -1), nh=nh, tm=tm)
        out_a = _dil_call(slopes_a, qa, ka, va)
        out_b = _dsa_call(slopes_b, qit, wit, ki, qlt, ckv, ckvt, w_uv[l].astype(BF16))
        x = _out_call(x, out_a, out_b, mod, w_out[l].astype(BF16), g_ffn[l].reshape(1, d),
                      w_gu[l].astype(BF16), w_down[l].astype(BF16), g_final.reshape(1, d),
                      tm=tm, final_norm=(l == depth - 1))
    return x
```

```python
import functools

import numpy as np
import jax
import jax.numpy as jnp
from jax import lax
from jax.experimental import pallas as pl
from jax.experimental.pallas import tpu as pltpu

F32 = jnp.float32
BF16 = jnp.bfloat16

HEAD_DIM = 64
BLOCK = 128
DILATED_CONFIGS = ((128, 1), (512, 4), (2048, 16))
IDX_HEADS = 8
IDX_DIM = 64
TOPK_MAX = 256
EPS = 1e-6
NEG = -1e30
F32_MIN = float(np.finfo(np.float32).min)
LOG2E = float(np.log2(np.e))
POS_RADIX = 64
POS_ROWS = 16
LANES = 128
SUBLANES = 8
COUNT_CHAINS = 8
VMEM_LIMIT = 56 * 1024 * 1024


def _cparams(sem):
    return pltpu.CompilerParams(dimension_semantics=sem, vmem_limit_bytes=VMEM_LIMIT)


def _const_spec(shape):
    nd = len(shape)
    return pl.BlockSpec(shape, lambda *_: (0,) * nd, pipeline_mode=pl.Buffered(1))


def _mod_kernel(c_ref, w_ref, b_ref, o_ref):
    c = c_ref[...]
    c_act = c * (1.0 / (1.0 + jnp.exp(-c)))
    o_ref[...] = jnp.dot(c_act, w_ref[...], preferred_element_type=F32,
                         precision=lax.Precision.HIGHEST) + b_ref[...]


def _mod_call(c, w_ada, b_ada):
    bsz, d = c.shape
    n = w_ada.shape[1]
    tn = 1024
    return pl.pallas_call(
        _mod_kernel,
        grid=(n // tn,),
        in_specs=[pl.BlockSpec((bsz, d), lambda j: (0, 0)),
                  pl.BlockSpec((d, tn), lambda j: (0, j)),
                  pl.BlockSpec((1, tn), lambda j: (0, j))],
        out_specs=pl.BlockSpec((bsz, tn), lambda j: (0, j)),
        out_shape=jax.ShapeDtypeStruct((bsz, n), F32),
        compiler_params=_cparams(("arbitrary",)),
        name="adaln_mod",
    )(c, w_ada, b_ada.reshape(1, n))


def _rms(x):
    return x * lax.rsqrt(jnp.mean(x * x, axis=-1, keepdims=True) + EPS)


def _in_kernel(x_ref, mod_ref, g_ref, w_ref, wuk_ref, kvg_ref,
               qa_ref, ka_ref, va_ref, qlt_ref, ckv_ref, ckvt_ref, qit_ref, ki_ref, wit_ref,
               *, nh, idx_scale):
    hd = HEAD_DIM
    h = _rms(x_ref[0]) * g_ref[...]
    h = h * (1.0 + mod_ref[0, 1:2, :]) + mod_ref[0, 0:1, :]
    proj = jnp.dot(h.astype(BF16), w_ref[...], preferred_element_type=F32)
    wa = nh * hd
    rank = ckvt_ref.shape[1]
    tm = x_ref.shape[1]
    o_qb, o_ckv = 3 * wa, 4 * wa
    o_qi = o_ckv + rank
    o_ki = o_qi + IDX_HEADS * IDX_DIM
    qa_ref[0] = proj[:, 0:wa] * (hd ** -0.5 * LOG2E)
    ka_ref[0] = proj[:, wa:2 * wa]
    va_ref[0] = proj[:, 2 * wa:3 * wa]
    for hh in range(nh):
        qb_h = proj[:, o_qb + hh * hd:o_qb + (hh + 1) * hd].astype(BF16)
        q_lat = jnp.dot(qb_h, wuk_ref[hh], preferred_element_type=F32) * (hd ** -0.5 * LOG2E)
        qlt_ref[0, hh * rank:(hh + 1) * rank, :] = q_lat.T.astype(BF16)
    ckv = (_rms(proj[:, o_ckv:o_ckv + rank]) * kvg_ref[...])
    ckvt_ref[0] = ckv.T.astype(BF16)
    t = pl.program_id(1) * tm + lax.broadcasted_iota(jnp.int32, (tm, LANES), 0)
    lane = lax.broadcasted_iota(jnp.int32, (tm, LANES), 1)
    pos = jnp.where(lane < 2, t // POS_RADIX, jnp.where(lane < 4, t % POS_RADIX, jnp.where(lane < 6, 1, 0)))
    ckv_ref[0] = jnp.concatenate([ckv, pos.astype(F32)], axis=-1).astype(BF16)
    for pp in range(IDX_HEADS * IDX_DIM // LANES):
        qit_ref[0, pp * LANES:(pp + 1) * LANES, :] = proj[:, o_qi + pp * LANES:o_qi + (pp + 1) * LANES].T.astype(BF16)
    kiw = proj[:, o_ki:o_ki + LANES]
    ki_ref[0] = kiw[:, :IDX_DIM].astype(BF16)
    wit_ref[0] = kiw.T[IDX_DIM:IDX_DIM + IDX_HEADS, :] * idx_scale


def _in_call(x, mod, g, w_in_p, w_uk, kvg, *, nh, tm):
    bsz, seq, d = x.shape
    n_pad = w_in_p.shape[1]
    rank = w_uk.shape[-1]
    hd = HEAD_DIM
    wa = nh * hd
    idx_scale = float((IDX_HEADS * IDX_DIM) ** -0.5)
    row_spec = lambda w: pl.BlockSpec((1, tm, w), lambda b, i: (b, i, 0))
    col_spec = lambda r: pl.BlockSpec((1, r, tm), lambda b, i: (b, 0, i))
    return pl.pallas_call(
        functools.partial(_in_kernel, nh=nh, idx_scale=idx_scale),
        grid=(bsz, seq // tm),
        in_specs=[pl.BlockSpec((1, tm, d), lambda b, i: (b, i, 0)),
                  pl.BlockSpec((1, 6, d), lambda b, i: (b, 0, 0)),
                  _const_spec((1, d)),
                  _const_spec((d, n_pad)),
                  _const_spec((nh, hd, rank)),
                  _const_spec((1, rank))],
        out_specs=[row_spec(wa), row_spec(wa), row_spec(wa),
                   col_spec(nh * rank), row_spec(rank + LANES), col_spec(rank),
                   col_spec(IDX_HEADS * IDX_DIM), row_spec(IDX_DIM), col_spec(IDX_HEADS)],
        out_shape=[jax.ShapeDtypeStruct((bsz, seq, wa), F32),
                   jax.ShapeDtypeStruct((bsz, seq, wa), F32),
                   jax.ShapeDtypeStruct((bsz, seq, wa), F32),
                   jax.ShapeDtypeStruct((bsz, nh * rank, seq), BF16),
                   jax.ShapeDtypeStruct((bsz, seq, rank + LANES), BF16),
                   jax.ShapeDtypeStruct((bsz, rank, seq), BF16),
                   jax.ShapeDtypeStruct((bsz, IDX_HEADS * IDX_DIM, seq), BF16),
                   jax.ShapeDtypeStruct((bsz, seq, IDX_DIM), BF16),
                   jax.ShapeDtypeStruct((bsz, IDX_HEADS, seq), F32)],
        compiler_params=_cparams(("parallel", "parallel")),
        name="in_proj",
    )(x, mod, g, w_in_p, w_uk, kvg)


def _band_tables(tq, kw, w_res):
    r = np.arange(tq)[None, :]
    c = np.arange(kw)[:, None]
    clean = lambda d: np.where((d >= 0) & (d <= w_res), d, -1).astype(np.float32)
    return jnp.asarray(np.stack([clean(r - c), clean(r - c + w_res)]))


def _dil_kernel(slope_ref, q_ref, k_ref, v_ref, *refs, w_res, plan):
    n_tab = 1 + max(t for _, _, t in plan)
    dtab_refs = refs[:n_tab]
    o_ref, m_ref, l_ref, acc_ref, bias_ref, s_ref, mloc_ref = refs[n_tab:]
    seq = q_ref.shape[1]
    hp = pl.program_id(1)
    feat = lax.broadcasted_iota(jnp.int32, (LANES, 1), 0)
    first = feat < HEAD_DIM
    keep = (first, jnp.logical_not(first))

    for branch, (dil, tq, tab) in enumerate(plan):
        fresh = branch == 0
        dtab = dtab_refs[tab]
        kw = dtab.shape[1]
        nt = seq // dil // tq
        for hh in range(2):
            neg_slope = -slope_ref[hp * 2 + hh] * (float(dil) * LOG2E)
            for off in range(2):
                bias_ref[off, hh, :kw, :tq] = jnp.where(dtab[off] >= 0.0, neg_slope * dtab[off], NEG)

        def rows(start, size, dil=dil):
            return pl.ds(start, size) if dil == 1 else pl.ds(start, size, stride=dil)

        def coords(w, dil=dil, tq=tq, nt=nt):
            r = w // nt
            jt = w - r * nt
            q0 = jt * (tq * dil) + r
            k0 = jnp.maximum(jt * tq - w_res, 0) * dil + r
            return q0, k0, jnp.minimum(jt, 1)

        def logits_stage(w, rows=rows, coords=coords, tq=tq, kw=kw):
            q0, k0, off = coords(w)
            q_t = q_ref[0, rows(q0, tq), :].T
            k2 = k_ref[0, rows(k0, kw), :].astype(BF16)
            for hh in range(2):
                qh = jnp.where(keep[hh], q_t, 0.0).astype(BF16)
                s = jnp.dot(k2, qh, preferred_element_type=F32) + bias_ref[off, hh, :kw, :tq]
                s_ref[hh, :kw, :tq] = s
                mloc_ref[hh, :, :tq] = jnp.max(s, axis=0, keepdims=True)

        def value_stage(w, rows=rows, coords=coords, tq=tq, kw=kw, fresh=fresh):
            q0, k0, _ = coords(w)
            v_t = v_ref[0, rows(k0, kw), :].T.astype(BF16)
            ms = [mloc_ref[hh, :, :tq] for hh in range(2)]
            ps = [jnp.exp2(s_ref[hh, :kw, :tq] - ms[hh]) for hh in range(2)]
            ls = [jnp.sum(p, axis=0, keepdims=True) for p in ps]
            outs = [jnp.dot(v_t, p.astype(BF16), preferred_element_type=F32) for p in ps]
            pick = lambda pair: jnp.where(first, pair[0], pair[1]).T
            spread = lambda pair: [jnp.broadcast_to(x, (LANES, tq)) for x in pair]
            m_t, l_t, o_t = pick(spread(ms)), pick(spread(ls)), pick(outs)
            q_rows = rows(q0, tq)
            if fresh:
                acc_ref[q_rows, :] = o_t
                l_ref[q_rows, :] = l_t
                m_ref[q_rows, :] = m_t
                return
            m_old = m_ref[q_rows, :]
            m_new = jnp.maximum(m_old, m_t)
            a_old = jnp.exp2(m_old - m_new)
            a_new = jnp.exp2(m_t - m_new)
            acc_ref[q_rows, :] = a_old * acc_ref[q_rows, :] + a_new * o_t
            l_ref[q_rows, :] = a_old * l_ref[q_rows, :] + a_new * l_t
            m_ref[q_rows, :] = m_new

        def pipe_body(w, carry, logits_stage=logits_stage, value_stage=value_stage):
            value_stage(w - 1)
            logits_stage(w)
            return carry

        n_tiles = dil * nt
        logits_stage(0)
        lax.fori_loop(1, n_tiles, pipe_body, 0)
        value_stage(n_tiles - 1)

    o_ref[0] = (acc_ref[...] / l_ref[...]).astype(BF16)


def _dil_call(slopes, qa, ka, va):
    bsz, seq, wa = qa.shape
    w_res = DILATED_CONFIGS[0][0] // DILATED_CONFIGS[0][1]
    assert all(w // d == w_res for w, d in DILATED_CONFIGS)
    tile_shapes, plan = [], []
    for _, dil in DILATED_CONFIGS:
        n = seq // dil
        tq = min(256, n)
        kw = tq if n == tq else tq + w_res
        assert n % tq == 0 and tq >= w_res and kw <= n
        if (tq, kw) not in tile_shapes:
            tile_shapes.append((tq, kw))
        plan.append((dil, tq, tile_shapes.index((tq, kw))))
    tables = [_band_tables(tq, kw, w_res) for tq, kw in tile_shapes]
    tq_max = max(tq for tq, _ in tile_shapes)
    kw_max = max(kw for _, kw in tile_shapes)
    blk = pl.BlockSpec((1, seq, LANES), lambda b, h: (b, 0, h))
    return pl.pallas_call(
        functools.partial(_dil_kernel, w_res=w_res, plan=tuple(plan)),
        grid=(bsz, wa // LANES),
        in_specs=[pl.BlockSpec(memory_space=pltpu.SMEM), blk, blk, blk] + [_const_spec(t.shape) for t in tables],
        out_specs=blk,
        out_shape=jax.ShapeDtypeStruct((bsz, seq, wa), BF16),
        scratch_shapes=[pltpu.VMEM((seq, LANES), F32), pltpu.VMEM((seq, LANES), F32),
                        pltpu.VMEM((seq, LANES), F32),
                        pltpu.VMEM((2, 2, kw_max, tq_max), F32),
                        pltpu.VMEM((2, kw_max, tq_max), F32),
                        pltpu.VMEM((2, 1, tq_max), F32)],
        compiler_params=_cparams(("parallel", "parallel")),
        name="dilated_attn",
    )(jnp.asarray(slopes, F32), qa, ka, va, *tables)


def _key_to_f32(key):
    bits = key ^ (lax.shift_right_arithmetic(key, 31) & jnp.int32(0x7FFFFFFF))
    return lax.bitcast_convert_type(bits, F32)


def _dsa_kernel(qit_ref, wit_ref, ki_ref, qlt_ref, qpos_ref, ckv_ref, ckvt_ref, wuv_ref, tri_ref, o_ref,
                sc_ref, acc_ref, s_ref, s2_ref, l_ref, qidx_ref, qaug_ref, *, topk, chunk):
    i = pl.program_id(1)
    t0 = i * BLOCK
    rank = ckvt_ref.shape[1]
    nh = qlt_ref.shape[1] // rank
    n_ch = (t0 + BLOCK + chunk - 1) // chunk
    kk = lax.broadcasted_iota(jnp.int32, (chunk, BLOCK), 0)
    qq = lax.broadcasted_iota(jnp.int32, (chunk, BLOCK), 1)
    rel = qq - kk
    k_start = lambda j: pl.multiple_of(j * chunk, chunk)
    heads = lambda ref, width: jnp.concatenate(
        [ref[0, hh * width:(hh + 1) * width, :] for hh in range(ref.shape[1] // width)], axis=-1)

    qidx_ref[...] = heads(qit_ref, IDX_DIM)
    qit = qidx_ref
    wt = wit_ref[0]

    idx_pair = 2 * BLOCK
    idx_cols = [slice(pp * idx_pair, (pp + 1) * idx_pair) for pp in range(IDX_HEADS * BLOCK // idx_pair)]

    def idx_logits(j, buf):
        kc = ki_ref[0, pl.ds(k_start(j), chunk), :]
        for cols in idx_cols:
            buf[:, cols] = jnp.dot(kc, qit[:, cols], preferred_element_type=F32)

    def idx_scores(j, buf):
        k0 = k_start(j)
        score = jnp.zeros((chunk, BLOCK), F32)
        for pp, cols in enumerate(idx_cols):
            lg = buf[:, cols]
            score = score + (wt[2 * pp:2 * pp + 1, :] * jnp.maximum(lg[:, :BLOCK], 0.0)
                             + wt[2 * pp + 1:2 * pp + 2, :] * jnp.maximum(lg[:, BLOCK:], 0.0))
        score = jnp.where(score == 0.0, 0.0, score)
        sc_ref[pl.ds(k0, chunk), :] = jnp.where(rel + (t0 - k0) >= 0, score, -jnp.inf)

    def idx_advance(j, buf_next, buf_cur):
        idx_logits(j + 1, buf_next)
        idx_scores(j, buf_cur)

    def idx_double_step(t, carry):
        idx_advance(2 * t, s2_ref, s_ref)
        idx_advance(2 * t + 1, s_ref, s2_ref)
        return carry

    idx_logits(0, s_ref)
    n_double = (n_ch - 1) // 2
    lax.fori_loop(0, n_double, idx_double_step, 0)
    last = 2 * n_double

    @pl.when(last == n_ch - 1)
    def _():
        idx_scores(last, s_ref)

    @pl.when(last < n_ch - 1)
    def _():
        idx_advance(last, s2_ref, s_ref)
        idx_scores(last + 1, s2_ref)

    def count(pred):
        def body(j, cnt):
            hit = jnp.where(pred(sc_ref[pl.ds(k_start(j), chunk), :]), 1.0, 0.0)
            return cnt + jnp.sum(hit.reshape(-1, COUNT_CHAINS, SUBLANES, BLOCK), axis=0)

        def two_chunks(t, cnt):
            return body(2 * t + 1, body(2 * t, cnt))

        cnt = lax.fori_loop(0, n_ch // 2, two_chunks, jnp.zeros((COUNT_CHAINS, SUBLANES, BLOCK), F32))
        cnt = lax.fori_loop(2 * (n_ch // 2), n_ch, body, cnt)
        parts = [cnt[c] for c in range(COUNT_CHAINS)]
        while len(parts) > 1:
            parts = [parts[c] + parts[c + 1] for c in range(0, len(parts), 2)]
        return jnp.sum(parts[0], axis=0, keepdims=True)

    kf = float(topk)
    n_all = (n_ch * chunk).astype(F32)
    n_pos = count(lambda s: s >= 0.0)
    pos_ok = n_pos >= kf
    key = jnp.where(pos_ok, jnp.int32(0), jnp.int32(-2 ** 31))
    n_key = jnp.where(pos_ok, n_pos, n_all)

    def bit_body(b, carry):
        key, n_key = carry
        trial = key | lax.shift_left(jnp.int32(1), 30 - b)
        cand = _key_to_f32(trial)
        n_trial = count(lambda s: s >= cand)
        ok = n_trial >= kf
        return jnp.where(ok, trial, key), jnp.where(ok, n_trial, n_key)

    tq = t0 + lax.broadcasted_iota(jnp.int32, (1, BLOCK), 1)
    few = tq < topk - 1
    key, n_key = lax.fori_loop(0, 31, bit_body, (key, n_key))
    tau = jnp.where(few, F32_MIN, _key_to_f32(key))

    @pl.when(jnp.max(jnp.where(few, 0.0, n_key - kf)) > 0.0)
    def _():
        n_tie = kf - count(lambda s: s > tau)

        def tie_body(j, seen):
            k0 = k_start(j)
            s = sc_ref[pl.ds(k0, chunk), :]
            eq = jnp.where(s == tau, 1.0, 0.0)
            rank_eq = seen + jnp.dot(tri_ref[...], eq.astype(BF16), preferred_element_type=F32)
            drop = jnp.where(rank_eq > n_tie, eq, 0.0)
            sc_ref[pl.ds(k0, chunk), :] = jnp.where(drop > 0.0, -jnp.inf, s)
            return seen + jnp.sum(eq, axis=0, keepdims=True)

        lax.fori_loop(0, n_ch, tie_body, jnp.zeros((1, BLOCK), F32))

    pad_rows = ckv_ref.shape[-1] - rank - POS_ROWS
    qaug_ref[...] = jnp.concatenate([heads(qlt_ref, rank), qpos_ref[0],
                                     jnp.zeros((pad_rows, nh * BLOCK), BF16)], axis=0)
    q_aug = qaug_ref
    acc_ref[...] = jnp.zeros(acc_ref.shape, F32)
    pair = 2 * BLOCK
    n_pairs = nh * BLOCK // pair
    col_slices = [slice(pp * pair, (pp + 1) * pair) for pp in range(n_pairs)]

    cat = lambda parts: jnp.concatenate(parts, axis=-1)

    def logits_stage(j, buf):
        mask = jnp.where(sc_ref[pl.ds(k_start(j), chunk), :] >= tau, 0.0, NEG)
        mask2 = jnp.concatenate([mask, mask], axis=-1)
        kc = ckv_ref[0, pl.ds(k_start(j), chunk), :]
        cmax = []
        for cols in col_slices:
            s = jnp.dot(kc, q_aug[:, cols], preferred_element_type=F32) + mask2
            buf[:, cols] = s
            cmax.append(jnp.max(s, axis=0, keepdims=True))
        return cat(cmax)

    ones_rows = jnp.ones((2 * SUBLANES, chunk), BF16)

    def value_stage(j, buf, m_use, alpha):
        kct = jnp.concatenate([ckvt_ref[0, :, pl.ds(k_start(j), chunk)], ones_rows], axis=0)
        sums = []
        for cols in col_slices:
            p = jnp.exp2(buf[:, cols] - m_use[:, cols])
            pv = jnp.dot(kct, p.astype(BF16), preferred_element_type=F32)
            acc_ref[:, cols] = alpha[:, cols] * acc_ref[:, cols] + pv[:rank]
            sums.append(pv[rank:rank + 1])
        return cat(sums)

    def finish(j, buf_cur, carry):
        m_cur, alpha_cur, l_prev = carry
        return alpha_cur * l_prev + value_stage(j, buf_cur, m_cur, alpha_cur)

    def advance(j, buf_next, buf_cur, carry):
        m_cur = carry[0]
        m_next = jnp.maximum(m_cur, logits_stage(j + 1, buf_next))
        return m_next, jnp.exp2(m_cur - m_next), finish(j, buf_cur, carry)

    def double_step(t, carry):
        carry = advance(2 * t, s2_ref, s_ref, carry)
        return advance(2 * t + 1, s_ref, s2_ref, carry)

    zeros_row = jnp.zeros((1, nh * BLOCK), F32)
    n_double = (n_ch - 1) // 2
    carry = lax.fori_loop(0, n_double, double_step, (logits_stage(0, s_ref), zeros_row, zeros_row))
    last = 2 * n_double

    @pl.when(last == n_ch - 1)
    def _():
        l_ref[...] = finish(last, s_ref, carry)

    @pl.when(last < n_ch - 1)
    def _():
        l_ref[...] = finish(last + 1, s2_ref, advance(last, s2_ref, s_ref, carry))

    l_fin = l_ref[...]

    o_lat_t = acc_ref[...] / l_fin
    outs = [jnp.dot(o_lat_t[:, hh * BLOCK:(hh + 1) * BLOCK].T.astype(BF16), wuv_ref[hh],
                    preferred_element_type=F32) for hh in range(nh)]
    o_ref[0] = jnp.concatenate(outs, axis=-1).astype(BF16)


def _alibi_query_table(slopes, n_blocks):
    sl = np.asarray(slopes, np.float64) * LOG2E
    t0 = np.arange(n_blocks, dtype=np.float64)[:, None] * BLOCK
    coef = [np.broadcast_to(sl * POS_RADIX, (n_blocks, sl.size)),
            np.broadcast_to(sl, (n_blocks, sl.size)),
            -sl[None, :] * t0]
    rows = []
    for cf in coef:
        hi = jnp.asarray(cf, F32).astype(BF16)
        lo = (jnp.asarray(cf, F32) - hi.astype(F32)).astype(BF16)
        rows += [hi, lo]
    tab = jnp.stack(rows + [jnp.zeros_like(rows[0])] * (POS_ROWS - len(rows)), axis=1)
    return jnp.repeat(tab, BLOCK, axis=-1)


def _dsa_call(slopes, qit, wit, ki, qlt, ckv, ckvt, w_uv, *, chunk=256):
    bsz, seq, ckv_w = ckv.shape
    rank = ckvt.shape[1]
    nh, _, hd = w_uv.shape
    topk = min(TOPK_MAX, seq // 4)
    assert seq % chunk == 0 and seq // POS_RADIX <= 256
    tri = jnp.asarray(np.tril(np.ones((chunk, chunk), np.float32)), BF16)
    qpos = _alibi_query_table(slopes, seq // BLOCK)
    return pl.pallas_call(
        functools.partial(_dsa_kernel, topk=topk, chunk=chunk),
        grid=(bsz, seq // BLOCK),
        in_specs=[pl.BlockSpec((1, IDX_HEADS * IDX_DIM, BLOCK), lambda b, i: (b, 0, i)),
                  pl.BlockSpec((1, IDX_HEADS, BLOCK), lambda b, i: (b, 0, i)),
                  pl.BlockSpec((1, seq, IDX_DIM), lambda b, i: (b, 0, 0)),
                  pl.BlockSpec((1, nh * rank, BLOCK), lambda b, i: (b, 0, i)),
                  pl.BlockSpec((1, POS_ROWS, nh * BLOCK), lambda b, i: (i, 0, 0)),
                  pl.BlockSpec((1, seq, ckv_w), lambda b, i: (b, 0, 0)),
                  pl.BlockSpec((1, rank, seq), lambda b, i: (b, 0, 0)),
                  _const_spec((nh, rank, hd)),
                  _const_spec((chunk, chunk))],
        out_specs=pl.BlockSpec((1, BLOCK, nh * hd), lambda b, i: (b, i, 0)),
        out_shape=jax.ShapeDtypeStruct((bsz, seq, nh * hd), BF16),
        scratch_shapes=[pltpu.VMEM((seq, BLOCK), F32),
                        pltpu.VMEM((rank, nh * BLOCK), F32),
                        pltpu.VMEM((chunk, nh * BLOCK), F32),
                        pltpu.VMEM((chunk, nh * BLOCK), F32),
                        pltpu.VMEM((1, nh * BLOCK), F32),
                        pltpu.VMEM((IDX_DIM, IDX_HEADS * BLOCK), BF16),
                        pltpu.VMEM((ckv_w, nh * BLOCK), BF16)],
        compiler_params=_cparams(("parallel", "arbitrary")),
        name="sparse_attn",
    )(qit, wit, ki, qlt, qpos, ckv, ckvt, w_uv, tri)


def _out_kernel(x_ref, oa_ref, ob_ref, mod_ref, wo_ref, g2_ref, wgu_ref, wd_ref, gf_ref, o_ref,
                *, d_ff, ff_chunks, final_norm):
    mixed = jnp.concatenate([oa_ref[0], ob_ref[0]], axis=-1)
    x1 = x_ref[0] + mod_ref[0, 2:3, :] * jnp.dot(mixed, wo_ref[...], preferred_element_type=F32)
    h2 = _rms(x1) * g2_ref[...]
    h2 = (h2 * (1.0 + mod_ref[0, 4:5, :]) + mod_ref[0, 3:4, :]).astype(BF16)
    fc = d_ff // ff_chunks
    ffn = jnp.zeros(x1.shape, F32)
    for cc in range(ff_chunks):
        gate = jnp.dot(h2, wgu_ref[:, cc * fc:(cc + 1) * fc], preferred_element_type=F32)
        up = jnp.dot(h2, wgu_ref[:, d_ff + cc * fc:d_ff + (cc + 1) * fc], preferred_element_type=F32)
        act = gate * (1.0 / (1.0 + jnp.exp(-gate))) * up
        ffn = ffn + jnp.dot(act.astype(BF16), wd_ref[cc * fc:(cc + 1) * fc, :], preferred_element_type=F32)
    x2 = x1 + mod_ref[0, 5:6, :] * ffn
    if final_norm:
        x2 = _rms(x2) * gf_ref[...]
    o_ref[0] = x2


def _out_call(x, oa, ob, mod, w_out, g_ffn, w_gu, w_down, g_final, *, tm, final_norm):
    bsz, seq, d = x.shape
    d_ff = w_down.shape[0]
    ff_chunks = 2 if d_ff % (2 * LANES) == 0 else 1
    row = lambda w: pl.BlockSpec((1, tm, w), lambda b, i: (b, i, 0))
    return pl.pallas_call(
        functools.partial(_out_kernel, d_ff=d_ff, ff_chunks=ff_chunks, final_norm=final_norm),
        grid=(bsz, seq // tm),
        in_specs=[row(d), row(oa.shape[-1]), row(ob.shape[-1]),
                  pl.BlockSpec((1, 6, d), lambda b, i: (b, 0, 0)),
                  _const_spec((d, d)), _const_spec((1, d)),
                  _const_spec((d, 2 * d_ff)), _const_spec((d_ff, d)), _const_spec((1, d))],
        out_specs=row(d),
        out_shape=jax.ShapeDtypeStruct((bsz, seq, d), F32),
        compiler_params=_cparams(("parallel", "parallel")),
        name="out_proj_ffn",
    )(x, oa, ob, mod, w_out, g_ffn, w_gu, w_down, g_final)


def _alibi_slopes(n_heads):
    s = 2.0 ** (-8.0 * (np.arange(n_heads, dtype=np.float32) + 1.0) / n_heads)
    return s[0::2], s[1::2]


def kernel(x, c, w_ada, b_ada, g_attn, w_in, kv_norm_g, w_uk, w_uv, w_out, g_ffn, w_gu, w_down, g_final):
    bsz, seq, d = x.shape
    depth = w_ada.shape[0]
    nh = w_uk.shape[1]
    slopes_a, slopes_b = _alibi_slopes(2 * nh)
    d_in = w_in.shape[-1]
    n_pad = -(-d_in // LANES) * LANES
    tm = 512 if seq % 512 == 0 else BLOCK
    for l in range(depth):
        mod = _mod_call(c, w_ada[l], b_ada[l]).reshape(bsz, 6, d)
        w_in_p = jnp.pad(w_in[l], ((0, 0), (0, n_pad - d_in))).astype(BF16)
        qa, ka, va, qlt, ckv, ckvt, qit, ki, wit = _in_call(
            x, mod, g_attn[l].reshape(1, d), w_in_p, w_uk[l].astype(BF16),
            kv_norm_g[l].reshape(1, -1), nh=nh, tm=tm)
        out_a = _dil_call(slopes_a, qa, ka, va)
        out_b = _dsa_call(slopes_b, qit, wit, ki, qlt, ckv, ckvt, w_uv[l].astype(BF16))
        x = _out_call(x, out_a, out_b, mod, w_out[l].astype(BF16), g_ffn[l].reshape(1, d),
                      w_gu[l].astype(BF16), w_down[l].astype(BF16), g_final.reshape(1, d),
                      tm=tm, final_norm=(l == depth - 1))
    return x
```
